```python
import jax, jax.numpy as jnp
from jax import lax
import numpy as np

D_MODEL = 1024
BATCH = 16
SEQ = 2048
DEPTH = 2

MIX_WIDTH = D_MODEL
MLSTM_WIDTH = D_MODEL // 2
MLSTM_HEADS = 4
MLSTM_HEAD_DIM = MLSTM_WIDTH // MLSTM_HEADS
CHUNK = 64
CONV_WIDTH = 4
POOL_WIDTH = MIX_WIDTH - MLSTM_WIDTH
POOL_WINDOWS = (2, 4, 8, 16)
POOL_GROUP = POOL_WIDTH // len(POOL_WINDOWS)
N_IN = 4 * MLSTM_WIDTH + 2 * MLSTM_HEADS + POOL_WIDTH
MEM_LEN = 256
XATTN_HEADS = 4
XATTN_HEAD_DIM = D_MODEL // XATTN_HEADS
N_EXPERTS = 16
N_GROUPS = 4
EXPERTS_PER_GROUP = N_EXPERTS // N_GROUPS
TOP_K = 2
D_EXPERT = D_MODEL // 4
MOE_BLOCK = 128
DEEPNORM_ALPHA = (2 * DEPTH) ** 0.25
DEEPNORM_BETA = (8 * DEPTH) ** -0.25
LN_EPS = 1e-5

kernel_name = "hybrid_mlstm_pool_memxattn_groupmoe_deepnorm"


def layer_norm(x, g, b):
    xf = x.astype(jnp.float32)
    mu = xf.mean(-1, keepdims=True)
    var = jnp.square(xf - mu).mean(-1, keepdims=True)
    return ((xf - mu) * lax.rsqrt(var + LN_EPS) * g + b).astype(x.dtype)


def causal_depthwise_conv(u, w):
    S = u.shape[1]
    K = w.shape[0]
    up = jnp.pad(u, ((0, 0), (K - 1, 0), (0, 0)))
    return sum(up[:, j:j + S] * w[j] for j in range(K))


def mlstm_chunkwise(q, k, v, i_pre, f_pre):
    B, S, H, dh = q.shape
    L = CHUNK
    NC = S // L

    def chunk(t):
        t = t.reshape((B, NC, L, H) + t.shape[3:])
        return jnp.moveaxis(t, 3, 1)

    q = chunk(q.astype(jnp.float32)) * dh ** -0.5
    k = chunk(k.astype(jnp.float32))
    v = chunk(v.astype(jnp.float32))
    ig = chunk(i_pre.astype(jnp.float32))
    b = jnp.cumsum(jax.nn.log_sigmoid(chunk(f_pre.astype(jnp.float32))), axis=-1)
    b_last = b[..., -1]

    w_state = b_last[..., None] - b + ig
    m_loc = w_state.max(-1)
    a = jnp.exp(w_state - m_loc[..., None])
    C_loc = jnp.einsum('bhcl,bhcld,bhcle->bhcde', a, k, v)
    n_loc = jnp.einsum('bhcl,bhcld->bhcd', a, k)

    def step(carry, inp):
        C, n, m = carry
        Cl, nl, ml, bl = inp
        m_new = jnp.maximum(bl + m, ml)
        s_old = jnp.exp(bl + m - m_new)
        s_new = jnp.exp(ml - m_new)
        C_new = s_old[..., None, None] * C + s_new[..., None, None] * Cl
        n_new = s_old[..., None] * n + s_new[..., None] * nl
        return (C_new, n_new, m_new), (C, n, m)

    init = (jnp.zeros((B, H, dh, dh), jnp.float32),
            jnp.zeros((B, H, dh), jnp.float32),
            jnp.zeros((B, H), jnp.float32))
    xs = tuple(jnp.moveaxis(t, 2, 0) for t in (C_loc, n_loc, m_loc, b_last))
    _, (C_prev, n_prev, m_prev) = lax.scan(step, init, xs)
    C_prev = jnp.moveaxis(C_prev, 0, 2)
    n_prev = jnp.moveaxis(n_prev, 0, 2)
    m_prev = jnp.moveaxis(m_prev, 0, 2)

    causal = jnp.tril(jnp.ones((L, L), dtype=bool))
    log_d = jnp.where(causal, b[..., :, None] - b[..., None, :] + ig[..., None, :], -jnp.inf)
    log_inter = b + m_prev[..., None]
    m_t = jnp.maximum(log_d.max(-1), log_inter)
    p = jnp.exp(log_d - m_t[..., None]) * jnp.einsum('bhcld,bhcsd->bhcls', q, k)
    inter = jnp.exp(log_inter - m_t)
    num = (jnp.einsum('bhcls,bhcse->bhcle', p, v)
           + inter[..., None] * jnp.einsum('bhcld,bhcde->bhcle', q, C_prev))
    den = p.sum(-1) + inter * jnp.einsum('bhcld,bhcd->bhcl', q, n_prev)
    h = num / jnp.maximum(jnp.abs(den), jnp.exp(-m_t))[..., None]
    return jnp.moveaxis(h, 1, 3).reshape(B, S, H, dh)


def multiscale_pool(u):
    S = u.shape[1]
    pos = jnp.arange(1, S + 1, dtype=jnp.float32)[:, None]
    outs = []
    for g, w in enumerate(POOL_WINDOWS):
        ug = u[..., g * POOL_GROUP:(g + 1) * POOL_GROUP].astype(jnp.float32)
        cs = jnp.cumsum(ug, axis=1)
        lagged = jnp.pad(cs, ((0, 0), (w, 0), (0, 0)))[:, :S]
        outs.append((cs - lagged) / jnp.minimum(pos, w) - ug)
    return jnp.stack(outs, axis=2)


def hybrid_mixer(x, w_in, b_i, b_f, conv_qk, head_norm_g, pool_w, pool_scale, w_out):
    B, S, _ = x.shape
    M, H, dh = MLSTM_WIDTH, MLSTM_HEADS, MLSTM_HEAD_DIM
    z = x @ w_in
    qk = jax.nn.silu(causal_depthwise_conv(z[..., :2 * M], conv_qk))
    q = qk[..., :M].reshape(B, S, H, dh)
    k = qk[..., M:].reshape(B, S, H, dh)
    v = z[..., 2 * M:3 * M].reshape(B, S, H, dh)
    o_pre = z[..., 3 * M:4 * M]
    i_pre = z[..., 4 * M:4 * M + H] + b_i
    f_pre = z[..., 4 * M + H:4 * M + 2 * H] + b_f
    u = z[..., 4 * M + 2 * H:]

    h = mlstm_chunkwise(q, k, v, i_pre, f_pre)
    mu = h.mean(-1, keepdims=True)
    var = jnp.square(h - mu).mean(-1, keepdims=True)
    h = (h - mu) * lax.rsqrt(var + LN_EPS) * head_norm_g.reshape(H, dh).astype(jnp.float32)
    h = h.reshape(B, S, M) * jax.nn.sigmoid(o_pre.astype(jnp.float32))

    pooled = multiscale_pool(u)
    pm = jnp.einsum('bsgc,gcd->bsgd', pooled, pool_w.astype(jnp.float32))
    pm = pm.reshape(B, S, POOL_WIDTH) * pool_scale.astype(jnp.float32)

    mixed = jnp.concatenate([h, pm], axis=-1).astype(x.dtype)
    return mixed @ w_out


def memory_cross_attention(x, mem, w_q, w_kv, w_o):
    B, S, D = x.shape
    Mlen = mem.shape[1]
    q = (x @ w_q).reshape(B, S, XATTN_HEADS, XATTN_HEAD_DIM)
    kv = (mem @ w_kv).reshape(B, Mlen, 2, XATTN_HEADS, XATTN_HEAD_DIM)
    s = jnp.einsum('bshd,bmhd->bhsm', q, kv[:, :, 0]).astype(jnp.float32) * XATTN_HEAD_DIM ** -0.5
    p = jax.nn.softmax(s, axis=-1).astype(x.dtype)
    o = jnp.einsum('bhsm,bmhd->bshd', p, kv[:, :, 1]).reshape(B, S, D)
    return o @ w_o


def grouped_moe(x, router_w, router_bias, w_gate, w_up, w_down):
    B, S, D = x.shape
    T = B * S
    A = T * TOP_K
    xt = x.reshape(T, D)
    scores = jax.nn.sigmoid((xt @ router_w).astype(jnp.float32))
    sel = scores + router_bias.astype(jnp.float32)
    grp_score = lax.top_k(sel.reshape(T, N_GROUPS, EXPERTS_PER_GROUP), 2)[0].sum(-1)
    best = jnp.argmax(grp_score, axis=-1)
    in_grp = (jnp.arange(N_EXPERTS) // EXPERTS_PER_GROUP)[None, :] == best[:, None]
    _, idx = lax.top_k(jnp.where(in_grp, sel, -jnp.inf), TOP_K)
    gate = jnp.take_along_axis(scores, idx, axis=-1)
    gate = gate / gate.sum(-1, keepdims=True)

    e = idx.reshape(A).astype(jnp.int32)
    e_sorted, order = lax.sort((e, jnp.arange(A, dtype=jnp.int32)), num_keys=1)
    counts = jnp.bincount(e, length=N_EXPERTS)
    padded = (counts + MOE_BLOCK - 1) // MOE_BLOCK * MOE_BLOCK
    pend = jnp.cumsum(padded)
    pstart = pend - padded
    start = jnp.cumsum(counts) - counts
    dest_sorted = pstart[e_sorted] + jnp.arange(A, dtype=jnp.int32) - start[e_sorted]
    R = A + N_EXPERTS * MOE_BLOCK
    NB = R // MOE_BLOCK
    buf = jnp.zeros((R, D), x.dtype).at[dest_sorted].set(xt[order // TOP_K])
    blk_e = jnp.minimum(jnp.searchsorted(pend, jnp.arange(NB, dtype=jnp.int32) * MOE_BLOCK,
                                         side='right'), N_EXPERTS - 1)

    def expert_block(args):
        xb, ei = args
        hb = jax.nn.silu(xb @ w_gate[ei]) * (xb @ w_up[ei])
        return hb @ w_down[ei]

    yb = lax.map(expert_block, (buf.reshape(NB, MOE_BLOCK, D), blk_e)).reshape(R, D)
    dest = jnp.zeros((A,), jnp.int32).at[order].set(dest_sorted)
    y = (yb[dest].reshape(T, TOP_K, D) * gate[..., None].astype(x.dtype)).sum(1)
    return y.reshape(B, S, D)


def setup_inputs(seed: int = 0) -> dict:
    key = jax.random.key(seed)
    ks = jax.random.split(key, 24)
    f32 = jnp.float32
    D, L_ = D_MODEL, DEPTH
    beta = DEEPNORM_BETA
    nrm = lambda k, shape, s: jax.random.normal(k, shape, f32) * s
    col_scale = jnp.asarray(np.concatenate([
        np.ones(2 * MLSTM_WIDTH), np.full(MLSTM_WIDTH, beta), np.ones(MLSTM_WIDTH + 2 * MLSTM_HEADS),
        np.full(POOL_WIDTH, beta)]).astype(np.float32))
    kv_scale = jnp.asarray(np.concatenate([np.ones(D), np.full(D, beta)]).astype(np.float32))
    return {
        "x": nrm(ks[0], (BATCH, SEQ, D), 1.0),
        "mem": nrm(ks[1], (BATCH, MEM_LEN, D), 1.0),
        "w_in": nrm(ks[2], (L_, D, N_IN), D ** -0.5) * col_scale,
        "b_i": nrm(ks[3], (L_, MLSTM_HEADS), 0.1),
        "b_f": jnp.linspace(3.0, 6.0, MLSTM_HEADS, dtype=f32)[None, :] + nrm(ks[4], (L_, MLSTM_HEADS), 0.1),
        "conv_qk": nrm(ks[5], (L_, CONV_WIDTH, 2 * MLSTM_WIDTH), CONV_WIDTH ** -0.5),
        "head_norm_g": 1.0 + nrm(ks[6], (L_, MLSTM_WIDTH), 0.05),
        "pool_w": nrm(ks[7], (L_, len(POOL_WINDOWS), POOL_GROUP, POOL_GROUP), POOL_GROUP ** -0.5),
        "pool_scale": 1.0 + nrm(ks[8], (L_, POOL_WIDTH), 0.1),
        "w_mix_out": nrm(ks[9], (L_, MIX_WIDTH, D), MIX_WIDTH ** -0.5 * beta),
        "ln_mix_g": 1.0 + nrm(ks[10], (L_, D), 0.05),
        "ln_mix_b": nrm(ks[11], (L_, D), 0.02),
        "w_xq": nrm(ks[12], (L_, D, D), D ** -0.5),
        "w_xkv": nrm(ks[13], (L_, D, 2 * D), D ** -0.5) * kv_scale,
        "w_xo": nrm(ks[14], (L_, D, D), D ** -0.5 * beta),
        "ln_x_g": 1.0 + nrm(ks[15], (L_, D), 0.05),
        "ln_x_b": nrm(ks[16], (L_, D), 0.02),
        "router_w": nrm(ks[17], (D, N_EXPERTS), D ** -0.5),
        "router_bias": nrm(ks[18], (N_EXPERTS,), 0.01),
        "w_gate": nrm(ks[19], (L_, N_EXPERTS, D, D_EXPERT), D ** -0.5 * beta),
        "w_up": nrm(ks[20], (L_, N_EXPERTS, D, D_EXPERT), D ** -0.5 * beta),
        "w_down": nrm(ks[21], (L_, N_EXPERTS, D_EXPERT, D), D_EXPERT ** -0.5 * beta),
        "ln_moe_g": 1.0 + nrm(ks[22], (L_, D), 0.05),
        "ln_moe_b": nrm(ks[23], (L_, D), 0.02),
    }


def reference(x, mem, w_in, b_i, b_f, conv_qk, head_norm_g, pool_w, pool_scale, w_mix_out,
              ln_mix_g, ln_mix_b, w_xq, w_xkv, w_xo, ln_x_g, ln_x_b, router_w, router_bias,
              w_gate, w_up, w_down, ln_moe_g, ln_moe_b):
    for l in range(DEPTH):
        mix = hybrid_mixer(x, w_in[l], b_i[l], b_f[l], conv_qk[l], head_norm_g[l],
                           pool_w[l], pool_scale[l], w_mix_out[l])
        x = layer_norm(DEEPNORM_ALPHA * x + mix, ln_mix_g[l], ln_mix_b[l])
        xa = memory_cross_attention(x, mem, w_xq[l], w_xkv[l], w_xo[l])
        x = layer_norm(DEEPNORM_ALPHA * x + xa, ln_x_g[l], ln_x_b[l])
        ff = grouped_moe(x, router_w, router_bias, w_gate[l], w_up[l], w_down[l])
        x = layer_norm(DEEPNORM_ALPHA * x + ff, ln_moe_g[l], ln_moe_b[l])
    return x
```

```python
import functools

import jax
import jax.numpy as jnp
from jax import lax
from jax.experimental import pallas as pl
from jax.experimental.pallas import tpu as pltpu

F32 = jnp.float32
BF16 = jnp.bfloat16

N_HEADS = 4
HEAD_DIM = 128
POOL_WINDOWS = (2, 4, 8, 16)
POOL_GROUP = 128
CONV_WIDTH = 4
XATTN_HEADS = 4
N_EXPERTS = 16
N_GROUPS = 4
EXPERTS_PER_GROUP = 4
DEPTH = 2
ALPHA = (2 * DEPTH) ** 0.25
LN_EPS = 1e-5

LANES = 128
SUBLANES = 8
VMEM_LIMIT = 56 * 1024 * 1024

SEQ_TILE = 256
MLSTM_CHUNK = 256
FFN_BLOCK = 256
MOVE_CHUNK = 2048
GATE_LANES = 128
CONV_CARRY = 8
POOL_CARRY = 16


def _layer_norm(y, g, b):
    mu = jnp.mean(y, axis=-1, keepdims=True)
    d = y - mu
    var = jnp.mean(d * d, axis=-1, keepdims=True)
    return d * lax.rsqrt(var + LN_EPS) * g + b


def _sigmoid(v):
    return 1.0 / (1.0 + jnp.exp(-v))


def _dot(a, b):
    return jnp.dot(a, b, preferred_element_type=F32)


def _dot_nt(a, b):
    return lax.dot_general(a, b, (((1,), (1,)), ((), ())), preferred_element_type=F32)


def _split3(v):
    hi = v.astype(BF16)
    r1 = v - hi.astype(F32)
    mid = r1.astype(BF16)
    lo = (r1 - mid.astype(F32)).astype(BF16)
    return hi, mid, lo


def _mixer_kernel(x_ref, wa_ref, wu_ref, wif_ref, bif_ref, conv_ref, hng_ref, poolw_ref,
                  pscale_ref, wout_ref, lng_ref, lnb_ref, o_ref,
                  zq_ext, u_ext, c_st, n_st, m_st, *, ts, lc):
    s = pl.program_id(1)
    mw = N_HEADS * HEAD_DIM

    @pl.when(s == 0)
    def _():
        zq_ext[0:CONV_CARRY, :] = jnp.zeros((CONV_CARRY, 2 * mw), F32)
        u_ext[0:POOL_CARRY, :] = jnp.zeros((POOL_CARRY, u_ext.shape[1]), F32)
        c_st[...] = jnp.zeros(c_st.shape, F32)
        n_st[...] = jnp.zeros(n_st.shape, F32)
        m_st[...] = jnp.zeros(m_st.shape, F32)

    x = x_ref[0]
    xb = x.astype(BF16)
    z = _dot(xb, wa_ref[...])
    u = _dot(xb, wu_ref[...])
    gts = _dot(xb, wif_ref[...]) + bif_ref[...]

    zq_ext[CONV_CARRY:CONV_CARRY + ts, :] = z[:, :2 * mw]
    cw = conv_ref[...]
    acc = zq_ext[CONV_CARRY:CONV_CARRY + ts, :] * cw[CONV_WIDTH - 1:CONV_WIDTH, :]
    for j in range(1, CONV_WIDTH):
        acc = acc + zq_ext[CONV_CARRY - j:CONV_CARRY - j + ts, :] * cw[CONV_WIDTH - 1 - j:CONV_WIDTH - j, :]
    zq_ext[0:CONV_CARRY, :] = zq_ext[ts:ts + CONV_CARRY, :]
    qk = acc * _sigmoid(acc)
    q_all = qk[:, :mw] * (HEAD_DIM ** -0.5)
    k_all = qk[:, mw:]
    v_all = z[:, 2 * mw:3 * mw]
    o_all = z[:, 3 * mw:4 * mw]

    lf_all = jnp.minimum(gts, 0.0) - jnp.log1p(jnp.exp(-jnp.abs(gts)))

    row_i = lax.broadcasted_iota(jnp.int32, (lc, lc), 0)
    col_i = lax.broadcasted_iota(jnp.int32, (lc, lc), 1)
    causal = col_i <= row_i
    tri = jnp.where(causal, 1.0, 0.0).astype(BF16)

    head_out = [[] for _ in range(N_HEADS)]
    for c in range(ts // lc):
        rows = slice(c * lc, (c + 1) * lc)
        hi, mid, lo = _split3(lf_all[rows, :])
        b_all = _dot(tri, hi) + _dot(tri, mid) + _dot(tri, lo)
        g_c = gts[rows, :]
        r_all = g_c - pltpu.roll(b_all, LANES - N_HEADS, 1)
        r_t = r_all.T
        for h in range(N_HEADS):
            hs = slice(h * HEAD_DIM, (h + 1) * HEAD_DIM)
            qh = q_all[rows, hs]
            kh = k_all[rows, hs]
            vh = v_all[rows, hs].astype(BF16)
            qhb = qh.astype(BF16)
            bc = b_all[:, N_HEADS + h:N_HEADS + h + 1]
            igc = g_c[:, h:h + 1]
            r_row = r_t[h:h + 1, :]
            c_prev = c_st[h]
            n_prev = n_st[h]
            m_prev = m_st[h][:, 0:1]

            log_d = jnp.where(causal, bc + r_row, -jnp.inf)
            m_intra = jnp.max(log_d, axis=1, keepdims=True)
            log_inter = bc + m_prev
            m_t = jnp.maximum(m_intra, log_inter)
            p = jnp.exp(log_d - m_t) * _dot_nt(qhb, kh.astype(BF16))
            inter = jnp.exp(log_inter - m_t)
            num = _dot(p.astype(BF16), vh) + inter * _dot(qhb, c_prev.astype(BF16))
            den = (jnp.sum(p, axis=1, keepdims=True)
                   + inter * jnp.sum(qh * n_prev, axis=1, keepdims=True))
            hh = num / jnp.maximum(jnp.abs(den), jnp.exp(-m_t))

            b_last = bc[lc - 1:lc, :]
            w_state = b_last - bc + igc
            m_loc = jnp.max(w_state, axis=0, keepdims=True)
            ka = kh * jnp.exp(w_state - m_loc)
            c_loc = _dot(ka.T.astype(BF16), vh)
            n_loc = jnp.sum(ka, axis=0, keepdims=True)
            m_new = jnp.maximum(b_last + m_prev, m_loc)
            s_old = jnp.exp(b_last + m_prev - m_new)
            s_new = jnp.exp(m_loc - m_new)
            c_st[h] = s_old * c_prev + s_new * c_loc
            n_st[h] = s_old * n_prev + s_new * n_loc
            m_st[h] = jnp.broadcast_to(m_new, (1, LANES))

            mu = jnp.mean(hh, axis=1, keepdims=True)
            dlt = hh - mu
            var = jnp.mean(dlt * dlt, axis=1, keepdims=True)
            hn = dlt * lax.rsqrt(var + LN_EPS) * hng_ref[:, hs]
            head_out[h].append(hn * _sigmoid(o_all[rows, hs]))

    mixed = [jnp.concatenate(ho, axis=0) if len(ho) > 1 else ho[0] for ho in head_out]

    u_ext[POOL_CARRY:POOL_CARRY + ts, :] = u
    pos = (lax.broadcasted_iota(jnp.int32, (ts, 1), 0) + s * ts + 1).astype(F32)
    for g, w in enumerate(POOL_WINDOWS):
        cs = slice(g * POOL_GROUP, (g + 1) * POOL_GROUP)
        ug = u_ext[POOL_CARRY:POOL_CARRY + ts, cs]
        win = ug
        for j in range(1, w):
            win = win + u_ext[POOL_CARRY - j:POOL_CARRY - j + ts, cs]
        pooled = win / jnp.minimum(pos, float(w)) - ug
        pm = _dot(pooled.astype(BF16), poolw_ref[g]) * pscale_ref[:, cs]
        mixed.append(pm)
    u_ext[0:POOL_CARRY, :] = u_ext[ts:ts + POOL_CARRY, :]

    mixed = jnp.concatenate(mixed, axis=1).astype(BF16)
    y = _dot(mixed, wout_ref[...])
    o_ref[0] = _layer_norm(ALPHA * x + y, lng_ref[...], lnb_ref[...])


def _mixer(x, wa, wu, wif, bif, conv, hng, poolw, pscale, wout, lng, lnb):
    bsz, seq, d = x.shape
    ts = min(SEQ_TILE, seq)
    lc = min(MLSTM_CHUNK, ts)
    mw = N_HEADS * HEAD_DIM
    pw = wu.shape[1]
    const = lambda shape: pl.BlockSpec(shape, lambda b, s: (0,) * len(shape))
    return pl.pallas_call(
        functools.partial(_mixer_kernel, ts=ts, lc=lc),
        out_shape=jax.ShapeDtypeStruct((bsz, seq, d), F32),
        grid=(bsz, seq // ts),
        in_specs=[
            pl.BlockSpec((1, ts, d), lambda b, s: (b, s, 0)),
            const(wa.shape), const(wu.shape), const(wif.shape), const(bif.shape),
            const(conv.shape), const(hng.shape), const(poolw.shape), const(pscale.shape),
            const(wout.shape), const(lng.shape), const(lnb.shape),
        ],
        out_specs=pl.BlockSpec((1, ts, d), lambda b, s: (b, s, 0)),
        scratch_shapes=[
            pltpu.VMEM((ts + CONV_CARRY, 2 * mw), F32),
            pltpu.VMEM((ts + POOL_CARRY, pw), F32),
            pltpu.VMEM((N_HEADS, HEAD_DIM, HEAD_DIM), F32),
            pltpu.VMEM((N_HEADS, 1, HEAD_DIM), F32),
            pltpu.VMEM((N_HEADS, 1, LANES), F32),
        ],
        compiler_params=pltpu.CompilerParams(
            dimension_semantics=("arbitrary", "arbitrary"), vmem_limit_bytes=VMEM_LIMIT),
        name="mixer",
    )(x, wa, wu, wif, bif, conv, hng, poolw, pscale, wout, lng, lnb)


def _top2_sum(a, b, c, d):
    hi1, lo1 = jnp.maximum(a, b), jnp.minimum(a, b)
    hi2, lo2 = jnp.maximum(c, d), jnp.minimum(c, d)
    return jnp.maximum(hi1, hi2) + jnp.maximum(jnp.minimum(hi1, hi2), jnp.maximum(lo1, lo2))


def _xattn_kernel(x_ref, mem_ref, wq_ref, wkv_ref, wo_ref, lng_ref, lnb_ref,
                  rwh_ref, rwm_ref, rwl_ref, rbias_ref,
                  xe_ref, best_ref, rank_ref, cnt_ref,
                  k_scr, v_scr, carry, *, ts):
    b = pl.program_id(0)
    s = pl.program_id(1)
    d = x_ref.shape[2]
    dh = d // XATTN_HEADS

    @pl.when(s == 0)
    def _():
        kv = _dot(mem_ref[0].astype(BF16), wkv_ref[...])
        k_scr[...] = kv[:, :d].astype(BF16)
        v_scr[...] = kv[:, d:].astype(BF16)

    @pl.when(jnp.logical_and(b == 0, s == 0))
    def _():
        carry[...] = jnp.zeros(carry.shape, F32)

    x = x_ref[0]
    q = (_dot(x.astype(BF16), wq_ref[...]) * (dh ** -0.5)).astype(BF16)
    outs = []
    for h in range(XATTN_HEADS):
        hs = slice(h * dh, (h + 1) * dh)
        sc = _dot_nt(q[:, hs], k_scr[:, hs])
        e = jnp.exp(sc - jnp.max(sc, axis=1, keepdims=True))
        l = jnp.sum(e, axis=1, keepdims=True)
        outs.append(_dot(e.astype(BF16), v_scr[:, hs]) * (1.0 / l))
    o = jnp.concatenate(outs, axis=1).astype(BF16)
    x2 = _layer_norm(ALPHA * x + _dot(o, wo_ref[...]), lng_ref[...], lnb_ref[...])
    xe_ref[0, :, 0:d] = x2

    xh, xm, xl = _split3(x2)
    wh, wm, wl = rwh_ref[...], rwm_ref[...], rwl_ref[...]
    logits = (_dot(xh, wh) + (_dot(xh, wm) + _dot(xm, wh))
              + (_dot(xh, wl) + _dot(xm, wm) + _dot(xl, wh)))
    lt = logits.T[0:N_EXPERTS, :]
    score = _sigmoid(lt)
    sel = score + rbias_ref[...]

    sel_r = [sel[e:e + 1, :] for e in range(N_EXPERTS)]
    score_r = [score[e:e + 1, :] for e in range(N_EXPERTS)]
    gs = [_top2_sum(*sel_r[EXPERTS_PER_GROUP * g:EXPERTS_PER_GROUP * (g + 1)]) for g in range(N_GROUPS)]
    best = jnp.zeros((1, ts), jnp.int32)
    bestv = gs[0]
    for g in range(1, N_GROUPS):
        better = gs[g] > bestv
        best = jnp.where(better, g, best)
        bestv = jnp.where(better, gs[g], bestv)
    in_g = [best == g for g in range(N_GROUPS)]

    def pick(rows, j):
        out = rows[j]
        for g in range(1, N_GROUPS):
            out = jnp.where(in_g[g], rows[EXPERTS_PER_GROUP * g + j], out)
        return out

    vsel = [pick(sel_r, j) for j in range(EXPERTS_PER_GROUP)]
    vsc = [pick(score_r, j) for j in range(EXPERTS_PER_GROUP)]
    gates = []
    for j in range(EXPERTS_PER_GROUP):
        beaten = jnp.zeros((1, ts), jnp.int32)
        for k in range(EXPERTS_PER_GROUP):
            if k == j:
                continue
            wins = (vsel[k] > vsel[j]) | ((vsel[k] == vsel[j]) & (k < j))
            beaten = beaten + wins.astype(jnp.int32)
        gates.append(jnp.where(beaten < 2, vsc[j], 0.0))
    gsum = gates[0] + gates[1] + gates[2] + gates[3]
    gates = [g / gsum for g in gates]

    sub = lax.broadcasted_iota(jnp.int32, (SUBLANES, ts), 0)
    g8 = jnp.zeros((SUBLANES, ts), F32)
    for j in range(EXPERTS_PER_GROUP):
        g8 = jnp.where(sub == j, gates[j], g8)
    gfull = jnp.concatenate([g8, jnp.zeros((GATE_LANES - SUBLANES, ts), F32)], axis=0)
    xe_ref[0, :, d:d + GATE_LANES] = gfull.T

    oh8 = jnp.zeros((SUBLANES, ts), F32)
    for g in range(N_GROUPS):
        oh8 = jnp.where((sub == g) & in_g[g], 1.0, oh8)
    r_i = lax.broadcasted_iota(jnp.int32, (ts, ts), 0)
    c_i = lax.broadcasted_iota(jnp.int32, (ts, ts), 1)
    upper = jnp.where(r_i < c_i, 1.0, 0.0).astype(BF16)
    excl = _dot(oh8.astype(BF16), upper)
    base = carry[:, 0:1]
    rank = jnp.sum(jnp.where(oh8 > 0.0, base + excl, 0.0), axis=0, keepdims=True)
    new_carry = jnp.broadcast_to(base + jnp.sum(oh8, axis=1, keepdims=True), carry.shape)
    carry[...] = new_carry
    cnt_ref[...] = new_carry
    best_ref[0] = best
    rank_ref[0] = rank.astype(jnp.int32)


def _xattn_router(x, mem, wq, wkv, wo, lng, lnb, rwh, rwm, rwl, rbias):
    bsz, seq, d = x.shape
    mlen = mem.shape[1]
    ts = min(SEQ_TILE, seq)
    ns = seq // ts
    const = lambda shape: pl.BlockSpec(shape, lambda b, s: (0,) * len(shape))
    return pl.pallas_call(
        functools.partial(_xattn_kernel, ts=ts),
        out_shape=(
            jax.ShapeDtypeStruct((bsz, seq, d + GATE_LANES), F32),
            jax.ShapeDtypeStruct((bsz * ns, 1, ts), jnp.int32),
            jax.ShapeDtypeStruct((bsz * ns, 1, ts), jnp.int32),
            jax.ShapeDtypeStruct((SUBLANES, LANES), F32),
        ),
        grid=(bsz, ns),
        in_specs=[
            pl.BlockSpec((1, ts, d), lambda b, s: (b, s, 0)),
            pl.BlockSpec((1, mlen, d), lambda b, s: (b, 0, 0)),
            const(wq.shape), const(wkv.shape), const(wo.shape), const(lng.shape), const(lnb.shape),
            const(rwh.shape), const(rwm.shape), const(rwl.shape), const(rbias.shape),
        ],
        out_specs=(
            pl.BlockSpec((1, ts, d + GATE_LANES), lambda b, s: (b, s, 0)),
            pl.BlockSpec((1, 1, ts), lambda b, s: (b * ns + s, 0, 0)),
            pl.BlockSpec((1, 1, ts), lambda b, s: (b * ns + s, 0, 0)),
            pl.BlockSpec((SUBLANES, LANES), lambda b, s: (0, 0)),
        ),
        scratch_shapes=[
            pltpu.VMEM((mlen, d), BF16),
            pltpu.VMEM((mlen, d), BF16),
            pltpu.VMEM((SUBLANES, LANES), F32),
        ],
        compiler_params=pltpu.CompilerParams(
            dimension_semantics=("arbitrary", "arbitrary"), vmem_limit_bytes=VMEM_LIMIT),
        name="xattn_router",
    )(x, mem, wq, wkv, wo, lng, lnb, rwh, rwm, rwl, rbias)


def _rowmove_kernel(idx_ref, zr_ref, src_hbm, dst_hbm, zero_scr, sem, zsem, *, ch, scatter):
    i = pl.program_id(0)

    def row_copy(j):
        t = i * ch + j
        r = idx_ref[0, 0, j]
        if scatter:
            return pltpu.make_async_copy(src_hbm.at[pl.ds(t, 1)], dst_hbm.at[pl.ds(r, 1)], sem)
        return pltpu.make_async_copy(src_hbm.at[pl.ds(r, 1)], dst_hbm.at[pl.ds(t, 1)], sem)

    def start(j, carry):
        row_copy(j).start()
        return carry

    def wait(j, carry):
        row_copy(j).wait()
        return carry

    lax.fori_loop(0, ch, start, 0, unroll=8)

    if scatter:
        @pl.when(i == 0)
        def _():
            zero_scr[...] = jnp.zeros(zero_scr.shape, F32)
            for g in range(N_GROUPS):
                lo = zr_ref[2 * g]
                hi = zr_ref[2 * g + 1]

                def zcopy(r):
                    return pltpu.make_async_copy(zero_scr.at[pl.ds(0, 1)], dst_hbm.at[pl.ds(r, 1)], zsem)

                def zstart(r, carry):
                    zcopy(r).start()
                    return carry

                def zwait(r, carry):
                    zcopy(r).wait()
                    return carry

                lax.fori_loop(lo, hi, zstart, 0)
                lax.fori_loop(lo, hi, zwait, 0)

    lax.fori_loop(0, ch, wait, 0, unroll=8)


def _rowmove(idx, zero_ranges, src, n_out, scatter):
    n_idx = idx.shape[0]
    width = src.shape[1]
    ch = min(MOVE_CHUNK, n_idx)
    idx3 = idx.reshape(n_idx // ch, 1, ch)
    return pl.pallas_call(
        functools.partial(_rowmove_kernel, ch=ch, scatter=scatter),
        out_shape=jax.ShapeDtypeStruct((n_out, width), F32),
        grid=(n_idx // ch,),
        in_specs=[
            pl.BlockSpec((1, 1, ch), lambda i: (i, 0, 0), memory_space=pltpu.SMEM),
            pl.BlockSpec(memory_space=pltpu.SMEM),
            pl.BlockSpec(memory_space=pl.ANY),
        ],
        out_specs=pl.BlockSpec(memory_space=pl.ANY),
        scratch_shapes=[
            pltpu.VMEM((SUBLANES, width), F32),
            pltpu.SemaphoreType.DMA,
            pltpu.SemaphoreType.DMA,
        ],
        compiler_params=pltpu.CompilerParams(
            dimension_semantics=("arbitrary",), has_side_effects=True),
        name="scatter_rows" if scatter else "gather_rows",
    )(idx3, zero_ranges, src)


def _ffn_kernel(grp_ref, xe_ref, wg_ref, wu_ref, wd_ref, lng_ref, lnb_ref, o_ref):
    d = o_ref.shape[1]
    de = wg_ref.shape[2] // EXPERTS_PER_GROUP
    x = xe_ref[:, 0:d]
    xb = x.astype(BF16)
    hg = _dot(xb, wg_ref[0])
    hu = _dot(xb, wu_ref[0])
    hid = hg * _sigmoid(hg) * hu
    parts = []
    for j in range(EXPERTS_PER_GROUP):
        gate = xe_ref[:, d + j:d + j + 1]
        hj = hid[:, j * de:(j + 1) * de]
        parts.append(jnp.where(gate != 0.0, hj * gate, 0.0))
    hid = jnp.concatenate(parts, axis=1).astype(BF16)
    y = _dot(hid, wd_ref[0])
    o_ref[...] = _layer_norm(ALPHA * x + y, lng_ref[...], lnb_ref[...])


def _ffn(blk_grp, xe_sorted, wg, wu, wd, lng, lnb):
    rows, width = xe_sorted.shape
    d = wd.shape[2]
    blk = FFN_BLOCK
    grid_spec = pltpu.PrefetchScalarGridSpec(
        num_scalar_prefetch=1,
        grid=(rows // blk,),
        in_specs=[
            pl.BlockSpec((blk, width), lambda i, grp: (i, 0)),
            pl.BlockSpec((1,) + wg.shape[1:], lambda i, grp: (grp[i], 0, 0)),
            pl.BlockSpec((1,) + wu.shape[1:], lambda i, grp: (grp[i], 0, 0)),
            pl.BlockSpec((1,) + wd.shape[1:], lambda i, grp: (grp[i], 0, 0)),
            pl.BlockSpec(lng.shape, lambda i, grp: (0, 0)),
            pl.BlockSpec(lnb.shape, lambda i, grp: (0, 0)),
        ],
        out_specs=pl.BlockSpec((blk, d), lambda i, grp: (i, 0)),
    )
    return pl.pallas_call(
        _ffn_kernel,
        out_shape=jax.ShapeDtypeStruct((rows, d), F32),
        grid_spec=grid_spec,
        compiler_params=pltpu.CompilerParams(
            dimension_semantics=("arbitrary",), vmem_limit_bytes=VMEM_LIMIT),
        name="group_ffn",
    )(blk_grp, xe_sorted, wg, wu, wd, lng, lnb)


def _group_weights(w):
    e, a, b = w.shape
    return w.reshape(N_GROUPS, EXPERTS_PER_GROUP, a, b).transpose(0, 2, 1, 3).reshape(
        N_GROUPS, a, EXPERTS_PER_GROUP * b).astype(BF16)


def kernel(x, mem, w_in, b_i, b_f, conv_qk, head_norm_g, pool_w, pool_scale, w_mix_out,
           ln_mix_g, ln_mix_b, w_xq, w_xkv, w_xo, ln_x_g, ln_x_b, router_w, router_bias,
           w_gate, w_up, w_down, ln_moe_g, ln_moe_b):
    bsz, seq, d = x.shape
    n_tok = bsz * seq
    mw = N_HEADS * HEAD_DIM
    n_gate = 2 * N_HEADS
    blk = FFN_BLOCK
    n_rows = n_tok + N_GROUPS * blk

    rw = jnp.pad(router_w, ((0, 0), (0, LANES - N_EXPERTS)))
    rwh, rwm, rwl = _split3(rw)
    rbias = router_bias.reshape(N_EXPERTS, 1).astype(F32)
    row = lambda v: v.reshape(1, -1).astype(F32)

    for l in range(DEPTH):
        wa = w_in[l][:, :4 * mw].astype(BF16)
        wu = w_in[l][:, 4 * mw + n_gate:].astype(BF16)
        wif = jnp.pad(w_in[l][:, 4 * mw:4 * mw + n_gate], ((0, 0), (0, LANES - n_gate))).astype(BF16)
        bif = jnp.pad(jnp.concatenate([b_i[l], b_f[l]]), (0, LANES - n_gate)).reshape(1, LANES)
        x = _mixer(x, wa, wu, wif, bif, conv_qk[l], row(head_norm_g[l]), pool_w[l].astype(BF16),
                   row(pool_scale[l]), w_mix_out[l].astype(BF16), row(ln_mix_g[l]), row(ln_mix_b[l]))

        xe, best, rank, cnt = _xattn_router(
            x, mem, w_xq[l].astype(BF16), w_xkv[l].astype(BF16), w_xo[l].astype(BF16),
            row(ln_x_g[l]), row(ln_x_b[l]), rwh, rwm, rwl, rbias)

        counts = cnt[:N_GROUPS, 0].astype(jnp.int32)
        padded = (counts + blk - 1) // blk * blk
        pend = jnp.cumsum(padded)
        pstart = pend - padded
        best = best.reshape(n_tok)
        dest = pstart[best] + rank.reshape(n_tok)
        seg_end = jnp.concatenate([pstart[1:], jnp.array([n_rows], jnp.int32)])
        zero_ranges = jnp.stack([pstart + counts, seg_end], axis=1).reshape(-1).astype(jnp.int32)
        blk_grp = jnp.minimum(
            jnp.searchsorted(pend, jnp.arange(n_rows // blk, dtype=jnp.int32) * blk, side="right"),
            N_GROUPS - 1).astype(jnp.int32)

        xe_sorted = _rowmove(dest, zero_ranges, xe.reshape(n_tok, d + GATE_LANES), n_rows, scatter=True)
        y_sorted = _ffn(blk_grp, xe_sorted, _group_weights(w_gate[l]), _group_weights(w_up[l]),
                        w_down[l].reshape(N_GROUPS, EXPERTS_PER_GROUP * w_down.shape[2], d).astype(BF16),
                        row(ln_moe_g[l]), row(ln_moe_b[l]))
        x = _rowmove(dest, zero_ranges, y_sorted, n_tok, scatter=False).reshape(bsz, seq, d)
    return x
```

```python
import functools

import jax
import jax.numpy as jnp
from jax import lax
from jax.experimental import pallas as pl
from jax.experimental.pallas import tpu as pltpu

F32 = jnp.float32
BF16 = jnp.bfloat16

N_HEADS = 4
HEAD_DIM = 128
POOL_WINDOWS = (2, 4, 8, 16)
POOL_GROUP = 128
CONV_WIDTH = 4
XATTN_HEADS = 4
N_EXPERTS = 16
N_GROUPS = 4
EXPERTS_PER_GROUP = 4
DEPTH = 2
ALPHA = (2 * DEPTH) ** 0.25
LN_EPS = 1e-5

LANES = 128
SUBLANES = 8
VMEM_LIMIT = 56 * 1024 * 1024

SEQ_TILE = 256
MLSTM_CHUNK = 256
FFN_BLOCK = 256
TAIL_LANES = 128
ID_LANE = EXPERTS_PER_GROUP
CONV_CARRY = 8
POOL_CARRY = 16


def _layer_norm(y, g, b):
    mu = jnp.mean(y, axis=-1, keepdims=True)
    d = y - mu
    var = jnp.mean(d * d, axis=-1, keepdims=True)
    return d * lax.rsqrt(var + LN_EPS) * g + b


def _sigmoid(v):
    return 1.0 / (1.0 + jnp.exp(-v))


def _dot(a, b):
    return jnp.dot(a, b, preferred_element_type=F32)


def _dot_nt(a, b):
    return lax.dot_general(a, b, (((1,), (1,)), ((), ())), preferred_element_type=F32)


def _split3(v):
    hi = v.astype(BF16)
    r1 = v - hi.astype(F32)
    mid = r1.astype(BF16)
    lo = (r1 - mid.astype(F32)).astype(BF16)
    return hi, mid, lo


def _loop(n, body, unroll=1):
    lax.fori_loop(0, n, lambda j, c: (body(j), c)[1], 0, unroll=unroll)


def _mixer_kernel(x_ref, wa_ref, wu_ref, wif_ref, bif_ref, conv_ref, hng_ref, poolw_ref,
                  pscale_ref, wout_ref, lng_ref, lnb_ref, o_ref,
                  zq_ext, u_ext, c_st, n_st, m_st, *, ts, lc):
    s = pl.program_id(1)
    mw = N_HEADS * HEAD_DIM

    @pl.when(s == 0)
    def _():
        zq_ext[0:CONV_CARRY, :] = jnp.zeros((CONV_CARRY, 2 * mw), F32)
        u_ext[0:POOL_CARRY, :] = jnp.zeros((POOL_CARRY, u_ext.shape[1]), F32)
        c_st[...] = jnp.zeros(c_st.shape, F32)
        n_st[...] = jnp.zeros(n_st.shape, F32)
        m_st[...] = jnp.zeros(m_st.shape, F32)

    x = x_ref[0]
    xb = x.astype(BF16)
    z = _dot(xb, wa_ref[...])
    u = _dot(xb, wu_ref[...])
    gts = _dot(xb, wif_ref[...]) + bif_ref[...]

    zq_ext[CONV_CARRY:CONV_CARRY + ts, :] = z[:, :2 * mw]
    cw = conv_ref[...]
    acc = zq_ext[CONV_CARRY:CONV_CARRY + ts, :] * cw[CONV_WIDTH - 1:CONV_WIDTH, :]
    for j in range(1, CONV_WIDTH):
        acc = acc + zq_ext[CONV_CARRY - j:CONV_CARRY - j + ts, :] * cw[CONV_WIDTH - 1 - j:CONV_WIDTH - j, :]
    zq_ext[0:CONV_CARRY, :] = zq_ext[ts:ts + CONV_CARRY, :]
    qk = acc * _sigmoid(acc)
    q_all = qk[:, :mw] * (HEAD_DIM ** -0.5)
    k_all = qk[:, mw:]
    v_all = z[:, 2 * mw:3 * mw]
    o_all = z[:, 3 * mw:4 * mw]

    lf_all = jnp.minimum(gts, 0.0) - jnp.log1p(jnp.exp(-jnp.abs(gts)))

    row_i = lax.broadcasted_iota(jnp.int32, (lc, lc), 0)
    col_i = lax.broadcasted_iota(jnp.int32, (lc, lc), 1)
    causal = col_i <= row_i
    tri = jnp.where(causal, 1.0, 0.0).astype(BF16)

    head_out = [[] for _ in range(N_HEADS)]
    for c in range(ts // lc):
        rows = slice(c * lc, (c + 1) * lc)
        hi, mid, lo = _split3(lf_all[rows, :])
        b_all = _dot(tri, hi) + _dot(tri, mid) + _dot(tri, lo)
        g_c = gts[rows, :]
        r_all = g_c - pltpu.roll(b_all, LANES - N_HEADS, 1)
        r_t = r_all.T
        for h in range(N_HEADS):
            hs = slice(h * HEAD_DIM, (h + 1) * HEAD_DIM)
            qh = q_all[rows, hs]
            kh = k_all[rows, hs]
            vh = v_all[rows, hs].astype(BF16)
            qhb = qh.astype(BF16)
            bc = b_all[:, N_HEADS + h:N_HEADS + h + 1]
            igc = g_c[:, h:h + 1]
            r_row = r_t[h:h + 1, :]
            c_prev = c_st[h]
            n_prev = n_st[h]
            m_prev = m_st[h][:, 0:1]

            log_d = jnp.where(causal, bc + r_row, -jnp.inf)
            m_intra = jnp.max(log_d, axis=1, keepdims=True)
            log_inter = bc + m_prev
            m_t = jnp.maximum(m_intra, log_inter)
            p = jnp.exp(log_d - m_t) * _dot_nt(qhb, kh.astype(BF16))
            inter = jnp.exp(log_inter - m_t)
            num = _dot(p.astype(BF16), vh) + inter * _dot(qhb, c_prev.astype(BF16))
            den = (jnp.sum(p, axis=1, keepdims=True)
                   + inter * jnp.sum(qh * n_prev, axis=1, keepdims=True))
            hh = num / jnp.maximum(jnp.abs(den), jnp.exp(-m_t))

            b_last = bc[lc - 1:lc, :]
            w_state = b_last - bc + igc
            m_loc = jnp.max(w_state, axis=0, keepdims=True)
            ka = kh * jnp.exp(w_state - m_loc)
            c_loc = _dot(ka.T.astype(BF16), vh)
            n_loc = jnp.sum(ka, axis=0, keepdims=True)
            m_new = jnp.maximum(b_last + m_prev, m_loc)
            s_old = jnp.exp(b_last + m_prev - m_new)
            s_new = jnp.exp(m_loc - m_new)
            c_st[h] = s_old * c_prev + s_new * c_loc
            n_st[h] = s_old * n_prev + s_new * n_loc
            m_st[h] = jnp.broadcast_to(m_new, (1, LANES))

            mu = jnp.mean(hh, axis=1, keepdims=True)
            dlt = hh - mu
            var = jnp.mean(dlt * dlt, axis=1, keepdims=True)
            hn = dlt * lax.rsqrt(var + LN_EPS) * hng_ref[:, hs]
            head_out[h].append(hn * _sigmoid(o_all[rows, hs]))

    mixed = [jnp.concatenate(ho, axis=0) if len(ho) > 1 else ho[0] for ho in head_out]

    u_ext[POOL_CARRY:POOL_CARRY + ts, :] = u
    pos = (lax.broadcasted_iota(jnp.int32, (ts, 1), 0) + s * ts + 1).astype(F32)
    for g, w in enumerate(POOL_WINDOWS):
        cs = slice(g * POOL_GROUP, (g + 1) * POOL_GROUP)
        ug = u_ext[POOL_CARRY:POOL_CARRY + ts, cs]
        win = ug
        for j in range(1, w):
            win = win + u_ext[POOL_CARRY - j:POOL_CARRY - j + ts, cs]
        pooled = win / jnp.minimum(pos, float(w)) - ug
        pm = _dot(pooled.astype(BF16), poolw_ref[g]) * pscale_ref[:, cs]
        mixed.append(pm)
    u_ext[0:POOL_CARRY, :] = u_ext[ts:ts + POOL_CARRY, :]

    mixed = jnp.concatenate(mixed, axis=1).astype(BF16)
    y = _dot(mixed, wout_ref[...])
    o_ref[0] = _layer_norm(ALPHA * x + y, lng_ref[...], lnb_ref[...])


def _mixer(x, wa, wu, wif, bif, conv, hng, poolw, pscale, wout, lng, lnb):
    bsz, seq, d = x.shape
    ts = min(SEQ_TILE, seq)
    lc = min(MLSTM_CHUNK, ts)
    mw = N_HEADS * HEAD_DIM
    pw = wu.shape[1]
    const = lambda shape: pl.BlockSpec(shape, lambda b, s: (0,) * len(shape))
    return pl.pallas_call(
        functools.partial(_mixer_kernel, ts=ts, lc=lc),
        out_shape=jax.ShapeDtypeStruct((bsz, seq, d), F32),
        grid=(bsz, seq // ts),
        in_specs=[
            pl.BlockSpec((1, ts, d), lambda b, s: (b, s, 0)),
            const(wa.shape), const(wu.shape), const(wif.shape), const(bif.shape),
            const(conv.shape), const(hng.shape), const(poolw.shape), const(pscale.shape),
            const(wout.shape), const(lng.shape), const(lnb.shape),
        ],
        out_specs=pl.BlockSpec((1, ts, d), lambda b, s: (b, s, 0)),
        scratch_shapes=[
            pltpu.VMEM((ts + CONV_CARRY, 2 * mw), F32),
            pltpu.VMEM((ts + POOL_CARRY, pw), F32),
            pltpu.VMEM((N_HEADS, HEAD_DIM, HEAD_DIM), F32),
            pltpu.VMEM((N_HEADS, 1, HEAD_DIM), F32),
            pltpu.VMEM((N_HEADS, 1, LANES), F32),
        ],
        compiler_params=pltpu.CompilerParams(
            dimension_semantics=("arbitrary", "arbitrary"), vmem_limit_bytes=VMEM_LIMIT),
        name="mixer",
    )(x, wa, wu, wif, bif, conv, hng, poolw, pscale, wout, lng, lnb)


def _top2_sum(a, b, c, d):
    hi1, lo1 = jnp.maximum(a, b), jnp.minimum(a, b)
    hi2, lo2 = jnp.maximum(c, d), jnp.minimum(c, d)
    return jnp.maximum(hi1, hi2) + jnp.maximum(jnp.minimum(hi1, hi2), jnp.maximum(lo1, lo2))


def _xattn_kernel(x_ref, mem_ref, wq_ref, wkv_ref, wo_ref, lng_ref, lnb_ref,
                  rwh_ref, rwm_ref, rwl_ref, rbias_ref,
                  buf_hbm, cnt_ref,
                  k_scr, v_scr, carry, xrow, dvec, dsm, cvec, csm, zrow, sem, ssem, zsem,
                  *, ts, cap, blk):
    b = pl.program_id(0)
    s = pl.program_id(1)
    step = b * pl.num_programs(1) + s
    last = step == pl.num_programs(0) * pl.num_programs(1) - 1
    d = x_ref.shape[2]
    dh = d // XATTN_HEADS

    @pl.when(s == 0)
    def _():
        kv = _dot(mem_ref[0].astype(BF16), wkv_ref[...])
        k_scr[...] = kv[:, :d].astype(BF16)
        v_scr[...] = kv[:, d:].astype(BF16)

    @pl.when(step == 0)
    def _():
        carry[...] = jnp.zeros(carry.shape, F32)

    x = x_ref[0]
    q = (_dot(x.astype(BF16), wq_ref[...]) * (dh ** -0.5)).astype(BF16)
    outs = []
    for h in range(XATTN_HEADS):
        hs = slice(h * dh, (h + 1) * dh)
        sc = _dot_nt(q[:, hs], k_scr[:, hs])
        e = jnp.exp(sc - jnp.max(sc, axis=1, keepdims=True))
        l = jnp.sum(e, axis=1, keepdims=True)
        outs.append(_dot(e.astype(BF16), v_scr[:, hs]) * (1.0 / l))
    o = jnp.concatenate(outs, axis=1).astype(BF16)
    x2 = _layer_norm(ALPHA * x + _dot(o, wo_ref[...]), lng_ref[...], lnb_ref[...])

    xh, xm, xl = _split3(x2)
    wh, wm, wl = rwh_ref[...], rwm_ref[...], rwl_ref[...]
    logits = (_dot(xh, wh) + (_dot(xh, wm) + _dot(xm, wh))
              + (_dot(xh, wl) + _dot(xm, wm) + _dot(xl, wh)))
    lt = logits.T[0:N_EXPERTS, :]
    score = _sigmoid(lt)
    sel = score + rbias_ref[...]

    sel_r = [sel[e:e + 1, :] for e in range(N_EXPERTS)]
    score_r = [score[e:e + 1, :] for e in range(N_EXPERTS)]
    gs = [_top2_sum(*sel_r[EXPERTS_PER_GROUP * g:EXPERTS_PER_GROUP * (g + 1)]) for g in range(N_GROUPS)]
    best = jnp.zeros((1, ts), jnp.int32)
    bestv = gs[0]
    for g in range(1, N_GROUPS):
        better = gs[g] > bestv
        best = jnp.where(better, g, best)
        bestv = jnp.where(better, gs[g], bestv)
    in_g = [best == g for g in range(N_GROUPS)]

    def pick(rows, j):
        out = rows[j]
        for g in range(1, N_GROUPS):
            out = jnp.where(in_g[g], rows[EXPERTS_PER_GROUP * g + j], out)
        return out

    vsel = [pick(sel_r, j) for j in range(EXPERTS_PER_GROUP)]
    vsc = [pick(score_r, j) for j in range(EXPERTS_PER_GROUP)]
    gates = []
    for j in range(EXPERTS_PER_GROUP):
        beaten = jnp.zeros((1, ts), jnp.int32)
        for k in range(EXPERTS_PER_GROUP):
            if k == j:
                continue
            wins = (vsel[k] > vsel[j]) | ((vsel[k] == vsel[j]) & (k < j))
            beaten = beaten + wins.astype(jnp.int32)
        gates.append(jnp.where(beaten < 2, vsc[j], 0.0))
    gsum = gates[0] + gates[1] + gates[2] + gates[3]
    gates = [g / gsum for g in gates]

    sub = lax.broadcasted_iota(jnp.int32, (SUBLANES, ts), 0)
    tok = (lax.broadcasted_iota(jnp.int32, (1, ts), 1) + step * ts).astype(F32)
    t8 = jnp.where(sub == ID_LANE, tok, 0.0)
    for j in range(EXPERTS_PER_GROUP):
        t8 = jnp.where(sub == j, gates[j], t8)
    tail = jnp.concatenate([t8, jnp.zeros((TAIL_LANES - SUBLANES, ts), F32)], axis=0).T

    oh8 = jnp.zeros((SUBLANES, ts), F32)
    for g in range(N_GROUPS):
        oh8 = jnp.where((sub == g) & in_g[g], 1.0, oh8)
    r_i = lax.broadcasted_iota(jnp.int32, (ts, ts), 0)
    c_i = lax.broadcasted_iota(jnp.int32, (ts, ts), 1)
    upper = jnp.where(r_i < c_i, 1.0, 0.0).astype(BF16)
    excl = _dot(oh8.astype(BF16), upper)
    base = carry[:, 0:1]
    rank = jnp.sum(jnp.where(oh8 > 0.0, base + excl, 0.0), axis=0, keepdims=True)
    new_carry = jnp.broadcast_to(base + jnp.sum(oh8, axis=1, keepdims=True), carry.shape)
    carry[...] = new_carry
    cnt_ref[...] = new_carry
    dest = best * cap + rank.astype(jnp.int32)

    def tile_copy():
        return pltpu.make_async_copy(xrow, buf_hbm.at[pl.ds(0, ts)], sem)

    @pl.when(step > 0)
    def _():
        tile_copy().wait()

    xrow[:, 0:d] = x2
    xrow[:, d:d + TAIL_LANES] = tail
    dvec[...] = dest
    to_smem = pltpu.make_async_copy(dvec, dsm, ssem)
    to_smem.start()
    to_smem.wait()

    def send(j):
        pltpu.make_async_copy(xrow.at[pl.ds(j, 1)], buf_hbm.at[pl.ds(dsm[0, j], 1)], sem).start()

    _loop(ts, send, unroll=8)

    @pl.when(last)
    def _():
        tile_copy().wait()
        zrow[...] = jnp.zeros(zrow.shape, F32)
        cvec[...] = new_carry.astype(jnp.int32)
        cp = pltpu.make_async_copy(cvec, csm, ssem)
        cp.start()
        cp.wait()
        for g in range(N_GROUPS):
            n_g = csm[g, 0]
            lo = g * cap + n_g
            n_pad = (blk - n_g % blk) % blk

            def zcopy(r, lo=lo):
                return pltpu.make_async_copy(zrow.at[pl.ds(0, 1)], buf_hbm.at[pl.ds(lo + r, 1)], zsem)

            _loop(n_pad, lambda r: zcopy(r).start())
            _loop(n_pad, lambda r: zcopy(r).wait())


def _xattn_router(x, mem, wq, wkv, wo, lng, lnb, rwh, rwm, rwl, rbias, cap, blk):
    bsz, seq, d = x.shape
    mlen = mem.shape[1]
    ts = min(SEQ_TILE, seq)
    ns = seq // ts
    width = d + TAIL_LANES
    const = lambda shape: pl.BlockSpec(shape, lambda b, s: (0,) * len(shape))
    return pl.pallas_call(
        functools.partial(_xattn_kernel, ts=ts, cap=cap, blk=blk),
        out_shape=(
            jax.ShapeDtypeStruct((N_GROUPS * cap, width), F32),
            jax.ShapeDtypeStruct((SUBLANES, LANES), F32),
        ),
        grid=(bsz, ns),
        in_specs=[
            pl.BlockSpec((1, ts, d), lambda b, s: (b, s, 0)),
            pl.BlockSpec((1, mlen, d), lambda b, s: (b, 0, 0)),
            const(wq.shape), const(wkv.shape), const(wo.shape), const(lng.shape), const(lnb.shape),
            const(rwh.shape), const(rwm.shape), const(rwl.shape), const(rbias.shape),
        ],
        out_specs=(
            pl.BlockSpec(memory_space=pl.ANY),
            pl.BlockSpec((SUBLANES, LANES), lambda b, s: (0, 0)),
        ),
        scratch_shapes=[
            pltpu.VMEM((mlen, d), BF16),
            pltpu.VMEM((mlen, d), BF16),
            pltpu.VMEM((SUBLANES, LANES), F32),
            pltpu.VMEM((ts, width), F32),
            pltpu.VMEM((1, ts), jnp.int32),
            pltpu.SMEM((1, ts), jnp.int32),
            pltpu.VMEM((SUBLANES, LANES), jnp.int32),
            pltpu.SMEM((SUBLANES, LANES), jnp.int32),
            pltpu.VMEM((SUBLANES, width), F32),
            pltpu.SemaphoreType.DMA,
            pltpu.SemaphoreType.DMA,
            pltpu.SemaphoreType.DMA,
        ],
        compiler_params=pltpu.CompilerParams(
            dimension_semantics=("arbitrary", "arbitrary"), vmem_limit_bytes=VMEM_LIMIT,
            has_side_effects=True),
        name="xattn_router",
    )(x, mem, wq, wkv, wo, lng, lnb, rwh, rwm, rwl, rbias)


def _ffn_kernel(blk_in_ref, grp_ref, nv_ref, xe_ref, wg_ref, wu_ref, wd_ref, lng_ref, lnb_ref,
                out_hbm, yrow, idv, ids, sem, ssem):
    i = pl.program_id(0)
    blk = xe_ref.shape[0]
    d = yrow.shape[1]
    de = wg_ref.shape[2] // EXPERTS_PER_GROUP
    n_valid = nv_ref[i]

    tail_t = xe_ref[:, d:d + TAIL_LANES].T
    idv[...] = tail_t[ID_LANE:ID_LANE + 1, :].astype(jnp.int32)
    to_smem = pltpu.make_async_copy(idv, ids, ssem)
    to_smem.start()

    x = xe_ref[:, 0:d]
    xb = x.astype(BF16)
    hg = _dot(xb, wg_ref[0])
    hu = _dot(xb, wu_ref[0])
    hid = hg * _sigmoid(hg) * hu
    parts = []
    for j in range(EXPERTS_PER_GROUP):
        gate = xe_ref[:, d + j:d + j + 1]
        hj = hid[:, j * de:(j + 1) * de]
        parts.append(jnp.where(gate != 0.0, hj * gate, 0.0))
    hid = jnp.concatenate(parts, axis=1).astype(BF16)
    y = _layer_norm(ALPHA * x + _dot(hid, wd_ref[0]), lng_ref[...], lnb_ref[...])

    def row_copy(j):
        return pltpu.make_async_copy(yrow.at[pl.ds(j, 1)], out_hbm.at[pl.ds(ids[0, j], 1)], sem)

    def block_wait():
        pltpu.make_async_copy(yrow, out_hbm.at[pl.ds(0, blk)], sem).wait()

    @pl.when(jnp.logical_and(i > 0, nv_ref[jnp.maximum(i - 1, 0)] == blk))
    def _():
        block_wait()

    yrow[...] = y
    to_smem.wait()

    @pl.when(n_valid == blk)
    def _():
        _loop(blk, lambda j: row_copy(j).start(), unroll=8)

        @pl.when(i == pl.num_programs(0) - 1)
        def _():
            block_wait()

    @pl.when(n_valid < blk)
    def _():
        _loop(n_valid, lambda j: row_copy(j).start())
        _loop(n_valid, lambda j: row_copy(j).wait())


def _ffn(blk_in, blk_grp, n_valid, buf, wg, wu, wd, lng, lnb, n_tok):
    width = buf.shape[1]
    d = wd.shape[2]
    blk = FFN_BLOCK
    grid_spec = pltpu.PrefetchScalarGridSpec(
        num_scalar_prefetch=3,
        grid=(blk_in.shape[0],),
        in_specs=[
            pl.BlockSpec((blk, width), lambda i, bi, grp, nv: (bi[i], 0)),
            pl.BlockSpec((1,) + wg.shape[1:], lambda i, bi, grp, nv: (grp[i], 0, 0)),
            pl.BlockSpec((1,) + wu.shape[1:], lambda i, bi, grp, nv: (grp[i], 0, 0)),
            pl.BlockSpec((1,) + wd.shape[1:], lambda i, bi, grp, nv: (grp[i], 0, 0)),
            pl.BlockSpec(lng.shape, lambda i, bi, grp, nv: (0, 0)),
            pl.BlockSpec(lnb.shape, lambda i, bi, grp, nv: (0, 0)),
        ],
        out_specs=pl.BlockSpec(memory_space=pl.ANY),
        scratch_shapes=[
            pltpu.VMEM((blk, d), F32),
            pltpu.VMEM((1, blk), jnp.int32),
            pltpu.SMEM((1, blk), jnp.int32),
            pltpu.SemaphoreType.DMA,
            pltpu.SemaphoreType.DMA,
        ],
    )
    return pl.pallas_call(
        _ffn_kernel,
        out_shape=jax.ShapeDtypeStruct((n_tok, d), F32),
        grid_spec=grid_spec,
        compiler_params=pltpu.CompilerParams(
            dimension_semantics=("arbitrary",), vmem_limit_bytes=VMEM_LIMIT, has_side_effects=True),
        name="group_ffn",
    )(blk_in, blk_grp, n_valid, buf, wg, wu, wd, lng, lnb)


def _group_weights(w):
    e, a, b = w.shape
    return w.reshape(N_GROUPS, EXPERTS_PER_GROUP, a, b).transpose(0, 2, 1, 3).reshape(
        N_GROUPS, a, EXPERTS_PER_GROUP * b).astype(BF16)


def _block_tables(counts, cap, blk, n_steps):
    nblk = (counts + blk - 1) // blk
    bend = jnp.cumsum(nblk)
    bstart = bend - nblk
    step = jnp.arange(n_steps, dtype=jnp.int32)
    used = step < bend[-1]
    grp = jnp.minimum(jnp.searchsorted(bend, step, side="right"), N_GROUPS - 1).astype(jnp.int32)
    j = step - bstart[grp]
    blk_in = grp * (cap // blk) + j
    n_valid = jnp.clip(counts[grp] - j * blk, 0, blk)
    last_real = jnp.maximum(bend[-1] - 1, 0)
    blk_in = jnp.where(used, blk_in, blk_in[last_real])
    grp = jnp.where(used, grp, grp[last_real])
    n_valid = jnp.where(used, n_valid, 0)
    return blk_in.astype(jnp.int32), grp.astype(jnp.int32), n_valid.astype(jnp.int32)


def kernel(x, mem, w_in, b_i, b_f, conv_qk, head_norm_g, pool_w, pool_scale, w_mix_out,
           ln_mix_g, ln_mix_b, w_xq, w_xkv, w_xo, ln_x_g, ln_x_b, router_w, router_bias,
           w_gate, w_up, w_down, ln_moe_g, ln_moe_b):
    bsz, seq, d = x.shape
    n_tok = bsz * seq
    mw = N_HEADS * HEAD_DIM
    n_gate = 2 * N_HEADS
    blk = FFN_BLOCK
    cap = n_tok
    n_steps = n_tok // blk + N_GROUPS

    rw = jnp.pad(router_w, ((0, 0), (0, LANES - N_EXPERTS)))
    rwh, rwm, rwl = _split3(rw)
    rbias = router_bias.reshape(N_EXPERTS, 1).astype(F32)
    row = lambda v: v.reshape(1, -1).astype(F32)

    for l in range(DEPTH):
        wa = w_in[l][:, :4 * mw].astype(BF16)
        wu = w_in[l][:, 4 * mw + n_gate:].astype(BF16)
        wif = jnp.pad(w_in[l][:, 4 * mw:4 * mw + n_gate], ((0, 0), (0, LANES - n_gate))).astype(BF16)
        bif = jnp.pad(jnp.concatenate([b_i[l], b_f[l]]), (0, LANES - n_gate)).reshape(1, LANES)
        x = _mixer(x, wa, wu, wif, bif, conv_qk[l], row(head_norm_g[l]), pool_w[l].astype(BF16),
                   row(pool_scale[l]), w_mix_out[l].astype(BF16), row(ln_mix_g[l]), row(ln_mix_b[l]))

        buf, cnt = _xattn_router(
            x, mem, w_xq[l].astype(BF16), w_xkv[l].astype(BF16), w_xo[l].astype(BF16),
            row(ln_x_g[l]), row(ln_x_b[l]), rwh, rwm, rwl, rbias, cap, blk)

        counts = cnt[:N_GROUPS, 0].astype(jnp.int32)
        blk_in, blk_grp, n_valid = _block_tables(counts, cap, blk, n_steps)
        x = _ffn(blk_in, blk_grp, n_valid, buf, _group_weights(w_gate[l]), _group_weights(w_up[l]),
                 w_down[l].reshape(N_GROUPS, EXPERTS_PER_GROUP * w_down.shape[2], d).astype(BF16),
                 row(ln_moe_g[l]), row(ln_moe_b[l]), n_tok).reshape(bsz, seq, d)
    return x
```

```python
import functools

import jax
import jax.numpy as jnp
from jax import lax
from jax.experimental import pallas as pl
from jax.experimental.pallas import tpu as pltpu

F32 = jnp.float32
BF16 = jnp.bfloat16
I32 = jnp.int32

N_HEADS = 4
HEAD_DIM = 128
POOL_WINDOWS = (2, 4, 8, 16)
POOL_GROUP = 128
CONV_WIDTH = 4
XATTN_HEADS = 4
N_EXPERTS = 16
N_GROUPS = 4
EXPERTS_PER_GROUP = 4
DEPTH = 2
ALPHA = (2 * DEPTH) ** 0.25
LN_EPS = 1e-5

LANES = 128
SUBLANES = 8
BF16_TILE_ROWS = 16
VMEM_LIMIT = 56 * 1024 * 1024

SEQ_TILE = 256
MLSTM_CHUNK = 256
FFN_BLOCK = 256
CONV_CARRY = 8
POOL_CARRY = 16

RUN_ALIGN = BF16_TILE_ROWS
RUN_CHUNK_BITS = 5
STAGE_ROWS = SEQ_TILE + N_GROUPS * RUN_ALIGN
STAGE_ROWS_PADDED = 384
TAIL_LANES = 128
STAGE_LANE = 5
M_SLOT, M_LEN, M_OFF, M_TOTAL, M_END = 0, 1, 2, 3, 4
META_W = 16


def _layer_norm(y, g, b):
    mu = jnp.mean(y, axis=-1, keepdims=True)
    d = y - mu
    var = jnp.mean(d * d, axis=-1, keepdims=True)
    return d * lax.rsqrt(var + LN_EPS) * g + b


def _sigmoid(v):
    return 1.0 / (1.0 + jnp.exp(-v))


def _dot(a, b):
    return jnp.dot(a, b, preferred_element_type=F32)


def _dot_nt(a, b):
    return lax.dot_general(a, b, (((1,), (1,)), ((), ())), preferred_element_type=F32)


def _split3(v):
    hi = v.astype(BF16)
    r1 = v - hi.astype(F32)
    mid = r1.astype(BF16)
    lo = (r1 - mid.astype(F32)).astype(BF16)
    return hi, mid, lo


def _loop(n, body, unroll=1):
    lax.fori_loop(0, n, lambda j, c: (body(j), c)[1], 0, unroll=unroll)


def _aligned(v):
    return pl.multiple_of(v, RUN_ALIGN)


def _run_copies(length, make_copy):
    n_chunks = lax.shift_right_logical(length, RUN_ALIGN.bit_length() - 1)
    off = jnp.int32(0)
    for k in reversed(range(RUN_CHUNK_BITS)):
        size = RUN_ALIGN << k
        bit = lax.shift_right_logical(n_chunks, k) & 1

        @pl.when(bit == 1)
        def _(off=off, size=size):
            make_copy(off, size).start()

        off = off + bit * size


def _combine(meta_ref, tile, n_tiles, x2, tail_ref, ys_hbm, stage, sem, lng, lnb):
    ts = x2.shape[0]
    slot = tile % 2

    def fetch(t, sl):
        for g in range(N_GROUPS):
            src = meta_ref[t * META_W + g]
            dst = meta_ref[t * META_W + 2 * N_GROUPS + g]
            _run_copies(
                meta_ref[t * META_W + N_GROUPS + g],
                lambda off, size, src=src, dst=dst: pltpu.make_async_copy(
                    ys_hbm.at[pl.ds(_aligned(src + off), size)],
                    stage.at[sl, pl.ds(_aligned(dst + off), size)], sem.at[sl]))

    @pl.when(tile == 0)
    def _():
        stage[...] = jnp.zeros(stage.shape, stage.dtype)
        fetch(tile, slot)

    @pl.when(tile + 1 < n_tiles)
    def _():
        fetch(tile + 1, 1 - slot)

    total = _aligned(meta_ref[tile * META_W + 3 * N_GROUPS])

    @pl.when(total > 0)
    def _():
        pltpu.make_async_copy(ys_hbm.at[pl.ds(0, total)], stage.at[slot, pl.ds(0, total)],
                              sem.at[slot]).wait()

    pos = tail_ref[:, STAGE_LANE:STAGE_LANE + 1].astype(I32)
    lane = lax.broadcasted_iota(I32, (ts, STAGE_ROWS_PADDED), 1)
    unsort = jnp.where(lane == pos, 1.0, 0.0).astype(BF16)
    y = _dot(unsort, stage[slot])
    return _layer_norm(ALPHA * x2 + y, lng, lnb)


_COMBINE_SCRATCH = lambda d: [pltpu.VMEM((2, STAGE_ROWS_PADDED, d), BF16), pltpu.SemaphoreType.DMA((2,))]


def _final_kernel(meta_ref, x2_ref, tail_ref, ys_hbm, lng_ref, lnb_ref, o_ref, stage, sem):
    o_ref[...] = _combine(meta_ref, pl.program_id(0), pl.num_programs(0), x2_ref[...], tail_ref,
                          ys_hbm, stage, sem, lng_ref[...], lnb_ref[...])


def _final_combine(meta, x2, tail, ys, lng, lnb):
    n_tok, d = x2.shape
    ts = tail.shape[0] // (meta.shape[0] // META_W)
    grid_spec = pltpu.PrefetchScalarGridSpec(
        num_scalar_prefetch=1,
        grid=(n_tok // ts,),
        in_specs=[
            pl.BlockSpec((ts, d), lambda i, m: (i, 0)),
            pl.BlockSpec((ts, TAIL_LANES), lambda i, m: (i, 0)),
            pl.BlockSpec(memory_space=pl.ANY),
            pl.BlockSpec(lng.shape, lambda i, m: (0, 0)),
            pl.BlockSpec(lnb.shape, lambda i, m: (0, 0)),
        ],
        out_specs=pl.BlockSpec((ts, d), lambda i, m: (i, 0)),
        scratch_shapes=_COMBINE_SCRATCH(d),
    )
    return pl.pallas_call(
        _final_kernel,
        out_shape=jax.ShapeDtypeStruct((n_tok, d), F32),
        grid_spec=grid_spec,
        compiler_params=pltpu.CompilerParams(
            dimension_semantics=("arbitrary",), vmem_limit_bytes=VMEM_LIMIT),
        name="final_combine",
    )(meta, x2, tail, ys, lng, lnb)


def _mixer_kernel(*refs, ts, lc, combine):
    if combine:
        (meta_ref, x_ref, tail_ref, ys_hbm, cg_ref, cb_ref), refs = refs[:6], refs[6:]
    else:
        x_ref, refs = refs[0], refs[1:]
    (wa_ref, wu_ref, wif_ref, bif_ref, conv_ref, hng_ref, poolw_ref, pscale_ref, wout_ref,
     lng_ref, lnb_ref, o_ref, zq_ext, u_ext, c_st, n_st, m_st) = refs[:17]
    s = pl.program_id(1)
    mw = N_HEADS * HEAD_DIM

    @pl.when(s == 0)
    def _():
        zq_ext[0:CONV_CARRY, :] = jnp.zeros((CONV_CARRY, 2 * mw), F32)
        u_ext[0:POOL_CARRY, :] = jnp.zeros((POOL_CARRY, u_ext.shape[1]), F32)
        c_st[...] = jnp.zeros(c_st.shape, F32)
        n_st[...] = jnp.zeros(n_st.shape, F32)
        m_st[...] = jnp.zeros(m_st.shape, F32)

    x = x_ref[0]
    if combine:
        stage, sem = refs[17:19]
        tile = pl.program_id(0) * pl.num_programs(1) + s
        x = _combine(meta_ref, tile, pl.num_programs(0) * pl.num_programs(1), x, tail_ref,
                     ys_hbm, stage, sem, cg_ref[...], cb_ref[...])
    xb = x.astype(BF16)
    z = _dot(xb, wa_ref[...])
    u = _dot(xb, wu_ref[...])
    gts = _dot(xb, wif_ref[...]) + bif_ref[...]

    zq_ext[CONV_CARRY:CONV_CARRY + ts, :] = z[:, :2 * mw]
    cw = conv_ref[...]
    acc = zq_ext[CONV_CARRY:CONV_CARRY + ts, :] * cw[CONV_WIDTH - 1:CONV_WIDTH, :]
    for j in range(1, CONV_WIDTH):
        acc = acc + zq_ext[CONV_CARRY - j:CONV_CARRY - j + ts, :] * cw[CONV_WIDTH - 1 - j:CONV_WIDTH - j, :]
    zq_ext[0:CONV_CARRY, :] = zq_ext[ts:ts + CONV_CARRY, :]
    qk = acc * _sigmoid(acc)
    q_all = qk[:, :mw] * (HEAD_DIM ** -0.5)
    k_all = qk[:, mw:]
    v_all = z[:, 2 * mw:3 * mw]
    o_all = z[:, 3 * mw:4 * mw]

    lf_all = jnp.minimum(gts, 0.0) - jnp.log1p(jnp.exp(-jnp.abs(gts)))

    row_i = lax.broadcasted_iota(I32, (lc, lc), 0)
    col_i = lax.broadcasted_iota(I32, (lc, lc), 1)
    causal = col_i <= row_i
    tri = jnp.where(causal, 1.0, 0.0).astype(BF16)

    head_out = [[] for _ in range(N_HEADS)]
    for c in range(ts // lc):
        rows = slice(c * lc, (c + 1) * lc)
        hi, mid, lo = _split3(lf_all[rows, :])
        b_all = _dot(tri, hi) + _dot(tri, mid) + _dot(tri, lo)
        g_c = gts[rows, :]
        r_all = g_c - pltpu.roll(b_all, LANES - N_HEADS, 1)
        r_t = r_all.T
        for h in range(N_HEADS):
            hs = slice(h * HEAD_DIM, (h + 1) * HEAD_DIM)
            qh = q_all[rows, hs]
            kh = k_all[rows, hs]
            vh = v_all[rows, hs].astype(BF16)
            qhb = qh.astype(BF16)
            bc = b_all[:, N_HEADS + h:N_HEADS + h + 1]
            igc = g_c[:, h:h + 1]
            r_row = r_t[h:h + 1, :]
            c_prev = c_st[h]
            n_prev = n_st[h]
            m_prev = m_st[h][:, 0:1]

            log_d = jnp.where(causal, bc + r_row, -jnp.inf)
            m_intra = jnp.max(log_d, axis=1, keepdims=True)
            log_inter = bc + m_prev
            m_t = jnp.maximum(m_intra, log_inter)
            p = jnp.exp(log_d - m_t) * _dot_nt(qhb, kh.astype(BF16))
            inter = jnp.exp(log_inter - m_t)
            num = _dot(p.astype(BF16), vh) + inter * _dot(qhb, c_prev.astype(BF16))
            den = (jnp.sum(p, axis=1, keepdims=True)
                   + inter * jnp.sum(qh * n_prev, axis=1, keepdims=True))
            hh = num / jnp.maximum(jnp.abs(den), jnp.exp(-m_t))

            b_last = bc[lc - 1:lc, :]
            w_state = b_last - bc + igc
            m_loc = jnp.max(w_state, axis=0, keepdims=True)
            ka = kh * jnp.exp(w_state - m_loc)
            c_loc = _dot(ka.T.astype(BF16), vh)
            n_loc = jnp.sum(ka, axis=0, keepdims=True)
            m_new = jnp.maximum(b_last + m_prev, m_loc)
            s_old = jnp.exp(b_last + m_prev - m_new)
            s_new = jnp.exp(m_loc - m_new)
            c_st[h] = s_old * c_prev + s_new * c_loc
            n_st[h] = s_old * n_prev + s_new * n_loc
            m_st[h] = jnp.broadcast_to(m_new, (1, LANES))

            mu = jnp.mean(hh, axis=1, keepdims=True)
            dlt = hh - mu
            var = jnp.mean(dlt * dlt, axis=1, keepdims=True)
            hn = dlt * lax.rsqrt(var + LN_EPS) * hng_ref[:, hs]
            head_out[h].append(hn * _sigmoid(o_all[rows, hs]))

    mixed = [jnp.concatenate(ho, axis=0) if len(ho) > 1 else ho[0] for ho in head_out]

    u_ext[POOL_CARRY:POOL_CARRY + ts, :] = u
    pos = (lax.broadcasted_iota(I32, (ts, 1), 0) + s * ts + 1).astype(F32)
    for g, w in enumerate(POOL_WINDOWS):
        cs = slice(g * POOL_GROUP, (g + 1) * POOL_GROUP)
        ug = u_ext[POOL_CARRY:POOL_CARRY + ts, cs]
        win = ug
        for j in range(1, w):
            win = win + u_ext[POOL_CARRY - j:POOL_CARRY - j + ts, cs]
        pooled = win / jnp.minimum(pos, float(w)) - ug
        pm = _dot(pooled.astype(BF16), poolw_ref[g]) * pscale_ref[:, cs]
        mixed.append(pm)
    u_ext[0:POOL_CARRY, :] = u_ext[ts:ts + POOL_CARRY, :]

    mixed = jnp.concatenate(mixed, axis=1).astype(BF16)
    y = _dot(mixed, wout_ref[...])
    o_ref[0] = _layer_norm(ALPHA * x + y, lng_ref[...], lnb_ref[...])


def _mixer(x, weights, combine=None):
    bsz, seq, d = x.shape
    ts = min(SEQ_TILE, seq)
    lc = min(MLSTM_CHUNK, ts)
    ns = seq // ts
    mw = N_HEADS * HEAD_DIM
    pw = weights[1].shape[1]
    const = lambda a: pl.BlockSpec(a.shape, lambda b, s, *_: (0,) * a.ndim)
    in_specs = [pl.BlockSpec((1, ts, d), lambda b, s, *_: (b, s, 0))]
    args = [x]
    scratch = [
        pltpu.VMEM((ts + CONV_CARRY, 2 * mw), F32),
        pltpu.VMEM((ts + POOL_CARRY, pw), F32),
        pltpu.VMEM((N_HEADS, HEAD_DIM, HEAD_DIM), F32),
        pltpu.VMEM((N_HEADS, 1, HEAD_DIM), F32),
        pltpu.VMEM((N_HEADS, 1, LANES), F32),
    ]
    prefetch = []
    if combine is not None:
        meta, tail, ys, cg, cb = combine
        prefetch = [meta]
        in_specs += [pl.BlockSpec((ts, TAIL_LANES), lambda b, s, *_: (b * ns + s, 0)),
                     pl.BlockSpec(memory_space=pl.ANY), const(cg), const(cb)]
        args += [tail, ys, cg, cb]
        scratch += _COMBINE_SCRATCH(d)
    in_specs += [const(w) for w in weights]
    args += list(weights)
    grid_spec = pltpu.PrefetchScalarGridSpec(
        num_scalar_prefetch=len(prefetch),
        grid=(bsz, ns),
        in_specs=in_specs,
        out_specs=pl.BlockSpec((1, ts, d), lambda b, s, *_: (b, s, 0)),
        scratch_shapes=scratch,
    )
    return pl.pallas_call(
        functools.partial(_mixer_kernel, ts=ts, lc=lc, combine=combine is not None),
        out_shape=jax.ShapeDtypeStruct((bsz, seq, d), F32),
        grid_spec=grid_spec,
        compiler_params=pltpu.CompilerParams(
            dimension_semantics=("arbitrary", "arbitrary"), vmem_limit_bytes=VMEM_LIMIT),
        name="mixer",
    )(*prefetch, *args)


def _top2_sum(a, b, c, d):
    hi1, lo1 = jnp.maximum(a, b), jnp.minimum(a, b)
    hi2, lo2 = jnp.maximum(c, d), jnp.minimum(c, d)
    return jnp.maximum(hi1, hi2) + jnp.maximum(jnp.minimum(hi1, hi2), jnp.maximum(lo1, lo2))


def _xattn_kernel(x_ref, mem_ref, wq_ref, wkv_ref, wo_ref, lng_ref, lnb_ref,
                  rwh_ref, rwm_ref, rwl_ref, rbias_ref,
                  x2_ref, tail_ref, meta_ref, xs_hbm, tls_hbm,
                  k_scr, v_scr, carry, stx, stt, mvec, msm, prev_total, zx, zt, sem, ssem, zsem,
                  *, ts, cap, blk):
    b = pl.program_id(0)
    s = pl.program_id(1)
    step = b * pl.num_programs(1) + s
    last = step == pl.num_programs(0) * pl.num_programs(1) - 1
    d = x_ref.shape[2]
    dh = d // XATTN_HEADS

    @pl.when(s == 0)
    def _():
        kv = _dot(mem_ref[0].astype(BF16), wkv_ref[...])
        k_scr[...] = kv[:, :d].astype(BF16)
        v_scr[...] = kv[:, d:].astype(BF16)

    @pl.when(step == 0)
    def _():
        carry[...] = jnp.zeros(carry.shape, F32)

    x = x_ref[0]
    q = (_dot(x.astype(BF16), wq_ref[...]) * (dh ** -0.5)).astype(BF16)
    outs = []
    for h in range(XATTN_HEADS):
        hs = slice(h * dh, (h + 1) * dh)
        sc = _dot_nt(q[:, hs], k_scr[:, hs])
        e = jnp.exp(sc - jnp.max(sc, axis=1, keepdims=True))
        l = jnp.sum(e, axis=1, keepdims=True)
        outs.append(_dot(e.astype(BF16), v_scr[:, hs]) * (1.0 / l))
    o = jnp.concatenate(outs, axis=1).astype(BF16)
    x2 = _layer_norm(ALPHA * x + _dot(o, wo_ref[...]), lng_ref[...], lnb_ref[...])
    x2_ref[0] = x2

    xh, xm, xl = _split3(x2)
    wh, wm, wl = rwh_ref[...], rwm_ref[...], rwl_ref[...]
    logits = (_dot(xh, wh) + (_dot(xh, wm) + _dot(xm, wh))
              + (_dot(xh, wl) + _dot(xm, wm) + _dot(xl, wh)))
    lt = logits.T[0:N_EXPERTS, :]
    score = _sigmoid(lt)
    sel = score + rbias_ref[...]

    sel_r = [sel[e:e + 1, :] for e in range(N_EXPERTS)]
    score_r = [score[e:e + 1, :] for e in range(N_EXPERTS)]
    gs = [_top2_sum(*sel_r[EXPERTS_PER_GROUP * g:EXPERTS_PER_GROUP * (g + 1)]) for g in range(N_GROUPS)]
    best = jnp.zeros((1, ts), I32)
    bestv = gs[0]
    for g in range(1, N_GROUPS):
        better = gs[g] > bestv
        best = jnp.where(better, g, best)
        bestv = jnp.where(better, gs[g], bestv)
    in_g = [best == g for g in range(N_GROUPS)]

    def pick(rows, j):
        out = rows[j]
        for g in range(1, N_GROUPS):
            out = jnp.where(in_g[g], rows[EXPERTS_PER_GROUP * g + j], out)
        return out

    vsel = [pick(sel_r, j) for j in range(EXPERTS_PER_GROUP)]
    vsc = [pick(score_r, j) for j in range(EXPERTS_PER_GROUP)]
    gates = []
    for j in range(EXPERTS_PER_GROUP):
        beaten = jnp.zeros((1, ts), I32)
        for k in range(EXPERTS_PER_GROUP):
            if k == j:
                continue
            wins = (vsel[k] > vsel[j]) | ((vsel[k] == vsel[j]) & (k < j))
            beaten = beaten + wins.astype(I32)
        gates.append(jnp.where(beaten < 2, vsc[j], 0.0))
    gsum = gates[0] + gates[1] + gates[2] + gates[3]
    gates = [g / gsum for g in gates]

    sub = lax.broadcasted_iota(I32, (SUBLANES, ts), 0)
    oh8 = jnp.zeros((SUBLANES, ts), F32)
    for g in range(N_GROUPS):
        oh8 = jnp.where((sub == g) & in_g[g], 1.0, oh8)
    r_i = lax.broadcasted_iota(I32, (ts, ts), 0)
    c_i = lax.broadcasted_iota(I32, (ts, ts), 1)
    upper = jnp.where(r_i < c_i, 1.0, 0.0).astype(BF16)
    excl = _dot(oh8.astype(BF16), upper)
    n8 = jnp.sum(oh8, axis=1, keepdims=True)
    len8 = jnp.floor((n8 + (RUN_ALIGN - 1)) * (1.0 / RUN_ALIGN)) * RUN_ALIGN
    sub1 = lax.broadcasted_iota(I32, (SUBLANES, 1), 0)
    off8 = jnp.zeros((SUBLANES, 1), F32)
    run_off = jnp.zeros((1, 1), F32)
    for g in range(N_GROUPS):
        off8 = jnp.where(sub1 == g, run_off, off8)
        run_off = run_off + len8[g:g + 1, :]
    total = run_off
    pos = jnp.sum(jnp.where(oh8 > 0.0, off8 + excl, 0.0), axis=0, keepdims=True)
    base = carry[:, 0:1]
    new_base = base + len8
    carry[...] = jnp.broadcast_to(new_base, carry.shape)

    t8 = jnp.where(sub == STAGE_LANE, pos, 0.0)
    for j in range(EXPERTS_PER_GROUP):
        t8 = jnp.where(sub == j, gates[j], t8)
    tail = jnp.concatenate([t8, jnp.zeros((TAIL_LANES - SUBLANES, ts), F32)], axis=0).T
    tail_ref[...] = tail

    lane8 = lax.broadcasted_iota(I32, (SUBLANES, LANES), 1)
    slot8 = sub1.astype(F32) * float(cap) + base
    mv = jnp.where(lane8 == M_SLOT, slot8, 0.0)
    mv = jnp.where(lane8 == M_LEN, len8, mv)
    mv = jnp.where(lane8 == M_OFF, off8, mv)
    mv = jnp.where(lane8 == M_TOTAL, total, mv)
    mv = jnp.where(lane8 == M_END, new_base, mv).astype(I32)
    meta_ref[0] = mv

    srow = lax.broadcasted_iota(I32, (STAGE_ROWS, ts), 0)
    sort = jnp.where(srow == pos.astype(I32), 1.0, 0.0).astype(BF16)
    xs_sorted = _dot(sort, xh).astype(BF16)
    th, tm, tl = _split3(tail)
    tail_sorted = (_dot(sort, th) + _dot(sort, tm)) + _dot(sort, tl)

    def wait_runs(n_rows):
        pltpu.make_async_copy(stx.at[pl.ds(0, n_rows)], xs_hbm.at[pl.ds(0, n_rows)], sem.at[0]).wait()
        pltpu.make_async_copy(stt.at[pl.ds(0, n_rows)], tls_hbm.at[pl.ds(0, n_rows)], sem.at[1]).wait()

    @pl.when(step > 0)
    def _():
        n_prev = _aligned(prev_total[0])

        @pl.when(n_prev > 0)
        def _():
            wait_runs(n_prev)

    stx[...] = xs_sorted
    stt[...] = tail_sorted
    mvec[...] = mv
    to_smem = pltpu.make_async_copy(mvec, msm, ssem)
    to_smem.start()
    to_smem.wait()

    for g in range(N_GROUPS):
        dst = msm[g, M_SLOT]
        src = msm[g, M_OFF]
        _run_copies(msm[g, M_LEN], lambda off, size, src=src, dst=dst: pltpu.make_async_copy(
            stx.at[pl.ds(_aligned(src + off), size)], xs_hbm.at[pl.ds(_aligned(dst + off), size)], sem.at[0]))
        _run_copies(msm[g, M_LEN], lambda off, size, src=src, dst=dst: pltpu.make_async_copy(
            stt.at[pl.ds(_aligned(src + off), size)], tls_hbm.at[pl.ds(_aligned(dst + off), size)], sem.at[1]))
    prev_total[0] = msm[0, M_TOTAL]

    @pl.when(last)
    def _():
        n_own = _aligned(msm[0, M_TOTAL])

        @pl.when(n_own > 0)
        def _():
            wait_runs(n_own)

        zx[...] = jnp.zeros(zx.shape, BF16)
        zt[...] = jnp.zeros(zt.shape, F32)
        for g in range(N_GROUPS):
            end = msm[g, M_END]
            n_pad = (blk - end % blk) % blk
            n_chunks = lax.shift_right_logical(n_pad, RUN_ALIGN.bit_length() - 1)
            off = g * cap + end
            for k in reversed(range(RUN_CHUNK_BITS - 1)):
                size = RUN_ALIGN << k
                bit = lax.shift_right_logical(n_chunks, k) & 1

                @pl.when(bit == 1)
                def _(off=off, size=size):
                    cx = pltpu.make_async_copy(zx.at[pl.ds(0, size)], xs_hbm.at[pl.ds(_aligned(off), size)], zsem)
                    ct = pltpu.make_async_copy(zt.at[pl.ds(0, size)], tls_hbm.at[pl.ds(_aligned(off), size)], zsem)
                    cx.start()
                    ct.start()
                    cx.wait()
                    ct.wait()

                off = off + bit * size


def _xattn_router(x, mem, wq, wkv, wo, lng, lnb, rwh, rwm, rwl, rbias, cap, blk):
    bsz, seq, d = x.shape
    mlen = mem.shape[1]
    ts = min(SEQ_TILE, seq)
    assert ts == SEQ_TILE and blk % RUN_ALIGN == 0 and blk <= RUN_ALIGN << (RUN_CHUNK_BITS - 1)
    ns = seq // ts
    const = lambda a: pl.BlockSpec(a.shape, lambda b, s: (0,) * a.ndim)
    zero_rows = RUN_ALIGN << (RUN_CHUNK_BITS - 2)
    return pl.pallas_call(
        functools.partial(_xattn_kernel, ts=ts, cap=cap, blk=blk),
        out_shape=(
            jax.ShapeDtypeStruct((bsz, seq, d), F32),
            jax.ShapeDtypeStruct((bsz * seq, TAIL_LANES), F32),
            jax.ShapeDtypeStruct((bsz * ns, SUBLANES, LANES), I32),
            jax.ShapeDtypeStruct((N_GROUPS * cap, d), BF16),
            jax.ShapeDtypeStruct((N_GROUPS * cap, TAIL_LANES), F32),
        ),
        grid=(bsz, ns),
        in_specs=[
            pl.BlockSpec((1, ts, d), lambda b, s: (b, s, 0)),
            pl.BlockSpec((1, mlen, d), lambda b, s: (b, 0, 0)),
            const(wq), const(wkv), const(wo), const(lng), const(lnb),
            const(rwh), const(rwm), const(rwl), const(rbias),
        ],
        out_specs=(
            pl.BlockSpec((1, ts, d), lambda b, s: (b, s, 0)),
            pl.BlockSpec((ts, TAIL_LANES), lambda b, s: (b * ns + s, 0)),
            pl.BlockSpec((1, SUBLANES, LANES), lambda b, s: (b * ns + s, 0, 0)),
            pl.BlockSpec(memory_space=pl.ANY),
            pl.BlockSpec(memory_space=pl.ANY),
        ),
        scratch_shapes=[
            pltpu.VMEM((mlen, d), BF16),
            pltpu.VMEM((mlen, d), BF16),
            pltpu.VMEM((SUBLANES, LANES), F32),
            pltpu.VMEM((STAGE_ROWS, d), BF16),
            pltpu.VMEM((STAGE_ROWS, TAIL_LANES), F32),
            pltpu.VMEM((SUBLANES, LANES), I32),
            pltpu.SMEM((SUBLANES, LANES), I32),
            pltpu.SMEM((1,), I32),
            pltpu.VMEM((zero_rows, d), BF16),
            pltpu.VMEM((zero_rows, TAIL_LANES), F32),
            pltpu.SemaphoreType.DMA((2,)),
            pltpu.SemaphoreType.DMA,
            pltpu.SemaphoreType.DMA,
        ],
        compiler_params=pltpu.CompilerParams(
            dimension_semantics=("arbitrary", "arbitrary"), vmem_limit_bytes=VMEM_LIMIT,
            has_side_effects=True),
        name="xattn_router",
    )(x, mem, wq, wkv, wo, lng, lnb, rwh, rwm, rwl, rbias)


def _ffn_kernel(blk_in_ref, grp_ref, used_ref, xs_ref, tl_ref, wg_ref, wu_ref, wd_ref, o_ref):
    de = wg_ref.shape[2] // EXPERTS_PER_GROUP

    @pl.when(used_ref[pl.program_id(0)] == 1)
    def _():
        xb = xs_ref[...]
        hg = _dot(xb, wg_ref[0])
        hu = _dot(xb, wu_ref[0])
        hid = hg * _sigmoid(hg) * hu
        parts = []
        for j in range(EXPERTS_PER_GROUP):
            gate = tl_ref[:, j:j + 1]
            hj = hid[:, j * de:(j + 1) * de]
            parts.append(jnp.where(gate != 0.0, hj * gate, 0.0))
        hid = jnp.concatenate(parts, axis=1).astype(BF16)
        o_ref[...] = _dot(hid, wd_ref[0]).astype(BF16)


def _ffn(blk_in, blk_grp, used, xs, tls, wg, wu, wd):
    d = xs.shape[1]
    blk = FFN_BLOCK
    grid_spec = pltpu.PrefetchScalarGridSpec(
        num_scalar_prefetch=3,
        grid=(blk_in.shape[0],),
        in_specs=[
            pl.BlockSpec((blk, d), lambda i, bi, grp, us: (bi[i], 0)),
            pl.BlockSpec((blk, TAIL_LANES), lambda i, bi, grp, us: (bi[i], 0)),
            pl.BlockSpec((1,) + wg.shape[1:], lambda i, bi, grp, us: (grp[i], 0, 0)),
            pl.BlockSpec((1,) + wu.shape[1:], lambda i, bi, grp, us: (grp[i], 0, 0)),
            pl.BlockSpec((1,) + wd.shape[1:], lambda i, bi, grp, us: (grp[i], 0, 0)),
        ],
        out_specs=pl.BlockSpec((blk, d), lambda i, bi, grp, us: (bi[i], 0)),
    )
    return pl.pallas_call(
        _ffn_kernel,
        out_shape=jax.ShapeDtypeStruct(xs.shape, BF16),
        grid_spec=grid_spec,
        compiler_params=pltpu.CompilerParams(
            dimension_semantics=("arbitrary",), vmem_limit_bytes=VMEM_LIMIT),
        name="group_ffn",
    )(blk_in, blk_grp, used, xs, tls, wg, wu, wd)


def _group_weights(w):
    e, a, b = w.shape
    return w.reshape(N_GROUPS, EXPERTS_PER_GROUP, a, b).transpose(0, 2, 1, 3).reshape(
        N_GROUPS, a, EXPERTS_PER_GROUP * b).astype(BF16)


def _block_tables(seg_rows, cap, blk, n_steps):
    nblk = (seg_rows + blk - 1) // blk
    bend = jnp.cumsum(nblk)
    bstart = bend - nblk
    step = jnp.arange(n_steps, dtype=I32)
    used = step < bend[-1]
    grp = jnp.minimum(jnp.searchsorted(bend, step, side="right"), N_GROUPS - 1).astype(I32)
    blk_in = grp * (cap // blk) + step - bstart[grp]
    last_real = jnp.maximum(bend[-1] - 1, 0)
    blk_in = jnp.where(used, blk_in, blk_in[last_real])
    grp = jnp.where(used, grp, grp[last_real])
    return blk_in.astype(I32), grp.astype(I32), used.astype(I32)


def _flat_meta(meta):
    m = meta[:, :N_GROUPS, :]
    rec = jnp.concatenate([m[:, :, M_SLOT], m[:, :, M_LEN], m[:, :, M_OFF], m[:, :1, M_TOTAL],
                           jnp.zeros((m.shape[0], META_W - 3 * N_GROUPS - 1), I32)], axis=1)
    return rec.reshape(-1)


def kernel(x, mem, w_in, b_i, b_f, conv_qk, head_norm_g, pool_w, pool_scale, w_mix_out,
           ln_mix_g, ln_mix_b, w_xq, w_xkv, w_xo, ln_x_g, ln_x_b, router_w, router_bias,
           w_gate, w_up, w_down, ln_moe_g, ln_moe_b):
    bsz, seq, d = x.shape
    n_tok = bsz * seq
    n_tiles = n_tok // SEQ_TILE
    mw = N_HEADS * HEAD_DIM
    n_gate = 2 * N_HEADS
    blk = FFN_BLOCK
    cap = -(-(n_tok + RUN_ALIGN * n_tiles) // blk) * blk
    n_steps = (n_tok + N_GROUPS * RUN_ALIGN * n_tiles) // blk + N_GROUPS

    rw = jnp.pad(router_w, ((0, 0), (0, LANES - N_EXPERTS)))
    rwh, rwm, rwl = _split3(rw)
    rbias = router_bias.reshape(N_EXPERTS, 1).astype(F32)
    row = lambda v: v.reshape(1, -1).astype(F32)

    combine = None
    for l in range(DEPTH):
        wa = w_in[l][:, :4 * mw].astype(BF16)
        wu = w_in[l][:, 4 * mw + n_gate:].astype(BF16)
        wif = jnp.pad(w_in[l][:, 4 * mw:4 * mw + n_gate], ((0, 0), (0, LANES - n_gate))).astype(BF16)
        bif = jnp.pad(jnp.concatenate([b_i[l], b_f[l]]), (0, LANES - n_gate)).reshape(1, LANES)
        weights = (wa, wu, wif, bif, conv_qk[l], row(head_norm_g[l]), pool_w[l].astype(BF16),
                   row(pool_scale[l]), w_mix_out[l].astype(BF16), row(ln_mix_g[l]), row(ln_mix_b[l]))
        x = _mixer(x, weights, combine)

        x2, tail, meta, xs, tls = _xattn_router(
            x, mem, w_xq[l].astype(BF16), w_xkv[l].astype(BF16), w_xo[l].astype(BF16),
            row(ln_x_g[l]), row(ln_x_b[l]), rwh, rwm, rwl, rbias, cap, blk)

        seg_rows = meta[-1, :N_GROUPS, M_END]
        blk_in, blk_grp, used = _block_tables(seg_rows, cap, blk, n_steps)
        ys = _ffn(blk_in, blk_grp, used, xs, tls, _group_weights(w_gate[l]), _group_weights(w_up[l]),
                  w_down[l].reshape(N_GROUPS, EXPERTS_PER_GROUP * w_down.shape[2], d).astype(BF16))
        x = x2
        combine = (_flat_meta(meta), tail, ys, row(ln_moe_g[l]), row(ln_moe_b[l]))

    meta, tail, ys, cg, cb = combine
    return _final_combine(meta, x.reshape(n_tok, d), tail, ys, cg, cb).reshape(bsz, seq, d)
```

```python
import functools

import jax
import jax.numpy as jnp
from jax import lax
from jax.experimental import pallas as pl
from jax.experimental.pallas import tpu as pltpu

F32 = jnp.float32
BF16 = jnp.bfloat16
I32 = jnp.int32

N_HEADS = 4
HEAD_DIM = 128
POOL_WINDOWS = (2, 4, 8, 16)
POOL_GROUP = 128
CONV_WIDTH = 4
XATTN_HEADS = 4
N_EXPERTS = 16
N_GROUPS = 4
EXPERTS_PER_GROUP = 4
DEPTH = 2
ALPHA = (2 * DEPTH) ** 0.25
LN_EPS = 1e-5

LANES = 128
SUBLANES = 8
BF16_TILE_ROWS = 16
VMEM_LIMIT = 56 * 1024 * 1024

SEQ_TILE = 256
MLSTM_CHUNK = 256
FFN_BLOCK = 256
CONV_CARRY = 8
POOL_CARRY = 16

RUN_ALIGN = BF16_TILE_ROWS
STAGE_ROWS = SEQ_TILE + N_GROUPS * RUN_ALIGN
STAGE_ROWS_PADDED = 384
TAIL_LANES = 128
STAGE_LANE = 5
M_SLOT, M_LEN, M_OFF, M_TOTAL, M_END = 0, 1, 2, 3, 4
META_W = 16


def _layer_norm(y, g, b):
    mu = jnp.mean(y, axis=-1, keepdims=True)
    d = y - mu
    var = jnp.mean(d * d, axis=-1, keepdims=True)
    return d * lax.rsqrt(var + LN_EPS) * g + b


def _sigmoid(v):
    return 1.0 / (1.0 + jnp.exp(-v))


def _dot(a, b):
    return jnp.dot(a, b, preferred_element_type=F32)


def _dot_nt(a, b):
    return lax.dot_general(a, b, (((1,), (1,)), ((), ())), preferred_element_type=F32)


def _split3(v):
    hi = v.astype(BF16)
    r1 = v - hi.astype(F32)
    mid = r1.astype(BF16)
    lo = (r1 - mid.astype(F32)).astype(BF16)
    return hi, mid, lo


def _loop(n, body, unroll=1):
    lax.fori_loop(0, n, lambda j, c: (body(j), c)[1], 0, unroll=unroll)


def _aligned(v):
    return pl.multiple_of(v, RUN_ALIGN)


def _run_copy(src, src_row, dst, dst_row, length, sem):
    length = _aligned(length)

    @pl.when(length > 0)
    def _():
        pltpu.make_async_copy(src.at[pl.ds(_aligned(src_row), length)],
                              dst.at[pl.ds(_aligned(dst_row), length)], sem).start()


def _combine(meta_ref, tile, n_tiles, x2, tail_ref, ys_hbm, stage, sem, lng, lnb):
    ts = x2.shape[0]
    slot = tile % 2

    def fetch(t, sl):
        for g in range(N_GROUPS):
            _run_copy(ys_hbm, meta_ref[t * META_W + g],
                      stage.at[sl], meta_ref[t * META_W + 2 * N_GROUPS + g],
                      meta_ref[t * META_W + N_GROUPS + g], sem.at[sl])

    @pl.when(tile == 0)
    def _():
        stage[...] = jnp.zeros(stage.shape, stage.dtype)
        fetch(tile, slot)

    @pl.when(tile + 1 < n_tiles)
    def _():
        fetch(tile + 1, 1 - slot)

    total = _aligned(meta_ref[tile * META_W + 3 * N_GROUPS])

    @pl.when(total > 0)
    def _():
        pltpu.make_async_copy(ys_hbm.at[pl.ds(0, total)], stage.at[slot, pl.ds(0, total)],
                              sem.at[slot]).wait()

    pos = tail_ref[:, STAGE_LANE:STAGE_LANE + 1].astype(I32)
    lane = lax.broadcasted_iota(I32, (ts, STAGE_ROWS_PADDED), 1)
    unsort = jnp.where(lane == pos, 1.0, 0.0).astype(BF16)
    y = _dot(unsort, stage[slot])
    return _layer_norm(ALPHA * x2 + y, lng, lnb)


_COMBINE_SCRATCH = lambda d: [pltpu.VMEM((2, STAGE_ROWS_PADDED, d), BF16), pltpu.SemaphoreType.DMA((2,))]


def _final_kernel(meta_ref, x2_ref, tail_ref, ys_hbm, lng_ref, lnb_ref, o_ref, stage, sem):
    o_ref[...] = _combine(meta_ref, pl.program_id(0), pl.num_programs(0), x2_ref[...], tail_ref,
                          ys_hbm, stage, sem, lng_ref[...], lnb_ref[...])


def _final_combine(meta, x2, tail, ys, lng, lnb):
    n_tok, d = x2.shape
    ts = tail.shape[0] // (meta.shape[0] // META_W)
    grid_spec = pltpu.PrefetchScalarGridSpec(
        num_scalar_prefetch=1,
        grid=(n_tok // ts,),
        in_specs=[
            pl.BlockSpec((ts, d), lambda i, m: (i, 0)),
            pl.BlockSpec((ts, TAIL_LANES), lambda i, m: (i, 0)),
            pl.BlockSpec(memory_space=pl.ANY),
            pl.BlockSpec(lng.shape, lambda i, m: (0, 0)),
            pl.BlockSpec(lnb.shape, lambda i, m: (0, 0)),
        ],
        out_specs=pl.BlockSpec((ts, d), lambda i, m: (i, 0)),
        scratch_shapes=_COMBINE_SCRATCH(d),
    )
    return pl.pallas_call(
        _final_kernel,
        out_shape=jax.ShapeDtypeStruct((n_tok, d), F32),
        grid_spec=grid_spec,
        compiler_params=pltpu.CompilerParams(
            dimension_semantics=("arbitrary",), vmem_limit_bytes=VMEM_LIMIT),
        name="final_combine",
    )(meta, x2, tail, ys, lng, lnb)


def _mixer_kernel(*refs, ts, lc, combine):
    if combine:
        (meta_ref, x_ref, tail_ref, ys_hbm, cg_ref, cb_ref), refs = refs[:6], refs[6:]
    else:
        x_ref, refs = refs[0], refs[1:]
    (wa_ref, wu_ref, wif_ref, bif_ref, conv_ref, hng_ref, poolw_ref, pscale_ref, wout_ref,
     lng_ref, lnb_ref, o_ref, zq_ext, u_ext, c_st, n_st, m_st) = refs[:17]
    s = pl.program_id(1)
    mw = N_HEADS * HEAD_DIM

    @pl.when(s == 0)
    def _():
        zq_ext[0:CONV_CARRY, :] = jnp.zeros((CONV_CARRY, 2 * mw), F32)
        u_ext[0:POOL_CARRY, :] = jnp.zeros((POOL_CARRY, u_ext.shape[1]), F32)
        c_st[...] = jnp.zeros(c_st.shape, F32)
        n_st[...] = jnp.zeros(n_st.shape, F32)
        m_st[...] = jnp.zeros(m_st.shape, F32)

    x = x_ref[0]
    if combine:
        stage, sem = refs[17:19]
        tile = pl.program_id(0) * pl.num_programs(1) + s
        x = _combine(meta_ref, tile, pl.num_programs(0) * pl.num_programs(1), x, tail_ref,
                     ys_hbm, stage, sem, cg_ref[...], cb_ref[...])
    xb = x.astype(BF16)
    z = _dot(xb, wa_ref[...])
    u = _dot(xb, wu_ref[...])
    gts = _dot(xb, wif_ref[...]) + bif_ref[...]

    zq_ext[CONV_CARRY:CONV_CARRY + ts, :] = z[:, :2 * mw]
    cw = conv_ref[...]
    acc = zq_ext[CONV_CARRY:CONV_CARRY + ts, :] * cw[CONV_WIDTH - 1:CONV_WIDTH, :]
    for j in range(1, CONV_WIDTH):
        acc = acc + zq_ext[CONV_CARRY - j:CONV_CARRY - j + ts, :] * cw[CONV_WIDTH - 1 - j:CONV_WIDTH - j, :]
    zq_ext[0:CONV_CARRY, :] = zq_ext[ts:ts + CONV_CARRY, :]
    qk = acc * _sigmoid(acc)
    q_all = qk[:, :mw] * (HEAD_DIM ** -0.5)
    k_all = qk[:, mw:]
    v_all = z[:, 2 * mw:3 * mw]
    o_all = z[:, 3 * mw:4 * mw]

    lf_all = jnp.minimum(gts, 0.0) - jnp.log1p(jnp.exp(-jnp.abs(gts)))

    row_i = lax.broadcasted_iota(I32, (lc, lc), 0)
    col_i = lax.broadcasted_iota(I32, (lc, lc), 1)
    causal = col_i <= row_i
    tri = jnp.where(causal, 1.0, 0.0).astype(BF16)

    head_out = [[] for _ in range(N_HEADS)]
    for c in range(ts // lc):
        rows = slice(c * lc, (c + 1) * lc)
        hi, mid, lo = _split3(lf_all[rows, :])
        b_all = _dot(tri, hi) + _dot(tri, mid) + _dot(tri, lo)
        g_c = gts[rows, :]
        r_all = g_c - pltpu.roll(b_all, LANES - N_HEADS, 1)
        r_t = r_all.T
        for h in range(N_HEADS):
            hs = slice(h * HEAD_DIM, (h + 1) * HEAD_DIM)
            qh = q_all[rows, hs]
            kh = k_all[rows, hs]
            vh = v_all[rows, hs].astype(BF16)
            qhb = qh.astype(BF16)
            bc = b_all[:, N_HEADS + h:N_HEADS + h + 1]
            igc = g_c[:, h:h + 1]
            r_row = r_t[h:h + 1, :]
            c_prev = c_st[h]
            n_prev = n_st[h]
            m_prev = m_st[h][:, 0:1]

            log_d = jnp.where(causal, bc + r_row, -jnp.inf)
            m_intra = jnp.max(log_d, axis=1, keepdims=True)
            log_inter = bc + m_prev
            m_t = jnp.maximum(m_intra, log_inter)
            p = jnp.exp(log_d - m_t) * _dot_nt(qhb, kh.astype(BF16))
            inter = jnp.exp(log_inter - m_t)
            num = _dot(p.astype(BF16), vh) + inter * _dot(qhb, c_prev.astype(BF16))
            den = (jnp.sum(p, axis=1, keepdims=True)
                   + inter * jnp.sum(qh * n_prev, axis=1, keepdims=True))
            hh = num / jnp.maximum(jnp.abs(den), jnp.exp(-m_t))

            b_last = bc[lc - 1:lc, :]
            w_state = b_last - bc + igc
            m_loc = jnp.max(w_state, axis=0, keepdims=True)
            ka = kh * jnp.exp(w_state - m_loc)
            c_loc = _dot(ka.T.astype(BF16), vh)
            n_loc = jnp.sum(ka, axis=0, keepdims=True)
            m_new = jnp.maximum(b_last + m_prev, m_loc)
            s_old = jnp.exp(b_last + m_prev - m_new)
            s_new = jnp.exp(m_loc - m_new)
            c_st[h] = s_old * c_prev + s_new * c_loc
            n_st[h] = s_old * n_prev + s_new * n_loc
            m_st[h] = jnp.broadcast_to(m_new, (1, LANES))

            mu = jnp.mean(hh, axis=1, keepdims=True)
            dlt = hh - mu
            var = jnp.mean(dlt * dlt, axis=1, keepdims=True)
            hn = dlt * lax.rsqrt(var + LN_EPS) * hng_ref[:, hs]
            head_out[h].append(hn * _sigmoid(o_all[rows, hs]))

    mixed = [jnp.concatenate(ho, axis=0) if len(ho) > 1 else ho[0] for ho in head_out]

    u_ext[POOL_CARRY:POOL_CARRY + ts, :] = u
    pos = (lax.broadcasted_iota(I32, (ts, 1), 0) + s * ts + 1).astype(F32)
    for g, w in enumerate(POOL_WINDOWS):
        cs = slice(g * POOL_GROUP, (g + 1) * POOL_GROUP)
        ug = u_ext[POOL_CARRY:POOL_CARRY + ts, cs]
        win = ug
        for j in range(1, w):
            win = win + u_ext[POOL_CARRY - j:POOL_CARRY - j + ts, cs]
        pooled = win / jnp.minimum(pos, float(w)) - ug
        pm = _dot(pooled.astype(BF16), poolw_ref[g]) * pscale_ref[:, cs]
        mixed.append(pm)
    u_ext[0:POOL_CARRY, :] = u_ext[ts:ts + POOL_CARRY, :]

    mixed = jnp.concatenate(mixed, axis=1).astype(BF16)
    y = _dot(mixed, wout_ref[...])
    o_ref[0] = _layer_norm(ALPHA * x + y, lng_ref[...], lnb_ref[...])


def _mixer(x, weights, combine=None):
    bsz, seq, d = x.shape
    ts = min(SEQ_TILE, seq)
    lc = min(MLSTM_CHUNK, ts)
    ns = seq // ts
    mw = N_HEADS * HEAD_DIM
    pw = weights[1].shape[1]
    const = lambda a: pl.BlockSpec(a.shape, lambda b, s, *_: (0,) * a.ndim)
    in_specs = [pl.BlockSpec((1, ts, d), lambda b, s, *_: (b, s, 0))]
    args = [x]
    scratch = [
        pltpu.VMEM((ts + CONV_CARRY, 2 * mw), F32),
        pltpu.VMEM((ts + POOL_CARRY, pw), F32),
        pltpu.VMEM((N_HEADS, HEAD_DIM, HEAD_DIM), F32),
        pltpu.VMEM((N_HEADS, 1, HEAD_DIM), F32),
        pltpu.VMEM((N_HEADS, 1, LANES), F32),
    ]
    prefetch = []
    if combine is not None:
        meta, tail, ys, cg, cb = combine
        prefetch = [meta]
        in_specs += [pl.BlockSpec((ts, TAIL_LANES), lambda b, s, *_: (b * ns + s, 0)),
                     pl.BlockSpec(memory_space=pl.ANY), const(cg), const(cb)]
        args += [tail, ys, cg, cb]
        scratch += _COMBINE_SCRATCH(d)
    in_specs += [const(w) for w in weights]
    args += list(weights)
    grid_spec = pltpu.PrefetchScalarGridSpec(
        num_scalar_prefetch=len(prefetch),
        grid=(bsz, ns),
        in_specs=in_specs,
        out_specs=pl.BlockSpec((1, ts, d), lambda b, s, *_: (b, s, 0)),
        scratch_shapes=scratch,
    )
    return pl.pallas_call(
        functools.partial(_mixer_kernel, ts=ts, lc=lc, combine=combine is not None),
        out_shape=jax.ShapeDtypeStruct((bsz, seq, d), F32),
        grid_spec=grid_spec,
        compiler_params=pltpu.CompilerParams(
            dimension_semantics=("arbitrary", "arbitrary"), vmem_limit_bytes=VMEM_LIMIT),
        name="mixer",
    )(*prefetch, *args)


def _top2_sum(a, b, c, d):
    hi1, lo1 = jnp.maximum(a, b), jnp.minimum(a, b)
    hi2, lo2 = jnp.maximum(c, d), jnp.minimum(c, d)
    return jnp.maximum(hi1, hi2) + jnp.maximum(jnp.minimum(hi1, hi2), jnp.maximum(lo1, lo2))


def _xattn_kernel(x_ref, mem_ref, wq_ref, wkv_ref, wo_ref, lng_ref, lnb_ref,
                  rw2_ref, rbias_ref,
                  x2_ref, tail_ref, meta_ref, xs_hbm, tls_hbm,
                  k_scr, v_scr, carry, stx, stt, mvec, msm, prev_total, zx, zt, sem, ssem, zsem,
                  *, ts, cap, blk):
    b = pl.program_id(0)
    s = pl.program_id(1)
    step = b * pl.num_programs(1) + s
    last = step == pl.num_programs(0) * pl.num_programs(1) - 1
    d = x_ref.shape[2]
    dh = d // XATTN_HEADS

    @pl.when(s == 0)
    def _():
        kv = _dot(mem_ref[0].astype(BF16), wkv_ref[...])
        k_scr[...] = kv[:, :d].astype(BF16)
        v_scr[...] = kv[:, d:].astype(BF16)

    @pl.when(step == 0)
    def _():
        carry[...] = jnp.zeros(carry.shape, F32)

    x = x_ref[0]
    q = (_dot(x.astype(BF16), wq_ref[...]) * (dh ** -0.5)).astype(BF16)
    outs = []
    for h in range(XATTN_HEADS):
        hs = slice(h * dh, (h + 1) * dh)
        sc = _dot_nt(q[:, hs], k_scr[:, hs])
        e = jnp.exp(sc - jnp.max(sc, axis=1, keepdims=True))
        l = jnp.sum(e, axis=1, keepdims=True)
        outs.append(_dot(e.astype(BF16), v_scr[:, hs]) * (1.0 / l))
    o = jnp.concatenate(outs, axis=1).astype(BF16)
    x2 = _layer_norm(ALPHA * x + _dot(o, wo_ref[...]), lng_ref[...], lnb_ref[...])
    x2_ref[0] = x2

    xh, xm, _ = _split3(x2)
    both = _dot(xh, rw2_ref[...])
    logits = (both[:, :LANES] + both[:, LANES:]) + _dot(xm, rw2_ref[:, :LANES])
    lt = logits.T[0:N_EXPERTS, :]
    score = _sigmoid(lt)
    sel = score + rbias_ref[...]

    sel_r = [sel[e:e + 1, :] for e in range(N_EXPERTS)]
    score_r = [score[e:e + 1, :] for e in range(N_EXPERTS)]
    gs = [_top2_sum(*sel_r[EXPERTS_PER_GROUP * g:EXPERTS_PER_GROUP * (g + 1)]) for g in range(N_GROUPS)]
    best = jnp.zeros((1, ts), I32)
    bestv = gs[0]
    for g in range(1, N_GROUPS):
        better = gs[g] > bestv
        best = jnp.where(better, g, best)
        bestv = jnp.where(better, gs[g], bestv)
    in_g = [best == g for g in range(N_GROUPS)]

    def pick(rows, j):
        out = rows[j]
        for g in range(1, N_GROUPS):
            out = jnp.where(in_g[g], rows[EXPERTS_PER_GROUP * g + j], out)
        return out

    vsel = [pick(sel_r, j) for j in range(EXPERTS_PER_GROUP)]
    vsc = [pick(score_r, j) for j in range(EXPERTS_PER_GROUP)]
    gates = []
    for j in range(EXPERTS_PER_GROUP):
        beaten = jnp.zeros((1, ts), I32)
        for k in range(EXPERTS_PER_GROUP):
            if k == j:
                continue
            wins = (vsel[k] > vsel[j]) | ((vsel[k] == vsel[j]) & (k < j))
            beaten = beaten + wins.astype(I32)
        gates.append(jnp.where(beaten < 2, vsc[j], 0.0))
    gsum = gates[0] + gates[1] + gates[2] + gates[3]
    gates = [g / gsum for g in gates]

    sub = lax.broadcasted_iota(I32, (SUBLANES, ts), 0)
    oh8 = jnp.zeros((SUBLANES, ts), F32)
    for g in range(N_GROUPS):
        oh8 = jnp.where((sub == g) & in_g[g], 1.0, oh8)
    r_i = lax.broadcasted_iota(I32, (ts, ts), 0)
    c_i = lax.broadcasted_iota(I32, (ts, ts), 1)
    upper = jnp.where(r_i < c_i, 1.0, 0.0).astype(BF16)
    excl = _dot(oh8.astype(BF16), upper)
    n8 = jnp.sum(oh8, axis=1, keepdims=True)
    len8 = jnp.floor((n8 + (RUN_ALIGN - 1)) * (1.0 / RUN_ALIGN)) * RUN_ALIGN
    sub1 = lax.broadcasted_iota(I32, (SUBLANES, 1), 0)
    off8 = jnp.zeros((SUBLANES, 1), F32)
    run_off = jnp.zeros((1, 1), F32)
    for g in range(N_GROUPS):
        off8 = jnp.where(sub1 == g, run_off, off8)
        run_off = run_off + len8[g:g + 1, :]
    total = run_off
    pos = jnp.sum(jnp.where(oh8 > 0.0, off8 + excl, 0.0), axis=0, keepdims=True)
    base = carry[:, 0:1]
    new_base = base + len8
    carry[...] = jnp.broadcast_to(new_base, carry.shape)

    t8 = jnp.where(sub == STAGE_LANE, pos, 0.0)
    for j in range(EXPERTS_PER_GROUP):
        t8 = jnp.where(sub == j, gates[j], t8)
    tail = jnp.concatenate([t8, jnp.zeros((TAIL_LANES - SUBLANES, ts), F32)], axis=0).T
    tail_ref[...] = tail

    lane8 = lax.broadcasted_iota(I32, (SUBLANES, LANES), 1)
    slot8 = sub1.astype(F32) * float(cap) + base
    mv = jnp.where(lane8 == M_SLOT, slot8, 0.0)
    mv = jnp.where(lane8 == M_LEN, len8, mv)
    mv = jnp.where(lane8 == M_OFF, off8, mv)
    mv = jnp.where(lane8 == M_TOTAL, total, mv)
    mv = jnp.where(lane8 == M_END, new_base, mv).astype(I32)
    meta_ref[0] = mv

    srow = lax.broadcasted_iota(I32, (STAGE_ROWS, ts), 0)
    sort = jnp.where(srow == pos.astype(I32), 1.0, 0.0).astype(BF16)
    xs_sorted = _dot(sort, xh).astype(BF16)
    pieces = _dot(sort, jnp.concatenate(_split3(tail), axis=1))
    tail_sorted = (pieces[:, :LANES] + pieces[:, LANES:2 * LANES]) + pieces[:, 2 * LANES:]

    def wait_runs(n_rows):
        pltpu.make_async_copy(stx.at[pl.ds(0, n_rows)], xs_hbm.at[pl.ds(0, n_rows)], sem.at[0]).wait()
        pltpu.make_async_copy(stt.at[pl.ds(0, n_rows)], tls_hbm.at[pl.ds(0, n_rows)], sem.at[1]).wait()

    @pl.when(step > 0)
    def _():
        n_prev = _aligned(prev_total[0])

        @pl.when(n_prev > 0)
        def _():
            wait_runs(n_prev)

    stx[...] = xs_sorted
    stt[...] = tail_sorted
    mvec[...] = mv
    to_smem = pltpu.make_async_copy(mvec, msm, ssem)
    to_smem.start()
    to_smem.wait()

    for g in range(N_GROUPS):
        _run_copy(stx, msm[g, M_OFF], xs_hbm, msm[g, M_SLOT], msm[g, M_LEN], sem.at[0])
        _run_copy(stt, msm[g, M_OFF], tls_hbm, msm[g, M_SLOT], msm[g, M_LEN], sem.at[1])
    prev_total[0] = msm[0, M_TOTAL]

    @pl.when(last)
    def _():
        n_own = _aligned(msm[0, M_TOTAL])

        @pl.when(n_own > 0)
        def _():
            wait_runs(n_own)

        zx[...] = jnp.zeros(zx.shape, BF16)
        zt[...] = jnp.zeros(zt.shape, F32)
        for g in range(N_GROUPS):
            end = msm[g, M_END]
            n_pad = _aligned((blk - end % blk) % blk)
            _run_copy(zx, 0, xs_hbm, g * cap + end, n_pad, zsem.at[0])
            _run_copy(zt, 0, tls_hbm, g * cap + end, n_pad, zsem.at[1])

            @pl.when(n_pad > 0)
            def _(n_pad=n_pad):
                pltpu.make_async_copy(zx.at[pl.ds(0, n_pad)], xs_hbm.at[pl.ds(0, n_pad)], zsem.at[0]).wait()
                pltpu.make_async_copy(zt.at[pl.ds(0, n_pad)], tls_hbm.at[pl.ds(0, n_pad)], zsem.at[1]).wait()


def _xattn_router(x, mem, wq, wkv, wo, lng, lnb, rw2, rbias, cap, blk):
    bsz, seq, d = x.shape
    mlen = mem.shape[1]
    ts = min(SEQ_TILE, seq)
    assert ts == SEQ_TILE and blk % RUN_ALIGN == 0
    ns = seq // ts
    const = lambda a: pl.BlockSpec(a.shape, lambda b, s: (0,) * a.ndim)
    zero_rows = blk
    return pl.pallas_call(
        functools.partial(_xattn_kernel, ts=ts, cap=cap, blk=blk),
        out_shape=(
            jax.ShapeDtypeStruct((bsz, seq, d), F32),
            jax.ShapeDtypeStruct((bsz * seq, TAIL_LANES), F32),
            jax.ShapeDtypeStruct((bsz * ns, SUBLANES, LANES), I32),
            jax.ShapeDtypeStruct((N_GROUPS * cap, d), BF16),
            jax.ShapeDtypeStruct((N_GROUPS * cap, TAIL_LANES), F32),
        ),
        grid=(bsz, ns),
        in_specs=[
            pl.BlockSpec((1, ts, d), lambda b, s: (b, s, 0)),
            pl.BlockSpec((1, mlen, d), lambda b, s: (b, 0, 0)),
            const(wq), const(wkv), const(wo), const(lng), const(lnb),
            const(rw2), const(rbias),
        ],
        out_specs=(
            pl.BlockSpec((1, ts, d), lambda b, s: (b, s, 0)),
            pl.BlockSpec((ts, TAIL_LANES), lambda b, s: (b * ns + s, 0)),
            pl.BlockSpec((1, SUBLANES, LANES), lambda b, s: (b * ns + s, 0, 0)),
            pl.BlockSpec(memory_space=pl.ANY),
            pl.BlockSpec(memory_space=pl.ANY),
        ),
        scratch_shapes=[
            pltpu.VMEM((mlen, d), BF16),
            pltpu.VMEM((mlen, d), BF16),
            pltpu.VMEM((SUBLANES, LANES), F32),
            pltpu.VMEM((STAGE_ROWS, d), BF16),
            pltpu.VMEM((STAGE_ROWS, TAIL_LANES), F32),
            pltpu.VMEM((SUBLANES, LANES), I32),
            pltpu.SMEM((SUBLANES, LANES), I32),
            pltpu.SMEM((1,), I32),
            pltpu.VMEM((zero_rows, d), BF16),
            pltpu.VMEM((zero_rows, TAIL_LANES), F32),
            pltpu.SemaphoreType.DMA((2,)),
            pltpu.SemaphoreType.DMA,
            pltpu.SemaphoreType.DMA((2,)),
        ],
        compiler_params=pltpu.CompilerParams(
            dimension_semantics=("arbitrary", "arbitrary"), vmem_limit_bytes=VMEM_LIMIT,
            has_side_effects=True),
        name="xattn_router",
    )(x, mem, wq, wkv, wo, lng, lnb, rw2, rbias)


def _ffn_kernel(blk_in_ref, grp_ref, used_ref, xs_ref, tl_ref, wg_ref, wu_ref, wd_ref, o_ref):
    de = wg_ref.shape[2] // EXPERTS_PER_GROUP

    @pl.when(used_ref[pl.program_id(0)] == 1)
    def _():
        xb = xs_ref[...]
        hg = _dot(xb, wg_ref[0])
        hu = _dot(xb, wu_ref[0])
        hid = hg * _sigmoid(hg) * hu
        parts = []
        for j in range(EXPERTS_PER_GROUP):
            gate = tl_ref[:, j:j + 1]
            hj = hid[:, j * de:(j + 1) * de]
            parts.append(jnp.where(gate != 0.0, hj * gate, 0.0))
        hid = jnp.concatenate(parts, axis=1).astype(BF16)
        o_ref[...] = _dot(hid, wd_ref[0]).astype(BF16)


def _ffn(blk_in, blk_grp, used, xs, tls, wg, wu, wd):
    d = xs.shape[1]
    blk = FFN_BLOCK
    grid_spec = pltpu.PrefetchScalarGridSpec(
        num_scalar_prefetch=3,
        grid=(blk_in.shape[0],),
        in_specs=[
            pl.BlockSpec((blk, d), lambda i, bi, grp, us: (bi[i], 0)),
            pl.BlockSpec((blk, TAIL_LANES), lambda i, bi, grp, us: (bi[i], 0)),
            pl.BlockSpec((1,) + wg.shape[1:], lambda i, bi, grp, us: (grp[i], 0, 0)),
            pl.BlockSpec((1,) + wu.shape[1:], lambda i, bi, grp, us: (grp[i], 0, 0)),
            pl.BlockSpec((1,) + wd.shape[1:], lambda i, bi, grp, us: (grp[i], 0, 0)),
        ],
        out_specs=pl.BlockSpec((blk, d), lambda i, bi, grp, us: (bi[i], 0)),
    )
    return pl.pallas_call(
        _ffn_kernel,
        out_shape=jax.ShapeDtypeStruct(xs.shape, BF16),
        grid_spec=grid_spec,
        compiler_params=pltpu.CompilerParams(
            dimension_semantics=("arbitrary",), vmem_limit_bytes=VMEM_LIMIT),
        name="group_ffn",
    )(blk_in, blk_grp, used, xs, tls, wg, wu, wd)


def _group_weights(w):
    e, a, b = w.shape
    return w.reshape(N_GROUPS, EXPERTS_PER_GROUP, a, b).transpose(0, 2, 1, 3).reshape(
        N_GROUPS, a, EXPERTS_PER_GROUP * b).astype(BF16)


def _block_tables(seg_rows, cap, blk, n_steps):
    nblk = (seg_rows + blk - 1) // blk
    bend = jnp.cumsum(nblk)
    bstart = bend - nblk
    step = jnp.arange(n_steps, dtype=I32)
    used = step < bend[-1]
    grp = jnp.minimum(jnp.searchsorted(bend, step, side="right"), N_GROUPS - 1).astype(I32)
    blk_in = grp * (cap // blk) + step - bstart[grp]
    last_real = jnp.maximum(bend[-1] - 1, 0)
    blk_in = jnp.where(used, blk_in, blk_in[last_real])
    grp = jnp.where(used, grp, grp[last_real])
    return blk_in.astype(I32), grp.astype(I32), used.astype(I32)


def _flat_meta(meta):
    m = meta[:, :N_GROUPS, :]
    rec = jnp.concatenate([m[:, :, M_SLOT], m[:, :, M_LEN], m[:, :, M_OFF], m[:, :1, M_TOTAL],
                           jnp.zeros((m.shape[0], META_W - 3 * N_GROUPS - 1), I32)], axis=1)
    return rec.reshape(-1)


def kernel(x, mem, w_in, b_i, b_f, conv_qk, head_norm_g, pool_w, pool_scale, w_mix_out,
           ln_mix_g, ln_mix_b, w_xq, w_xkv, w_xo, ln_x_g, ln_x_b, router_w, router_bias,
           w_gate, w_up, w_down, ln_moe_g, ln_moe_b):
    bsz, seq, d = x.shape
    n_tok = bsz * seq
    n_tiles = n_tok // SEQ_TILE
    mw = N_HEADS * HEAD_DIM
    n_gate = 2 * N_HEADS
    blk = FFN_BLOCK
    cap = -(-(n_tok + RUN_ALIGN * n_tiles) // blk) * blk
    n_steps = (n_tok + N_GROUPS * RUN_ALIGN * n_tiles) // blk + N_GROUPS

    rw = jnp.pad(router_w, ((0, 0), (0, LANES - N_EXPERTS)))
    rwh, rwm, _ = _split3(rw)
    rw2 = jnp.concatenate([rwh, rwm], axis=1)
    rbias = router_bias.reshape(N_EXPERTS, 1).astype(F32)
    row = lambda v: v.reshape(1, -1).astype(F32)

    combine = None
    for l in range(DEPTH):
        wa = w_in[l][:, :4 * mw].astype(BF16)
        wu = w_in[l][:, 4 * mw + n_gate:].astype(BF16)
        wif = jnp.pad(w_in[l][:, 4 * mw:4 * mw + n_gate], ((0, 0), (0, LANES - n_gate))).astype(BF16)
        bif = jnp.pad(jnp.concatenate([b_i[l], b_f[l]]), (0, LANES - n_gate)).reshape(1, LANES)
        weights = (wa, wu, wif, bif, conv_qk[l], row(head_norm_g[l]), pool_w[l].astype(BF16),
                   row(pool_scale[l]), w_mix_out[l].astype(BF16), row(ln_mix_g[l]), row(ln_mix_b[l]))
        x = _mixer(x, weights, combine)

        x2, tail, meta, xs, tls = _xattn_router(
            x, mem, w_xq[l].astype(BF16), w_xkv[l].astype(BF16), w_xo[l].astype(BF16),
            row(ln_x_g[l]), row(ln_x_b[l]), rw2, rbias, cap, blk)

        seg_rows = meta[-1, :N_GROUPS, M_END]
        blk_in, blk_grp, used = _block_tables(seg_rows, cap, blk, n_steps)
        ys = _ffn(blk_in, blk_grp, used, xs, tls, _group_weights(w_gate[l]), _group_weights(w_up[l]),
                  w_down[l].reshape(N_GROUPS, EXPERTS_PER_GROUP * w_down.shape[2], d).astype(BF16))
        x = x2
        combine = (_flat_meta(meta), tail, ys, row(ln_moe_g[l]), row(ln_moe_b[l]))

    meta, tail, ys, cg, cb = combine
    return _final_combine(meta, x.reshape(n_tok, d), tail, ys, cg, cb).reshape(bsz, seq, d)
```

```python
import functools

import jax
import jax.numpy as jnp
from jax import lax
from jax.experimental import pallas as pl
from jax.experimental.pallas import tpu as pltpu

F32 = jnp.float32
BF16 = jnp.bfloat16
I32 = jnp.int32

N_HEADS = 4
HEAD_DIM = 128
POOL_WINDOWS = (2, 4, 8, 16)
POOL_GROUP = 128
CONV_WIDTH = 4
XATTN_HEADS = 4
N_EXPERTS = 16
N_GROUPS = 4
EXPERTS_PER_GROUP = 4
DEPTH = 2
ALPHA = (2 * DEPTH) ** 0.25
LN_EPS = 1e-5

LANES = 128
SUBLANES = 8
BF16_TILE_ROWS = 16
VMEM_LIMIT = 56 * 1024 * 1024

SEQ_TILE = 256
MIXER_SEQS = 2
HEADS_TOGETHER = 2
MLSTM_CHUNK = 256
FFN_BLOCK = 256
CONV_CARRY = 8
POOL_CARRY = 16

RUN_ALIGN = BF16_TILE_ROWS
STAGE_ROWS = SEQ_TILE + N_GROUPS * RUN_ALIGN
STAGE_ROWS_PADDED = 384
TAIL_LANES = 128
STAGE_LANE = 5
M_SLOT, M_LEN, M_OFF, M_TOTAL, M_END = 0, 1, 2, 3, 4
META_W = 16


def _layer_norm(y, g, b):
    mu = jnp.mean(y, axis=-1, keepdims=True)
    d = y - mu
    var = jnp.mean(d * d, axis=-1, keepdims=True)
    return d * lax.rsqrt(var + LN_EPS) * g + b


def _sigmoid(v):
    return 1.0 / (1.0 + jnp.exp(-v))


def _dot(a, b):
    return jnp.dot(a, b, preferred_element_type=F32)


def _dot_nt(a, b):
    return lax.dot_general(a, b, (((1,), (1,)), ((), ())), preferred_element_type=F32)


def _split3(v):
    hi = v.astype(BF16)
    r1 = v - hi.astype(F32)
    mid = r1.astype(BF16)
    lo = (r1 - mid.astype(F32)).astype(BF16)
    return hi, mid, lo


def _loop(n, body, unroll=1):
    lax.fori_loop(0, n, lambda j, c: (body(j), c)[1], 0, unroll=unroll)


def _aligned(v):
    return pl.multiple_of(v, RUN_ALIGN)


def _run_copy(src, src_row, dst, dst_row, length, sem):
    length = _aligned(length)

    @pl.when(length > 0)
    def _():
        pltpu.make_async_copy(src.at[pl.ds(_aligned(src_row), length)],
                              dst.at[pl.ds(_aligned(dst_row), length)], sem).start()


def _combine_fetch(meta_ref, tile, next_tile, first, has_next, slot, ys_hbm, stage, sem):
    def fetch(t, sl):
        for g in range(N_GROUPS):
            _run_copy(ys_hbm, meta_ref[t * META_W + g],
                      stage.at[sl], meta_ref[t * META_W + 2 * N_GROUPS + g],
                      meta_ref[t * META_W + N_GROUPS + g], sem.at[sl])

    @pl.when(first)
    def _():
        stage[...] = jnp.zeros(stage.shape, stage.dtype)
        fetch(tile, slot)

    @pl.when(has_next)
    def _():
        fetch(next_tile, 1 - slot)

    total = _aligned(meta_ref[tile * META_W + 3 * N_GROUPS])

    @pl.when(total > 0)
    def _():
        pltpu.make_async_copy(ys_hbm.at[pl.ds(0, total)], stage.at[slot, pl.ds(0, total)],
                              sem.at[slot]).wait()


def _combine(x2, tail, sorted_rows, lng, lnb):
    ts = x2.shape[0]
    pos = tail[:, STAGE_LANE:STAGE_LANE + 1].astype(I32)
    lane = lax.broadcasted_iota(I32, (ts, STAGE_ROWS_PADDED), 1)
    unsort = jnp.where(lane == pos, 1.0, 0.0).astype(BF16)
    return _layer_norm(ALPHA * x2 + _dot(unsort, sorted_rows), lng, lnb)


def _combine_scratch(d, lanes):
    return [pltpu.VMEM((lanes, 2, STAGE_ROWS_PADDED, d), BF16), pltpu.SemaphoreType.DMA((lanes, 2))]


def _final_kernel(meta_ref, x2_ref, tail_ref, ys_hbm, lng_ref, lnb_ref, o_ref, stage, sem):
    i = pl.program_id(0)
    _combine_fetch(meta_ref, i, i + 1, i == 0, i + 1 < pl.num_programs(0), i % 2, ys_hbm, stage.at[0], sem.at[0])
    o_ref[...] = _combine(x2_ref[...], tail_ref[...], stage[0, i % 2], lng_ref[...], lnb_ref[...])


def _final_combine(meta, x2, tail, ys, lng, lnb):
    n_tok, d = x2.shape
    ts = tail.shape[0] // (meta.shape[0] // META_W)
    grid_spec = pltpu.PrefetchScalarGridSpec(
        num_scalar_prefetch=1,
        grid=(n_tok // ts,),
        in_specs=[
            pl.BlockSpec((ts, d), lambda i, m: (i, 0)),
            pl.BlockSpec((ts, TAIL_LANES), lambda i, m: (i, 0)),
            pl.BlockSpec(memory_space=pl.ANY),
            pl.BlockSpec(lng.shape, lambda i, m: (0, 0)),
            pl.BlockSpec(lnb.shape, lambda i, m: (0, 0)),
        ],
        out_specs=pl.BlockSpec((ts, d), lambda i, m: (i, 0)),
        scratch_shapes=_combine_scratch(d, 1),
    )
    return pl.pallas_call(
        _final_kernel,
        out_shape=jax.ShapeDtypeStruct((n_tok, d), F32),
        grid_spec=grid_spec,
        compiler_params=pltpu.CompilerParams(
            dimension_semantics=("arbitrary",), vmem_limit_bytes=VMEM_LIMIT),
        name="final_combine",
    )(meta, x2, tail, ys, lng, lnb)


def _mixer_kernel(*refs, ts, lc, combine):
    n_lead = 6 if combine else 1
    zq_ext, u_carry, c_st, m_st = refs[n_lead + 12:n_lead + 16]
    b = pl.program_id(0)
    s = pl.program_id(1)
    ns = pl.num_programs(1)

    @pl.when(s == 0)
    def _():
        zq_ext[:, 0:CONV_CARRY, :] = jnp.zeros((MIXER_SEQS, CONV_CARRY, zq_ext.shape[2]), F32)
        u_carry[...] = jnp.zeros(u_carry.shape, F32)
        c_st[...] = jnp.zeros(c_st.shape, F32)
        m_st[...] = jnp.zeros(m_st.shape, F32)

    step = b * ns + s
    if combine:
        meta_ref, ys_hbm = refs[0], refs[3]
        stage, sem = refs[n_lead + 16:n_lead + 18]
        for lane in range(MIXER_SEQS):
            tile = (b * MIXER_SEQS + lane) * ns + s
            next_tile = jnp.where(s + 1 < ns, tile + 1, tile + (MIXER_SEQS - 1) * ns + 1)
            _combine_fetch(meta_ref, tile, next_tile, step == 0, step + 1 < pl.num_programs(0) * ns,
                           step % 2, ys_hbm, stage.at[lane], sem.at[lane])

    lanes = [_mixer_lane(lane, step % 2, refs, ts, lc, combine) for lane in range(MIXER_SEQS)]
    for _ in zip(*lanes):
        pass


def _mixer_lane(lane, slot, refs, ts, lc, combine):
    if combine:
        (_, x_ref, tail_ref, _, cg_ref, cb_ref), refs = refs[:6], refs[6:]
    else:
        x_ref, refs = refs[0], refs[1:]
    (wa_ref, wu_ref, wif_ref, bif_ref, conv_ref, hng_ref, poolw_ref, pscale_ref, wout_ref,
     lng_ref, lnb_ref, o_ref, zq_ext, u_carry, c_st, m_st) = refs[:16]
    zq_ext, u_carry, c_st, m_st = zq_ext.at[lane], u_carry.at[lane], c_st.at[lane], m_st.at[lane]
    s = pl.program_id(1)
    mw = N_HEADS * HEAD_DIM

    x = x_ref[lane]
    if combine:
        stage = refs[16]
        x = _combine(x, tail_ref[lane], stage[lane, slot], cg_ref[...], cb_ref[...])
    xb = x.astype(BF16)
    z = _dot(xb, wa_ref[...])
    u = _dot(xb, wu_ref[...])
    gts = _dot(xb, wif_ref[...]) + bif_ref[...]
    yield

    zq_ext[CONV_CARRY:CONV_CARRY + ts, :] = z[:, :2 * mw]
    cw = conv_ref[...]
    acc = zq_ext[CONV_CARRY:CONV_CARRY + ts, :] * cw[CONV_WIDTH - 1:CONV_WIDTH, :]
    for j in range(1, CONV_WIDTH):
        acc = acc + zq_ext[CONV_CARRY - j:CONV_CARRY - j + ts, :] * cw[CONV_WIDTH - 1 - j:CONV_WIDTH - j, :]
    zq_ext[0:CONV_CARRY, :] = zq_ext[ts:ts + CONV_CARRY, :]
    qk = acc * _sigmoid(acc)
    q_all = qk[:, :mw] * (HEAD_DIM ** -0.5)
    k_all = qk[:, mw:]
    v_all = z[:, 2 * mw:3 * mw]
    o_all = z[:, 3 * mw:4 * mw]

    lf_all = jnp.minimum(gts, 0.0) - jnp.log1p(jnp.exp(-jnp.abs(gts)))
    yield

    row_i = lax.broadcasted_iota(I32, (lc, lc), 0)
    col_i = lax.broadcasted_iota(I32, (lc, lc), 1)
    causal = col_i <= row_i
    tri = jnp.where(causal, 1.0, 0.0).astype(BF16)
    ones_col = jnp.where(lax.broadcasted_iota(I32, (lc, HEAD_DIM), 1) == 0, 1.0, 0.0).astype(BF16)

    head_out = [[] for _ in range(N_HEADS)]
    for c in range(ts // lc):
        rows = slice(c * lc, (c + 1) * lc)
        hi, mid, lo = _split3(lf_all[rows, :])
        b_all = _dot(tri, hi) + _dot(tri, mid) + _dot(tri, lo)
        g_c = gts[rows, :]
        r_all = g_c - pltpu.roll(b_all, LANES - N_HEADS, 1)
        r_t = r_all.T
        def head(h, rows=rows, b_all=b_all, g_c=g_c, r_t=r_t):
            hs = slice(h * HEAD_DIM, (h + 1) * HEAD_DIM)
            qh = q_all[rows, hs]
            kh = k_all[rows, hs]
            vh = jnp.concatenate([v_all[rows, hs].astype(BF16), ones_col], axis=1)
            qhb = qh.astype(BF16)
            bc = b_all[:, N_HEADS + h:N_HEADS + h + 1]
            igc = g_c[:, h:h + 1]
            r_row = r_t[h:h + 1, :]
            c_prev = c_st[h]
            m_prev = m_st[h][:, 0:1]
            qk = _dot_nt(qhb, kh.astype(BF16))
            qc = _dot(qhb, c_prev.astype(BF16))
            yield

            log_d = jnp.where(causal, bc + r_row, -jnp.inf)
            m_intra = jnp.max(log_d, axis=1, keepdims=True)
            log_inter = bc + m_prev
            m_t = jnp.maximum(m_intra, log_inter)
            p = jnp.exp(log_d - m_t) * qk
            inter = jnp.exp(log_inter - m_t)
            b_last = bc[lc - 1:lc, :]
            w_state = b_last - bc + igc
            m_loc = jnp.max(w_state, axis=0, keepdims=True)
            ka = kh * jnp.exp(w_state - m_loc)
            yield

            nd = _dot(p.astype(BF16), vh) + inter * qc
            c_loc = _dot(ka.T.astype(BF16), vh)
            yield

            den = nd[:, HEAD_DIM:HEAD_DIM + 1]
            hh = nd[:, :HEAD_DIM] * (1.0 / jnp.maximum(jnp.abs(den), jnp.exp(-m_t)))
            m_new = jnp.maximum(b_last + m_prev, m_loc)
            s_old = jnp.exp(b_last + m_prev - m_new)
            s_new = jnp.exp(m_loc - m_new)
            c_st[h] = s_old * c_prev + s_new * c_loc
            m_st[h] = jnp.broadcast_to(m_new, (1, LANES))

            mu = jnp.mean(hh, axis=1, keepdims=True)
            dlt = hh - mu
            var = jnp.mean(dlt * dlt, axis=1, keepdims=True)
            hn = dlt * lax.rsqrt(var + LN_EPS) * hng_ref[:, hs]
            head_out[h].append(hn * _sigmoid(o_all[rows, hs]))
            yield

        for h0 in range(0, N_HEADS, HEADS_TOGETHER):
            for _ in zip(*[head(h) for h in range(h0, h0 + HEADS_TOGETHER)]):
                yield

    mixed = [jnp.concatenate(ho, axis=0) if len(ho) > 1 else ho[0] for ho in head_out]

    ue = jnp.concatenate([u_carry[...], u], axis=0)
    u_carry[...] = u[ts - POOL_CARRY:ts, :]
    pos = (lax.broadcasted_iota(I32, (ts, 1), 0) + s * ts + 1).astype(F32)
    for g, w in enumerate(POOL_WINDOWS):
        cs = slice(g * POOL_GROUP, (g + 1) * POOL_GROUP)
        win = ue[:, cs]
        shift = 1
        while shift < w:
            win = win + pltpu.roll(win, shift, 0)
            shift *= 2
        ug = u[:, cs]
        pooled = win[POOL_CARRY:, :] / jnp.minimum(pos, float(w)) - ug
        pm = _dot(pooled.astype(BF16), poolw_ref[g]) * pscale_ref[:, cs]
        mixed.append(pm)
    yield

    mixed = jnp.concatenate(mixed, axis=1).astype(BF16)
    y = _dot(mixed, wout_ref[...])
    o_ref[lane] = _layer_norm(ALPHA * x + y, lng_ref[...], lnb_ref[...])
    yield


def _mixer(x, weights, combine=None):
    bsz, seq, d = x.shape
    ts = min(SEQ_TILE, seq)
    lc = min(MLSTM_CHUNK, ts)
    ns = seq // ts
    mw = N_HEADS * HEAD_DIM
    pw = weights[1].shape[1]
    const = lambda a: pl.BlockSpec(a.shape, lambda b, s, *_: (0,) * a.ndim)
    in_specs = [pl.BlockSpec((MIXER_SEQS, ts, d), lambda b, s, *_: (b, s, 0))]
    args = [x]
    scratch = [
        pltpu.VMEM((MIXER_SEQS, ts + CONV_CARRY, 2 * mw), F32),
        pltpu.VMEM((MIXER_SEQS, POOL_CARRY, pw), F32),
        pltpu.VMEM((MIXER_SEQS, N_HEADS, HEAD_DIM, 2 * HEAD_DIM), F32),
        pltpu.VMEM((MIXER_SEQS, N_HEADS, 1, LANES), F32),
    ]
    prefetch = []
    if combine is not None:
        meta, tail, ys, cg, cb = combine
        prefetch = [meta]
        in_specs += [pl.BlockSpec((MIXER_SEQS, ts, TAIL_LANES), lambda b, s, *_: (b, s, 0)),
                     pl.BlockSpec(memory_space=pl.ANY), const(cg), const(cb)]
        args += [tail.reshape(bsz, seq, TAIL_LANES), ys, cg, cb]
        scratch += _combine_scratch(d, MIXER_SEQS)
    in_specs += [const(w) for w in weights]
    args += list(weights)
    grid_spec = pltpu.PrefetchScalarGridSpec(
        num_scalar_prefetch=len(prefetch),
        grid=(bsz // MIXER_SEQS, ns),
        in_specs=in_specs,
        out_specs=pl.BlockSpec((MIXER_SEQS, ts, d), lambda b, s, *_: (b, s, 0)),
        scratch_shapes=scratch,
    )
    return pl.pallas_call(
        functools.partial(_mixer_kernel, ts=ts, lc=lc, combine=combine is not None),
        out_shape=jax.ShapeDtypeStruct((bsz, seq, d), F32),
        grid_spec=grid_spec,
        compiler_params=pltpu.CompilerParams(
            dimension_semantics=("arbitrary", "arbitrary"), vmem_limit_bytes=VMEM_LIMIT),
        name="mixer",
    )(*prefetch, *args)


def _top2_sum(a, b, c, d):
    hi1, lo1 = jnp.maximum(a, b), jnp.minimum(a, b)
    hi2, lo2 = jnp.maximum(c, d), jnp.minimum(c, d)
    return jnp.maximum(hi1, hi2) + jnp.maximum(jnp.minimum(hi1, hi2), jnp.maximum(lo1, lo2))


def _xattn_kernel(x_ref, mem_ref, wq_ref, wkv_ref, wo_ref, lng_ref, lnb_ref,
                  rw2_ref, rbias_ref,
                  x2_ref, tail_ref, meta_ref, xs_hbm, tls_hbm,
                  k_scr, v_scr, carry, stx, stt, mvec, msm, prev_total, zx, zt, sem, ssem, zsem,
                  *, ts, cap, blk):
    b = pl.program_id(0)
    s = pl.program_id(1)
    step = b * pl.num_programs(1) + s
    last = step == pl.num_programs(0) * pl.num_programs(1) - 1
    d = x_ref.shape[2]
    dh = d // XATTN_HEADS

    @pl.when(s == 0)
    def _():
        kv = _dot(mem_ref[0].astype(BF16), wkv_ref[...])
        k_scr[...] = kv[:, :d].astype(BF16)
        v_scr[...] = kv[:, d:].astype(BF16)

    @pl.when(step == 0)
    def _():
        carry[...] = jnp.zeros(carry.shape, F32)

    x = x_ref[0]
    q = (_dot(x.astype(BF16), wq_ref[...]) * (dh ** -0.5)).astype(BF16)
    outs = []
    for h in range(XATTN_HEADS):
        hs = slice(h * dh, (h + 1) * dh)
        sc = _dot_nt(q[:, hs], k_scr[:, hs])
        e = jnp.exp(sc - jnp.max(sc, axis=1, keepdims=True))
        l = jnp.sum(e, axis=1, keepdims=True)
        outs.append(_dot(e.astype(BF16), v_scr[:, hs]) * (1.0 / l))
    o = jnp.concatenate(outs, axis=1).astype(BF16)
    x2 = _layer_norm(ALPHA * x + _dot(o, wo_ref[...]), lng_ref[...], lnb_ref[...])
    x2_ref[0] = x2

    xh, xm, _ = _split3(x2)
    both = _dot(xh, rw2_ref[...])
    logits = (both[:, :LANES] + both[:, LANES:]) + _dot(xm, rw2_ref[:, :LANES])
    lt = logits.T[0:N_EXPERTS, :]
    score = _sigmoid(lt)
    sel = score + rbias_ref[...]

    sel_r = [sel[e:e + 1, :] for e in range(N_EXPERTS)]
    score_r = [score[e:e + 1, :] for e in range(N_EXPERTS)]
    gs = [_top2_sum(*sel_r[EXPERTS_PER_GROUP * g:EXPERTS_PER_GROUP * (g + 1)]) for g in range(N_GROUPS)]
    best = jnp.zeros((1, ts), I32)
    bestv = gs[0]
    for g in range(1, N_GROUPS):
        better = gs[g] > bestv
        best = jnp.where(better, g, best)
        bestv = jnp.where(better, gs[g], bestv)
    in_g = [best == g for g in range(N_GROUPS)]

    def pick(rows, j):
        out = rows[j]
        for g in range(1, N_GROUPS):
            out = jnp.where(in_g[g], rows[EXPERTS_PER_GROUP * g + j], out)
        return out

    vsel = [pick(sel_r, j) for j in range(EXPERTS_PER_GROUP)]
    vsc = [pick(score_r, j) for j in range(EXPERTS_PER_GROUP)]
    gates = []
    for j in range(EXPERTS_PER_GROUP):
        beaten = jnp.zeros((1, ts), I32)
        for k in range(EXPERTS_PER_GROUP):
            if k == j:
                continue
            wins = (vsel[k] > vsel[j]) | ((vsel[k] == vsel[j]) & (k < j))
            beaten = beaten + wins.astype(I32)
        gates.append(jnp.where(beaten < 2, vsc[j], 0.0))
    gsum = gates[0] + gates[1] + gates[2] + gates[3]
    gates = [g / gsum for g in gates]

    sub = lax.broadcasted_iota(I32, (SUBLANES, ts), 0)
    oh8 = jnp.zeros((SUBLANES, ts), F32)
    for g in range(N_GROUPS):
        oh8 = jnp.where((sub == g) & in_g[g], 1.0, oh8)
    r_i = lax.broadcasted_iota(I32, (ts, ts), 0)
    c_i = lax.broadcasted_iota(I32, (ts, ts), 1)
    upper = jnp.where(r_i < c_i, 1.0, 0.0).astype(BF16)
    excl = _dot(oh8.astype(BF16), upper)
    n8 = jnp.sum(oh8, axis=1, keepdims=True)
    len8 = jnp.floor((n8 + (RUN_ALIGN - 1)) * (1.0 / RUN_ALIGN)) * RUN_ALIGN
    sub1 = lax.broadcasted_iota(I32, (SUBLANES, 1), 0)
    off8 = jnp.zeros((SUBLANES, 1), F32)
    run_off = jnp.zeros((1, 1), F32)
    for g in range(N_GROUPS):
        off8 = jnp.where(sub1 == g, run_off, off8)
        run_off = run_off + len8[g:g + 1, :]
    total = run_off
    pos = jnp.sum(jnp.where(oh8 > 0.0, off8 + excl, 0.0), axis=0, keepdims=True)
    base = carry[:, 0:1]
    new_base = base + len8
    carry[...] = jnp.broadcast_to(new_base, carry.shape)

    t8 = jnp.where(sub == STAGE_LANE, pos, 0.0)
    for j in range(EXPERTS_PER_GROUP):
        t8 = jnp.where(sub == j, gates[j], t8)
    tail = jnp.concatenate([t8, jnp.zeros((TAIL_LANES - SUBLANES, ts), F32)], axis=0).T
    tail_ref[...] = tail

    lane8 = lax.broadcasted_iota(I32, (SUBLANES, LANES), 1)
    slot8 = sub1.astype(F32) * float(cap) + base
    mv = jnp.where(lane8 == M_SLOT, slot8, 0.0)
    mv = jnp.where(lane8 == M_LEN, len8, mv)
    mv = jnp.where(lane8 == M_OFF, off8, mv)
    mv = jnp.where(lane8 == M_TOTAL, total, mv)
    mv = jnp.where(lane8 == M_END, new_base, mv).astype(I32)
    meta_ref[0] = mv

    srow = lax.broadcasted_iota(I32, (STAGE_ROWS, ts), 0)
    sort = jnp.where(srow == pos.astype(I32), 1.0, 0.0).astype(BF16)
    xs_sorted = _dot(sort, xh).astype(BF16)
    pieces = _dot(sort, jnp.concatenate(_split3(tail), axis=1))
    tail_sorted = (pieces[:, :LANES] + pieces[:, LANES:2 * LANES]) + pieces[:, 2 * LANES:]

    def wait_runs(n_rows):
        pltpu.make_async_copy(stx.at[pl.ds(0, n_rows)], xs_hbm.at[pl.ds(0, n_rows)], sem.at[0]).wait()
        pltpu.make_async_copy(stt.at[pl.ds(0, n_rows)], tls_hbm.at[pl.ds(0, n_rows)], sem.at[1]).wait()

    @pl.when(step > 0)
    def _():
        n_prev = _aligned(prev_total[0])

        @pl.when(n_prev > 0)
        def _():
            wait_runs(n_prev)

    stx[...] = xs_sorted
    stt[...] = tail_sorted
    mvec[...] = mv
    to_smem = pltpu.make_async_copy(mvec, msm, ssem)
    to_smem.start()
    to_smem.wait()

    for g in range(N_GROUPS):
        _run_copy(stx, msm[g, M_OFF], xs_hbm, msm[g, M_SLOT], msm[g, M_LEN], sem.at[0])
        _run_copy(stt, msm[g, M_OFF], tls_hbm, msm[g, M_SLOT], msm[g, M_LEN], sem.at[1])
    prev_total[0] = msm[0, M_TOTAL]

    @pl.when(last)
    def _():
        n_own = _aligned(msm[0, M_TOTAL])

        @pl.when(n_own > 0)
        def _():
            wait_runs(n_own)

        zx[...] = jnp.zeros(zx.shape, BF16)
        zt[...] = jnp.zeros(zt.shape, F32)
        for g in range(N_GROUPS):
            end = msm[g, M_END]
            n_pad = _aligned((blk - end % blk) % blk)
            _run_copy(zx, 0, xs_hbm, g * cap + end, n_pad, zsem.at[0])
            _run_copy(zt, 0, tls_hbm, g * cap + end, n_pad, zsem.at[1])

            @pl.when(n_pad > 0)
            def _(n_pad=n_pad):
                pltpu.make_async_copy(zx.at[pl.ds(0, n_pad)], xs_hbm.at[pl.ds(0, n_pad)], zsem.at[0]).wait()
                pltpu.make_async_copy(zt.at[pl.ds(0, n_pad)], tls_hbm.at[pl.ds(0, n_pad)], zsem.at[1]).wait()


def _xattn_router(x, mem, wq, wkv, wo, lng, lnb, rw2, rbias, cap, blk):
    bsz, seq, d = x.shape
    mlen = mem.shape[1]
    ts = min(SEQ_TILE, seq)
    assert ts == SEQ_TILE and blk % RUN_ALIGN == 0
    ns = seq // ts
    const = lambda a: pl.BlockSpec(a.shape, lambda b, s: (0,) * a.ndim)
    zero_rows = blk
    return pl.pallas_call(
        functools.partial(_xattn_kernel, ts=ts, cap=cap, blk=blk),
        out_shape=(
            jax.ShapeDtypeStruct((bsz, seq, d), F32),
            jax.ShapeDtypeStruct((bsz * seq, TAIL_LANES), F32),
            jax.ShapeDtypeStruct((bsz * ns, SUBLANES, LANES), I32),
            jax.ShapeDtypeStruct((N_GROUPS * cap, d), BF16),
            jax.ShapeDtypeStruct((N_GROUPS * cap, TAIL_LANES), F32),
        ),
        grid=(bsz, ns),
        in_specs=[
            pl.BlockSpec((1, ts, d), lambda b, s: (b, s, 0)),
            pl.BlockSpec((1, mlen, d), lambda b, s: (b, 0, 0)),
            const(wq), const(wkv), const(wo), const(lng), const(lnb),
            const(rw2), const(rbias),
        ],
        out_specs=(
            pl.BlockSpec((1, ts, d), lambda b, s: (b, s, 0)),
            pl.BlockSpec((ts, TAIL_LANES), lambda b, s: (b * ns + s, 0)),
            pl.BlockSpec((1, SUBLANES, LANES), lambda b, s: (b * ns + s, 0, 0)),
            pl.BlockSpec(memory_space=pl.ANY),
            pl.BlockSpec(memory_space=pl.ANY),
        ),
        scratch_shapes=[
            pltpu.VMEM((mlen, d), BF16),
            pltpu.VMEM((mlen, d), BF16),
            pltpu.VMEM((SUBLANES, LANES), F32),
            pltpu.VMEM((STAGE_ROWS, d), BF16),
            pltpu.VMEM((STAGE_ROWS, TAIL_LANES), F32),
            pltpu.VMEM((SUBLANES, LANES), I32),
            pltpu.SMEM((SUBLANES, LANES), I32),
            pltpu.SMEM((1,), I32),
            pltpu.VMEM((zero_rows, d), BF16),
            pltpu.VMEM((zero_rows, TAIL_LANES), F32),
            pltpu.SemaphoreType.DMA((2,)),
            pltpu.SemaphoreType.DMA,
            pltpu.SemaphoreType.DMA((2,)),
        ],
        compiler_params=pltpu.CompilerParams(
            dimension_semantics=("arbitrary", "arbitrary"), vmem_limit_bytes=VMEM_LIMIT,
            has_side_effects=True),
        name="xattn_router",
    )(x, mem, wq, wkv, wo, lng, lnb, rw2, rbias)


def _ffn_kernel(blk_in_ref, grp_ref, used_ref, xs_ref, tl_ref, wg_ref, wu_ref, wd_ref, o_ref):
    de = wg_ref.shape[2] // EXPERTS_PER_GROUP

    @pl.when(used_ref[pl.program_id(0)] == 1)
    def _():
        xb = xs_ref[...]
        hg = _dot(xb, wg_ref[0])
        hu = _dot(xb, wu_ref[0])
        hid = hg * _sigmoid(hg) * hu
        parts = []
        for j in range(EXPERTS_PER_GROUP):
            gate = tl_ref[:, j:j + 1]
            hj = hid[:, j * de:(j + 1) * de]
            parts.append(jnp.where(gate != 0.0, hj * gate, 0.0))
        hid = jnp.concatenate(parts, axis=1).astype(BF16)
        o_ref[...] = _dot(hid, wd_ref[0]).astype(BF16)


def _ffn(blk_in, blk_grp, used, xs, tls, wg, wu, wd):
    d = xs.shape[1]
    blk = FFN_BLOCK
    grid_spec = pltpu.PrefetchScalarGridSpec(
        num_scalar_prefetch=3,
        grid=(blk_in.shape[0],),
        in_specs=[
            pl.BlockSpec((blk, d), lambda i, bi, grp, us: (bi[i], 0)),
            pl.BlockSpec((blk, TAIL_LANES), lambda i, bi, grp, us: (bi[i], 0)),
            pl.BlockSpec((1,) + wg.shape[1:], lambda i, bi, grp, us: (grp[i], 0, 0)),
            pl.BlockSpec((1,) + wu.shape[1:], lambda i, bi, grp, us: (grp[i], 0, 0)),
            pl.BlockSpec((1,) + wd.shape[1:], lambda i, bi, grp, us: (grp[i], 0, 0)),
        ],
        out_specs=pl.BlockSpec((blk, d), lambda i, bi, grp, us: (bi[i], 0)),
    )
    return pl.pallas_call(
        _ffn_kernel,
        out_shape=jax.ShapeDtypeStruct(xs.shape, BF16),
        grid_spec=grid_spec,
        compiler_params=pltpu.CompilerParams(
            dimension_semantics=("arbitrary",), vmem_limit_bytes=VMEM_LIMIT),
        name="group_ffn",
    )(blk_in, blk_grp, used, xs, tls, wg, wu, wd)


def _group_weights(w):
    e, a, b = w.shape
    return w.reshape(N_GROUPS, EXPERTS_PER_GROUP, a, b).transpose(0, 2, 1, 3).reshape(
        N_GROUPS, a, EXPERTS_PER_GROUP * b).astype(BF16)


def _block_tables(seg_rows, cap, blk, n_steps):
    nblk = (seg_rows + blk - 1) // blk
    bend = jnp.cumsum(nblk)
    bstart = bend - nblk
    step = jnp.arange(n_steps, dtype=I32)
    used = step < bend[-1]
    grp = jnp.minimum(jnp.searchsorted(bend, step, side="right"), N_GROUPS - 1).astype(I32)
    blk_in = grp * (cap // blk) + step - bstart[grp]
    last_real = jnp.maximum(bend[-1] - 1, 0)
    blk_in = jnp.where(used, blk_in, blk_in[last_real])
    grp = jnp.where(used, grp, grp[last_real])
    return blk_in.astype(I32), grp.astype(I32), used.astype(I32)


def _flat_meta(meta):
    m = meta[:, :N_GROUPS, :]
    rec = jnp.concatenate([m[:, :, M_SLOT], m[:, :, M_LEN], m[:, :, M_OFF], m[:, :1, M_TOTAL],
                           jnp.zeros((m.shape[0], META_W - 3 * N_GROUPS - 1), I32)], axis=1)
    return rec.reshape(-1)


def kernel(x, mem, w_in, b_i, b_f, conv_qk, head_norm_g, pool_w, pool_scale, w_mix_out,
           ln_mix_g, ln_mix_b, w_xq, w_xkv, w_xo, ln_x_g, ln_x_b, router_w, router_bias,
           w_gate, w_up, w_down, ln_moe_g, ln_moe_b):
    bsz, seq, d = x.shape
    n_tok = bsz * seq
    n_tiles = n_tok // SEQ_TILE
    mw = N_HEADS * HEAD_DIM
    n_gate = 2 * N_HEADS
    blk = FFN_BLOCK
    cap = -(-(n_tok + RUN_ALIGN * n_tiles) // blk) * blk
    n_steps = (n_tok + N_GROUPS * RUN_ALIGN * n_tiles) // blk + N_GROUPS

    rw = jnp.pad(router_w, ((0, 0), (0, LANES - N_EXPERTS)))
    rwh, rwm, _ = _split3(rw)
    rw2 = jnp.concatenate([rwh, rwm], axis=1)
    rbias = router_bias.reshape(N_EXPERTS, 1).astype(F32)
    row = lambda v: v.reshape(1, -1).astype(F32)

    combine = None
    for l in range(DEPTH):
        wa = w_in[l][:, :4 * mw].astype(BF16)
        wu = w_in[l][:, 4 * mw + n_gate:].astype(BF16)
        wif = jnp.pad(w_in[l][:, 4 * mw:4 * mw + n_gate], ((0, 0), (0, LANES - n_gate))).astype(BF16)
        bif = jnp.pad(jnp.concatenate([b_i[l], b_f[l]]), (0, LANES - n_gate)).reshape(1, LANES)
        weights = (wa, wu, wif, bif, conv_qk[l], row(head_norm_g[l]), pool_w[l].astype(BF16),
                   row(pool_scale[l]), w_mix_out[l].astype(BF16), row(ln_mix_g[l]), row(ln_mix_b[l]))
        x = _mixer(x, weights, combine)

        x2, tail, meta, xs, tls = _xattn_router(
            x, mem, w_xq[l].astype(BF16), w_xkv[l].astype(BF16), w_xo[l].astype(BF16),
            row(ln_x_g[l]), row(ln_x_b[l]), rw2, rbias, cap, blk)

        seg_rows = meta[-1, :N_GROUPS, M_END]
        blk_in, blk_grp, used = _block_tables(seg_rows, cap, blk, n_steps)
        ys = _ffn(blk_in, blk_grp, used, xs, tls, _group_weights(w_gate[l]), _group_weights(w_up[l]),
                  w_down[l].reshape(N_GROUPS, EXPERTS_PER_GROUP * w_down.shape[2], d).astype(BF16))
        x = x2
        combine = (_flat_meta(meta), tail, ys, row(ln_moe_g[l]), row(ln_moe_b[l]))

    meta, tail, ys, cg, cb = combine
    return _final_combine(meta, x.reshape(n_tok, d), tail, ys, cg, cb).reshape(bsz, seq, d)
```

```python
import functools

import jax
import jax.numpy as jnp
from jax import lax
from jax.experimental import pallas as pl
from jax.experimental.pallas import tpu as pltpu

F32 = jnp.float32
BF16 = jnp.bfloat16
I32 = jnp.int32

N_HEADS = 4
HEAD_DIM = 128
POOL_WINDOWS = (2, 4, 8, 16)
POOL_GROUP = 128
CONV_WIDTH = 4
XATTN_HEADS = 4
N_EXPERTS = 16
N_GROUPS = 4
EXPERTS_PER_GROUP = 4
DEPTH = 2
ALPHA = (2 * DEPTH) ** 0.25
LN_EPS = 1e-5

LANES = 128
SUBLANES = 8
BF16_TILE_ROWS = 16
VMEM_LIMIT = 56 * 1024 * 1024

SEQ_TILE = 256
MIXER_SEQS = 2
XATTN_SEQS = 2
HEADS_TOGETHER = 2
MLSTM_CHUNK = 256
FFN_BLOCK = 256
CONV_CARRY = 8
POOL_CARRY = 16

RUN_ALIGN = BF16_TILE_ROWS
STAGE_ROWS = SEQ_TILE + N_GROUPS * RUN_ALIGN
STAGE_ROWS_PADDED = 384
TAIL_LANES = 128
STAGE_LANE = 5
M_SLOT, M_LEN, M_OFF, M_TOTAL, M_END = 0, 1, 2, 3, 4
META_W = 16


def _layer_norm(y, g, b):
    mu = jnp.mean(y, axis=-1, keepdims=True)
    d = y - mu
    var = jnp.mean(d * d, axis=-1, keepdims=True)
    return d * lax.rsqrt(var + LN_EPS) * g + b


def _sigmoid(v):
    return 1.0 / (1.0 + jnp.exp(-v))


def _dot(a, b):
    return jnp.dot(a, b, preferred_element_type=F32)


def _dot_nt(a, b):
    return lax.dot_general(a, b, (((1,), (1,)), ((), ())), preferred_element_type=F32)


def _split3(v):
    hi = v.astype(BF16)
    r1 = v - hi.astype(F32)
    mid = r1.astype(BF16)
    lo = (r1 - mid.astype(F32)).astype(BF16)
    return hi, mid, lo


def _loop(n, body, unroll=1):
    lax.fori_loop(0, n, lambda j, c: (body(j), c)[1], 0, unroll=unroll)


def _aligned(v):
    return pl.multiple_of(v, RUN_ALIGN)


def _run_copy(src, src_row, dst, dst_row, length, sem):
    length = _aligned(length)

    @pl.when(length > 0)
    def _():
        pltpu.make_async_copy(src.at[pl.ds(_aligned(src_row), length)],
                              dst.at[pl.ds(_aligned(dst_row), length)], sem).start()


def _combine_fetch(meta_ref, tile, next_tile, first, has_next, slot, ys_hbm, stage, sem):
    def fetch(t, sl):
        for g in range(N_GROUPS):
            _run_copy(ys_hbm, meta_ref[t * META_W + g],
                      stage.at[sl], meta_ref[t * META_W + 2 * N_GROUPS + g],
                      meta_ref[t * META_W + N_GROUPS + g], sem.at[sl])

    @pl.when(first)
    def _():
        stage[...] = jnp.zeros(stage.shape, stage.dtype)
        fetch(tile, slot)

    @pl.when(has_next)
    def _():
        fetch(next_tile, 1 - slot)

    total = _aligned(meta_ref[tile * META_W + 3 * N_GROUPS])

    @pl.when(total > 0)
    def _():
        pltpu.make_async_copy(ys_hbm.at[pl.ds(0, total)], stage.at[slot, pl.ds(0, total)],
                              sem.at[slot]).wait()


def _combine(x2, tail, sorted_rows, lng, lnb):
    ts = x2.shape[0]
    pos = tail[:, STAGE_LANE:STAGE_LANE + 1].astype(I32)
    lane = lax.broadcasted_iota(I32, (ts, STAGE_ROWS_PADDED), 1)
    unsort = jnp.where(lane == pos, 1.0, 0.0).astype(BF16)
    return _layer_norm(ALPHA * x2 + _dot(unsort, sorted_rows), lng, lnb)


def _combine_scratch(d, lanes):
    return [pltpu.VMEM((lanes, 2, STAGE_ROWS_PADDED, d), BF16), pltpu.SemaphoreType.DMA((lanes, 2))]


def _final_kernel(meta_ref, x2_ref, tail_ref, ys_hbm, lng_ref, lnb_ref, o_ref, stage, sem):
    i = pl.program_id(0)
    _combine_fetch(meta_ref, i, i + 1, i == 0, i + 1 < pl.num_programs(0), i % 2, ys_hbm, stage.at[0], sem.at[0])
    o_ref[...] = _combine(x2_ref[...], tail_ref[...], stage[0, i % 2], lng_ref[...], lnb_ref[...])


def _final_combine(meta, x2, tail, ys, lng, lnb):
    n_tok, d = x2.shape
    ts = tail.shape[0] // (meta.shape[0] // META_W)
    grid_spec = pltpu.PrefetchScalarGridSpec(
        num_scalar_prefetch=1,
        grid=(n_tok // ts,),
        in_specs=[
            pl.BlockSpec((ts, d), lambda i, m: (i, 0)),
            pl.BlockSpec((ts, TAIL_LANES), lambda i, m: (i, 0)),
            pl.BlockSpec(memory_space=pl.ANY),
            pl.BlockSpec(lng.shape, lambda i, m: (0, 0)),
            pl.BlockSpec(lnb.shape, lambda i, m: (0, 0)),
        ],
        out_specs=pl.BlockSpec((ts, d), lambda i, m: (i, 0)),
        scratch_shapes=_combine_scratch(d, 1),
    )
    return pl.pallas_call(
        _final_kernel,
        out_shape=jax.ShapeDtypeStruct((n_tok, d), F32),
        grid_spec=grid_spec,
        compiler_params=pltpu.CompilerParams(
            dimension_semantics=("arbitrary",), vmem_limit_bytes=VMEM_LIMIT),
        name="final_combine",
    )(meta, x2, tail, ys, lng, lnb)


def _mixer_kernel(*refs, ts, lc, combine):
    n_lead = 6 if combine else 1
    zq_ext, u_carry, c_st, m_st = refs[n_lead + 12:n_lead + 16]
    b = pl.program_id(0)
    s = pl.program_id(1)
    ns = pl.num_programs(1)

    @pl.when(s == 0)
    def _():
        zq_ext[:, 0:CONV_CARRY, :] = jnp.zeros((MIXER_SEQS, CONV_CARRY, zq_ext.shape[2]), F32)
        u_carry[...] = jnp.zeros(u_carry.shape, F32)
        c_st[...] = jnp.zeros(c_st.shape, F32)
        m_st[...] = jnp.zeros(m_st.shape, F32)

    step = b * ns + s
    if combine:
        meta_ref, ys_hbm = refs[0], refs[3]
        stage, sem = refs[n_lead + 16:n_lead + 18]
        for lane in range(MIXER_SEQS):
            tile = (b * MIXER_SEQS + lane) * ns + s
            next_tile = jnp.where(s + 1 < ns, tile + 1, tile + (MIXER_SEQS - 1) * ns + 1)
            _combine_fetch(meta_ref, tile, next_tile, step == 0, step + 1 < pl.num_programs(0) * ns,
                           step % 2, ys_hbm, stage.at[lane], sem.at[lane])

    lanes = [_mixer_lane(lane, step % 2, refs, ts, lc, combine) for lane in range(MIXER_SEQS)]
    for _ in zip(*lanes):
        pass


def _mixer_lane(lane, slot, refs, ts, lc, combine):
    if combine:
        (_, x_ref, tail_ref, _, cg_ref, cb_ref), refs = refs[:6], refs[6:]
    else:
        x_ref, refs = refs[0], refs[1:]
    (wa_ref, wu_ref, wif_ref, bif_ref, conv_ref, hng_ref, poolw_ref, pscale_ref, wout_ref,
     lng_ref, lnb_ref, o_ref, zq_ext, u_carry, c_st, m_st) = refs[:16]
    zq_ext, u_carry, c_st, m_st = zq_ext.at[lane], u_carry.at[lane], c_st.at[lane], m_st.at[lane]
    s = pl.program_id(1)
    mw = N_HEADS * HEAD_DIM

    x = x_ref[lane]
    if combine:
        stage = refs[16]
        x = _combine(x, tail_ref[lane], stage[lane, slot], cg_ref[...], cb_ref[...])
    xb = x.astype(BF16)
    z = _dot(xb, wa_ref[...])
    u = _dot(xb, wu_ref[...])
    gts = _dot(xb, wif_ref[...]) + bif_ref[...]
    yield

    zq_ext[CONV_CARRY:CONV_CARRY + ts, :] = z[:, :2 * mw]
    cw = conv_ref[...]
    acc = zq_ext[CONV_CARRY:CONV_CARRY + ts, :] * cw[CONV_WIDTH - 1:CONV_WIDTH, :]
    for j in range(1, CONV_WIDTH):
        acc = acc + zq_ext[CONV_CARRY - j:CONV_CARRY - j + ts, :] * cw[CONV_WIDTH - 1 - j:CONV_WIDTH - j, :]
    zq_ext[0:CONV_CARRY, :] = zq_ext[ts:ts + CONV_CARRY, :]
    qk = acc * _sigmoid(acc)
    q_all = qk[:, :mw] * (HEAD_DIM ** -0.5)
    k_all = qk[:, mw:]
    v_all = z[:, 2 * mw:3 * mw]
    o_all = z[:, 3 * mw:4 * mw]

    lf_all = jnp.minimum(gts, 0.0) - jnp.log1p(jnp.exp(-jnp.abs(gts)))
    yield

    row_i = lax.broadcasted_iota(I32, (lc, lc), 0)
    col_i = lax.broadcasted_iota(I32, (lc, lc), 1)
    causal = col_i <= row_i
    tri = jnp.where(causal, 1.0, 0.0).astype(BF16)
    ones_col = jnp.where(lax.broadcasted_iota(I32, (lc, HEAD_DIM), 1) == 0, 1.0, 0.0).astype(BF16)

    head_out = [[] for _ in range(N_HEADS)]
    for c in range(ts // lc):
        rows = slice(c * lc, (c + 1) * lc)
        hi, mid, lo = _split3(lf_all[rows, :])
        b_all = _dot(tri, hi) + _dot(tri, mid) + _dot(tri, lo)
        g_c = gts[rows, :]
        r_all = g_c - pltpu.roll(b_all, LANES - N_HEADS, 1)
        r_t = r_all.T
        def head(h, rows=rows, b_all=b_all, g_c=g_c, r_t=r_t):
            hs = slice(h * HEAD_DIM, (h + 1) * HEAD_DIM)
            qh = q_all[rows, hs]
            kh = k_all[rows, hs]
            vh = jnp.concatenate([v_all[rows, hs].astype(BF16), ones_col], axis=1)
            qhb = qh.astype(BF16)
            bc = b_all[:, N_HEADS + h:N_HEADS + h + 1]
            igc = g_c[:, h:h + 1]
            r_row = r_t[h:h + 1, :]
            c_prev = c_st[h]
            m_prev = m_st[h][:, 0:1]
            qk = _dot_nt(qhb, kh.astype(BF16))
            qc = _dot(qhb, c_prev.astype(BF16))
            yield

            log_d = jnp.where(causal, bc + r_row, -jnp.inf)
            m_intra = jnp.max(log_d, axis=1, keepdims=True)
            log_inter = bc + m_prev
            m_t = jnp.maximum(m_intra, log_inter)
            p = jnp.exp(log_d - m_t) * qk
            inter = jnp.exp(log_inter - m_t)
            b_last = bc[lc - 1:lc, :]
            w_state = b_last - bc + igc
            m_loc = jnp.max(w_state, axis=0, keepdims=True)
            ka = kh * jnp.exp(w_state - m_loc)
            yield

            nd = _dot(p.astype(BF16), vh) + inter * qc
            c_loc = _dot(ka.T.astype(BF16), vh)
            yield

            den = nd[:, HEAD_DIM:HEAD_DIM + 1]
            hh = nd[:, :HEAD_DIM] * (1.0 / jnp.maximum(jnp.abs(den), jnp.exp(-m_t)))
            m_new = jnp.maximum(b_last + m_prev, m_loc)
            s_old = jnp.exp(b_last + m_prev - m_new)
            s_new = jnp.exp(m_loc - m_new)
            c_st[h] = s_old * c_prev + s_new * c_loc
            m_st[h] = jnp.broadcast_to(m_new, (1, LANES))

            mu = jnp.mean(hh, axis=1, keepdims=True)
            dlt = hh - mu
            var = jnp.mean(dlt * dlt, axis=1, keepdims=True)
            hn = dlt * lax.rsqrt(var + LN_EPS) * hng_ref[:, hs]
            head_out[h].append(hn * _sigmoid(o_all[rows, hs]))
            yield

        for h0 in range(0, N_HEADS, HEADS_TOGETHER):
            for _ in zip(*[head(h) for h in range(h0, h0 + HEADS_TOGETHER)]):
                yield

    mixed = [jnp.concatenate(ho, axis=0) if len(ho) > 1 else ho[0] for ho in head_out]

    ue = jnp.concatenate([u_carry[...], u], axis=0)
    u_carry[...] = u[ts - POOL_CARRY:ts, :]
    pos = (lax.broadcasted_iota(I32, (ts, 1), 0) + s * ts + 1).astype(F32)
    for g, w in enumerate(POOL_WINDOWS):
        cs = slice(g * POOL_GROUP, (g + 1) * POOL_GROUP)
        win = ue[:, cs]
        shift = 1
        while shift < w:
            win = win + pltpu.roll(win, shift, 0)
            shift *= 2
        ug = u[:, cs]
        pooled = win[POOL_CARRY:, :] / jnp.minimum(pos, float(w)) - ug
        pm = _dot(pooled.astype(BF16), poolw_ref[g]) * pscale_ref[:, cs]
        mixed.append(pm)
    yield

    mixed = jnp.concatenate(mixed, axis=1).astype(BF16)
    y = _dot(mixed, wout_ref[...])
    o_ref[lane] = _layer_norm(ALPHA * x + y, lng_ref[...], lnb_ref[...])
    yield


def _mixer(x, weights, combine=None):
    bsz, seq, d = x.shape
    ts = min(SEQ_TILE, seq)
    lc = min(MLSTM_CHUNK, ts)
    ns = seq // ts
    mw = N_HEADS * HEAD_DIM
    pw = weights[1].shape[1]
    const = lambda a: pl.BlockSpec(a.shape, lambda b, s, *_: (0,) * a.ndim)
    in_specs = [pl.BlockSpec((MIXER_SEQS, ts, d), lambda b, s, *_: (b, s, 0))]
    args = [x]
    scratch = [
        pltpu.VMEM((MIXER_SEQS, ts + CONV_CARRY, 2 * mw), F32),
        pltpu.VMEM((MIXER_SEQS, POOL_CARRY, pw), F32),
        pltpu.VMEM((MIXER_SEQS, N_HEADS, HEAD_DIM, 2 * HEAD_DIM), F32),
        pltpu.VMEM((MIXER_SEQS, N_HEADS, 1, LANES), F32),
    ]
    prefetch = []
    if combine is not None:
        meta, tail, ys, cg, cb = combine
        prefetch = [meta]
        in_specs += [pl.BlockSpec((MIXER_SEQS, ts, TAIL_LANES), lambda b, s, *_: (b, s, 0)),
                     pl.BlockSpec(memory_space=pl.ANY), const(cg), const(cb)]
        args += [tail, ys, cg, cb]
        scratch += _combine_scratch(d, MIXER_SEQS)
    in_specs += [const(w) for w in weights]
    args += list(weights)
    grid_spec = pltpu.PrefetchScalarGridSpec(
        num_scalar_prefetch=len(prefetch),
        grid=(bsz // MIXER_SEQS, ns),
        in_specs=in_specs,
        out_specs=pl.BlockSpec((MIXER_SEQS, ts, d), lambda b, s, *_: (b, s, 0)),
        scratch_shapes=scratch,
    )
    return pl.pallas_call(
        functools.partial(_mixer_kernel, ts=ts, lc=lc, combine=combine is not None),
        out_shape=jax.ShapeDtypeStruct((bsz, seq, d), F32),
        grid_spec=grid_spec,
        compiler_params=pltpu.CompilerParams(
            dimension_semantics=("arbitrary", "arbitrary"), vmem_limit_bytes=VMEM_LIMIT),
        name="mixer",
    )(*prefetch, *args)


def _top2_sum(a, b, c, d):
    hi1, lo1 = jnp.maximum(a, b), jnp.minimum(a, b)
    hi2, lo2 = jnp.maximum(c, d), jnp.minimum(c, d)
    return jnp.maximum(hi1, hi2) + jnp.maximum(jnp.minimum(hi1, hi2), jnp.maximum(lo1, lo2))


def _xattn_kernel(x_ref, mem_ref, wq_ref, wkv_ref, wo_ref, lng_ref, lnb_ref,
                  rw2_ref, rbias_ref,
                  x2_ref, tail_ref, meta_ref, xs_hbm, tls_hbm,
                  k_scr, v_scr, carry, stx, stt, mvec, msm, prev_total, zx, zt, sem, ssem, zsem,
                  *, ts, cap, blk):
    b = pl.program_id(0)
    s = pl.program_id(1)
    step = b * pl.num_programs(1) + s
    last = step == pl.num_programs(0) * pl.num_programs(1) - 1
    d = x_ref.shape[2]

    @pl.when(s == 0)
    def _():
        for lane in range(XATTN_SEQS):
            kv = _dot(mem_ref[lane].astype(BF16), wkv_ref[...])
            k_scr[lane] = kv[:, :d].astype(BF16)
            v_scr[lane] = kv[:, d:].astype(BF16)

    @pl.when(step == 0)
    def _():
        carry[...] = jnp.zeros(carry.shape, F32)

    results = [None] * XATTN_SEQS
    lanes = [_xattn_lane(lane, results, x_ref, wq_ref, wo_ref, lng_ref, lnb_ref, rw2_ref, rbias_ref,
                         x2_ref, tail_ref, k_scr, v_scr, ts) for lane in range(XATTN_SEQS)]
    for _ in zip(*lanes):
        pass

    sub1 = lax.broadcasted_iota(I32, (SUBLANES, 1), 0)
    lane8 = lax.broadcasted_iota(I32, (SUBLANES, LANES), 1)
    base = carry[:, 0:1]
    for lane in range(XATTN_SEQS):
        len8, off8, total, _, _ = results[lane]
        new_base = base + len8
        mv = jnp.where(lane8 == M_SLOT, sub1.astype(F32) * float(cap) + base, 0.0)
        mv = jnp.where(lane8 == M_LEN, len8, mv)
        mv = jnp.where(lane8 == M_OFF, off8, mv)
        mv = jnp.where(lane8 == M_TOTAL, total, mv)
        mv = jnp.where(lane8 == M_END, new_base, mv).astype(I32)
        meta_ref[lane, 0] = mv
        mvec[lane * SUBLANES:(lane + 1) * SUBLANES, :] = mv
        base = new_base
    carry[...] = jnp.broadcast_to(base, carry.shape)

    def wait_runs(lane, n_rows):
        pltpu.make_async_copy(stx.at[lane, pl.ds(0, n_rows)], xs_hbm.at[pl.ds(0, n_rows)],
                              sem.at[lane, 0]).wait()
        pltpu.make_async_copy(stt.at[lane, pl.ds(0, n_rows)], tls_hbm.at[pl.ds(0, n_rows)],
                              sem.at[lane, 1]).wait()

    @pl.when(step > 0)
    def _():
        for lane in range(XATTN_SEQS):
            n_prev = _aligned(prev_total[lane])

            @pl.when(n_prev > 0)
            def _(lane=lane, n_prev=n_prev):
                wait_runs(lane, n_prev)

    for lane in range(XATTN_SEQS):
        stx[lane] = results[lane][3]
        stt[lane] = results[lane][4]
    to_smem = pltpu.make_async_copy(mvec, msm, ssem)
    to_smem.start()
    to_smem.wait()

    for lane in range(XATTN_SEQS):
        for g in range(N_GROUPS):
            r = lane * SUBLANES + g
            _run_copy(stx.at[lane], msm[r, M_OFF], xs_hbm, msm[r, M_SLOT], msm[r, M_LEN], sem.at[lane, 0])
            _run_copy(stt.at[lane], msm[r, M_OFF], tls_hbm, msm[r, M_SLOT], msm[r, M_LEN], sem.at[lane, 1])
        prev_total[lane] = msm[lane * SUBLANES, M_TOTAL]

    @pl.when(last)
    def _():
        for lane in range(XATTN_SEQS):
            n_own = _aligned(msm[lane * SUBLANES, M_TOTAL])

            @pl.when(n_own > 0)
            def _(lane=lane, n_own=n_own):
                wait_runs(lane, n_own)

        zx[...] = jnp.zeros(zx.shape, BF16)
        zt[...] = jnp.zeros(zt.shape, F32)
        for g in range(N_GROUPS):
            end = msm[(XATTN_SEQS - 1) * SUBLANES + g, M_END]
            n_pad = _aligned((blk - end % blk) % blk)
            _run_copy(zx, 0, xs_hbm, g * cap + end, n_pad, zsem.at[0])
            _run_copy(zt, 0, tls_hbm, g * cap + end, n_pad, zsem.at[1])

            @pl.when(n_pad > 0)
            def _(n_pad=n_pad):
                pltpu.make_async_copy(zx.at[pl.ds(0, n_pad)], xs_hbm.at[pl.ds(0, n_pad)], zsem.at[0]).wait()
                pltpu.make_async_copy(zt.at[pl.ds(0, n_pad)], tls_hbm.at[pl.ds(0, n_pad)], zsem.at[1]).wait()


def _xattn_lane(lane, results, x_ref, wq_ref, wo_ref, lng_ref, lnb_ref, rw2_ref, rbias_ref,
                x2_ref, tail_ref, k_scr, v_scr, ts):
    d = x_ref.shape[2]
    dh = d // XATTN_HEADS
    x = x_ref[lane]
    q = (_dot(x.astype(BF16), wq_ref[...]) * (dh ** -0.5)).astype(BF16)
    yield
    outs = []
    for h in range(XATTN_HEADS):
        hs = slice(h * dh, (h + 1) * dh)
        sc = _dot_nt(q[:, hs], k_scr[lane, :, hs])
        e = jnp.exp(sc - jnp.max(sc, axis=1, keepdims=True))
        l = jnp.sum(e, axis=1, keepdims=True)
        outs.append(_dot(e.astype(BF16), v_scr[lane, :, hs]) * (1.0 / l))
        yield
    o = jnp.concatenate(outs, axis=1).astype(BF16)
    x2 = _layer_norm(ALPHA * x + _dot(o, wo_ref[...]), lng_ref[...], lnb_ref[...])
    x2_ref[lane] = x2
    yield

    xh, xm, _ = _split3(x2)
    both = _dot(xh, rw2_ref[...])
    logits = (both[:, :LANES] + both[:, LANES:]) + _dot(xm, rw2_ref[:, :LANES])
    yield
    lt = logits.T[0:N_EXPERTS, :]
    score = _sigmoid(lt)
    sel = score + rbias_ref[...]

    sel_r = [sel[e:e + 1, :] for e in range(N_EXPERTS)]
    score_r = [score[e:e + 1, :] for e in range(N_EXPERTS)]
    gs = [_top2_sum(*sel_r[EXPERTS_PER_GROUP * g:EXPERTS_PER_GROUP * (g + 1)]) for g in range(N_GROUPS)]
    best = jnp.zeros((1, ts), I32)
    bestv = gs[0]
    for g in range(1, N_GROUPS):
        better = gs[g] > bestv
        best = jnp.where(better, g, best)
        bestv = jnp.where(better, gs[g], bestv)
    in_g = [best == g for g in range(N_GROUPS)]

    def pick(rows, j):
        out = rows[j]
        for g in range(1, N_GROUPS):
            out = jnp.where(in_g[g], rows[EXPERTS_PER_GROUP * g + j], out)
        return out

    vsel = [pick(sel_r, j) for j in range(EXPERTS_PER_GROUP)]
    vsc = [pick(score_r, j) for j in range(EXPERTS_PER_GROUP)]
    gates = []
    for j in range(EXPERTS_PER_GROUP):
        beaten = jnp.zeros((1, ts), I32)
        for k in range(EXPERTS_PER_GROUP):
            if k == j:
                continue
            wins = (vsel[k] > vsel[j]) | ((vsel[k] == vsel[j]) & (k < j))
            beaten = beaten + wins.astype(I32)
        gates.append(jnp.where(beaten < 2, vsc[j], 0.0))
    gsum = gates[0] + gates[1] + gates[2] + gates[3]
    gates = [g / gsum for g in gates]

    sub = lax.broadcasted_iota(I32, (SUBLANES, ts), 0)
    oh8 = jnp.zeros((SUBLANES, ts), F32)
    for g in range(N_GROUPS):
        oh8 = jnp.where((sub == g) & in_g[g], 1.0, oh8)
    r_i = lax.broadcasted_iota(I32, (ts, ts), 0)
    c_i = lax.broadcasted_iota(I32, (ts, ts), 1)
    upper = jnp.where(r_i < c_i, 1.0, 0.0).astype(BF16)
    excl = _dot(oh8.astype(BF16), upper)
    n8 = jnp.sum(oh8, axis=1, keepdims=True)
    len8 = jnp.floor((n8 + (RUN_ALIGN - 1)) * (1.0 / RUN_ALIGN)) * RUN_ALIGN
    sub1 = lax.broadcasted_iota(I32, (SUBLANES, 1), 0)
    off8 = jnp.zeros((SUBLANES, 1), F32)
    run_off = jnp.zeros((1, 1), F32)
    for g in range(N_GROUPS):
        off8 = jnp.where(sub1 == g, run_off, off8)
        run_off = run_off + len8[g:g + 1, :]
    pos = jnp.sum(jnp.where(oh8 > 0.0, off8 + excl, 0.0), axis=0, keepdims=True)

    t8 = jnp.where(sub == STAGE_LANE, pos, 0.0)
    for j in range(EXPERTS_PER_GROUP):
        t8 = jnp.where(sub == j, gates[j], t8)
    tail = jnp.concatenate([t8, jnp.zeros((TAIL_LANES - SUBLANES, ts), F32)], axis=0).T
    tail_ref[lane] = tail
    yield

    srow = lax.broadcasted_iota(I32, (STAGE_ROWS, ts), 0)
    sort = jnp.where(srow == pos.astype(I32), 1.0, 0.0).astype(BF16)
    xs_sorted = _dot(sort, xh).astype(BF16)
    pieces = _dot(sort, jnp.concatenate(_split3(tail), axis=1))
    tail_sorted = (pieces[:, :LANES] + pieces[:, LANES:2 * LANES]) + pieces[:, 2 * LANES:]
    results[lane] = (len8, off8, run_off, xs_sorted, tail_sorted)
    yield


def _xattn_router(x, mem, wq, wkv, wo, lng, lnb, rw2, rbias, cap, blk):
    bsz, seq, d = x.shape
    mlen = mem.shape[1]
    ts = min(SEQ_TILE, seq)
    assert ts == SEQ_TILE and blk % RUN_ALIGN == 0
    ns = seq // ts
    nl = XATTN_SEQS
    const = lambda a: pl.BlockSpec(a.shape, lambda b, s: (0,) * a.ndim)
    zero_rows = blk
    return pl.pallas_call(
        functools.partial(_xattn_kernel, ts=ts, cap=cap, blk=blk),
        out_shape=(
            jax.ShapeDtypeStruct((bsz, seq, d), F32),
            jax.ShapeDtypeStruct((bsz, seq, TAIL_LANES), F32),
            jax.ShapeDtypeStruct((bsz, ns, SUBLANES, LANES), I32),
            jax.ShapeDtypeStruct((N_GROUPS * cap, d), BF16),
            jax.ShapeDtypeStruct((N_GROUPS * cap, TAIL_LANES), F32),
        ),
        grid=(bsz // nl, ns),
        in_specs=[
            pl.BlockSpec((nl, ts, d), lambda b, s: (b, s, 0)),
            pl.BlockSpec((nl, mlen, d), lambda b, s: (b, 0, 0)),
            const(wq), const(wkv), const(wo), const(lng), const(lnb),
            const(rw2), const(rbias),
        ],
        out_specs=(
            pl.BlockSpec((nl, ts, d), lambda b, s: (b, s, 0)),
            pl.BlockSpec((nl, ts, TAIL_LANES), lambda b, s: (b, s, 0)),
            pl.BlockSpec((nl, 1, SUBLANES, LANES), lambda b, s: (b, s, 0, 0)),
            pl.BlockSpec(memory_space=pl.ANY),
            pl.BlockSpec(memory_space=pl.ANY),
        ),
        scratch_shapes=[
            pltpu.VMEM((nl, mlen, d), BF16),
            pltpu.VMEM((nl, mlen, d), BF16),
            pltpu.VMEM((SUBLANES, LANES), F32),
            pltpu.VMEM((nl, STAGE_ROWS, d), BF16),
            pltpu.VMEM((nl, STAGE_ROWS, TAIL_LANES), F32),
            pltpu.VMEM((nl * SUBLANES, LANES), I32),
            pltpu.SMEM((nl * SUBLANES, LANES), I32),
            pltpu.SMEM((nl,), I32),
            pltpu.VMEM((zero_rows, d), BF16),
            pltpu.VMEM((zero_rows, TAIL_LANES), F32),
            pltpu.SemaphoreType.DMA((nl, 2)),
            pltpu.SemaphoreType.DMA,
            pltpu.SemaphoreType.DMA((2,)),
        ],
        compiler_params=pltpu.CompilerParams(
            dimension_semantics=("arbitrary", "arbitrary"), vmem_limit_bytes=VMEM_LIMIT,
            has_side_effects=True),
        name="xattn_router",
    )(x, mem, wq, wkv, wo, lng, lnb, rw2, rbias)


def _ffn_kernel(blk_in_ref, grp_ref, used_ref, xs_ref, tl_ref, wg_ref, wu_ref, wd_ref, o_ref):
    de = wg_ref.shape[2] // EXPERTS_PER_GROUP

    @pl.when(used_ref[pl.program_id(0)] == 1)
    def _():
        xb = xs_ref[...]
        hg = _dot(xb, wg_ref[0])
        hu = _dot(xb, wu_ref[0])
        hid = hg * _sigmoid(hg) * hu
        parts = []
        for j in range(EXPERTS_PER_GROUP):
            gate = tl_ref[:, j:j + 1]
            hj = hid[:, j * de:(j + 1) * de]
            parts.append(jnp.where(gate != 0.0, hj * gate, 0.0))
        hid = jnp.concatenate(parts, axis=1).astype(BF16)
        o_ref[...] = _dot(hid, wd_ref[0]).astype(BF16)


def _ffn(blk_in, blk_grp, used, xs, tls, wg, wu, wd):
    d = xs.shape[1]
    blk = FFN_BLOCK
    grid_spec = pltpu.PrefetchScalarGridSpec(
        num_scalar_prefetch=3,
        grid=(blk_in.shape[0],),
        in_specs=[
            pl.BlockSpec((blk, d), lambda i, bi, grp, us: (bi[i], 0)),
            pl.BlockSpec((blk, TAIL_LANES), lambda i, bi, grp, us: (bi[i], 0)),
            pl.BlockSpec((1,) + wg.shape[1:], lambda i, bi, grp, us: (grp[i], 0, 0)),
            pl.BlockSpec((1,) + wu.shape[1:], lambda i, bi, grp, us: (grp[i], 0, 0)),
            pl.BlockSpec((1,) + wd.shape[1:], lambda i, bi, grp, us: (grp[i], 0, 0)),
        ],
        out_specs=pl.BlockSpec((blk, d), lambda i, bi, grp, us: (bi[i], 0)),
    )
    return pl.pallas_call(
        _ffn_kernel,
        out_shape=jax.ShapeDtypeStruct(xs.shape, BF16),
        grid_spec=grid_spec,
        compiler_params=pltpu.CompilerParams(
            dimension_semantics=("arbitrary",), vmem_limit_bytes=VMEM_LIMIT),
        name="group_ffn",
    )(blk_in, blk_grp, used, xs, tls, wg, wu, wd)


def _group_weights(w):
    e, a, b = w.shape
    return w.reshape(N_GROUPS, EXPERTS_PER_GROUP, a, b).transpose(0, 2, 1, 3).reshape(
        N_GROUPS, a, EXPERTS_PER_GROUP * b).astype(BF16)


def _block_tables(seg_rows, cap, blk, n_steps):
    nblk = (seg_rows + blk - 1) // blk
    bend = jnp.cumsum(nblk)
    bstart = bend - nblk
    step = jnp.arange(n_steps, dtype=I32)
    used = step < bend[-1]
    grp = jnp.minimum(jnp.searchsorted(bend, step, side="right"), N_GROUPS - 1).astype(I32)
    blk_in = grp * (cap // blk) + step - bstart[grp]
    last_real = jnp.maximum(bend[-1] - 1, 0)
    blk_in = jnp.where(used, blk_in, blk_in[last_real])
    grp = jnp.where(used, grp, grp[last_real])
    return blk_in.astype(I32), grp.astype(I32), used.astype(I32)


def _flat_meta(meta):
    m = meta.reshape(-1, SUBLANES, LANES)[:, :N_GROUPS, :]
    rec = jnp.concatenate([m[:, :, M_SLOT], m[:, :, M_LEN], m[:, :, M_OFF], m[:, :1, M_TOTAL],
                           jnp.zeros((m.shape[0], META_W - 3 * N_GROUPS - 1), I32)], axis=1)
    return rec.reshape(-1)


def kernel(x, mem, w_in, b_i, b_f, conv_qk, head_norm_g, pool_w, pool_scale, w_mix_out,
           ln_mix_g, ln_mix_b, w_xq, w_xkv, w_xo, ln_x_g, ln_x_b, router_w, router_bias,
           w_gate, w_up, w_down, ln_moe_g, ln_moe_b):
    bsz, seq, d = x.shape
    n_tok = bsz * seq
    n_tiles = n_tok // SEQ_TILE
    mw = N_HEADS * HEAD_DIM
    n_gate = 2 * N_HEADS
    blk = FFN_BLOCK
    cap = -(-(n_tok + RUN_ALIGN * n_tiles) // blk) * blk
    n_steps = (n_tok + N_GROUPS * RUN_ALIGN * n_tiles) // blk + N_GROUPS

    rw = jnp.pad(router_w, ((0, 0), (0, LANES - N_EXPERTS)))
    rwh, rwm, _ = _split3(rw)
    rw2 = jnp.concatenate([rwh, rwm], axis=1)
    rbias = router_bias.reshape(N_EXPERTS, 1).astype(F32)
    row = lambda v: v.reshape(1, -1).astype(F32)

    combine = None
    for l in range(DEPTH):
        wa = w_in[l][:, :4 * mw].astype(BF16)
        wu = w_in[l][:, 4 * mw + n_gate:].astype(BF16)
        wif = jnp.pad(w_in[l][:, 4 * mw:4 * mw + n_gate], ((0, 0), (0, LANES - n_gate))).astype(BF16)
        bif = jnp.pad(jnp.concatenate([b_i[l], b_f[l]]), (0, LANES - n_gate)).reshape(1, LANES)
        weights = (wa, wu, wif, bif, conv_qk[l], row(head_norm_g[l]), pool_w[l].astype(BF16),
                   row(pool_scale[l]), w_mix_out[l].astype(BF16), row(ln_mix_g[l]), row(ln_mix_b[l]))
        x = _mixer(x, weights, combine)

        x2, tail, meta, xs, tls = _xattn_router(
            x, mem, w_xq[l].astype(BF16), w_xkv[l].astype(BF16), w_xo[l].astype(BF16),
            row(ln_x_g[l]), row(ln_x_b[l]), rw2, rbias, cap, blk)

        seg_rows = meta[-1, -1, :N_GROUPS, M_END]
        blk_in, blk_grp, used = _block_tables(seg_rows, cap, blk, n_steps)
        ys = _ffn(blk_in, blk_grp, used, xs, tls, _group_weights(w_gate[l]), _group_weights(w_up[l]),
                  w_down[l].reshape(N_GROUPS, EXPERTS_PER_GROUP * w_down.shape[2], d).astype(BF16))
        x = x2
        combine = (_flat_meta(meta), tail, ys, row(ln_moe_g[l]), row(ln_moe_b[l]))

    meta, tail, ys, cg, cb = combine
    return _final_combine(meta, x.reshape(n_tok, d), tail.reshape(n_tok, TAIL_LANES), ys, cg, cb).reshape(bsz, seq, d)
```

```python
import functools

import jax
import jax.numpy as jnp
from jax import lax
from jax.experimental import pallas as pl
from jax.experimental.pallas import tpu as pltpu

F32 = jnp.float32
BF16 = jnp.bfloat16
I32 = jnp.int32

N_HEADS = 4
HEAD_DIM = 128
POOL_WINDOWS = (2, 4, 8, 16)
POOL_GROUP = 128
CONV_WIDTH = 4
XATTN_HEADS = 4
N_EXPERTS = 16
N_GROUPS = 4
EXPERTS_PER_GROUP = 4
DEPTH = 2
ALPHA = (2 * DEPTH) ** 0.25
LN_EPS = 1e-5

LANES = 128
SUBLANES = 8
BF16_TILE_ROWS = 16
VMEM_LIMIT = 56 * 1024 * 1024

SEQ_TILE = 256
MIXER_SEQS = 2
XATTN_SEQS = 2
FINAL_TILES = 2
HEADS_TOGETHER = 2
MLSTM_CHUNK = 256
FFN_BLOCK = 256
CONV_CARRY = 8
POOL_CARRY = 16

RUN_ALIGN = BF16_TILE_ROWS
STAGE_ROWS = SEQ_TILE + N_GROUPS * RUN_ALIGN
STAGE_ROWS_PADDED = 384
TAIL_LANES = 128
STAGE_LANE = 5
M_SLOT, M_LEN, M_OFF, M_TOTAL, M_END = 0, 1, 2, 3, 4
META_W = 16


def _layer_norm(y, g, b):
    mu = jnp.mean(y, axis=-1, keepdims=True)
    d = y - mu
    var = jnp.mean(d * d, axis=-1, keepdims=True)
    return d * lax.rsqrt(var + LN_EPS) * g + b


def _sigmoid(v):
    return 1.0 / (1.0 + jnp.exp(-v))


def _dot(a, b):
    return jnp.dot(a, b, preferred_element_type=F32)


def _dot_nt(a, b):
    return lax.dot_general(a, b, (((1,), (1,)), ((), ())), preferred_element_type=F32)


def _split3(v):
    hi = v.astype(BF16)
    r1 = v - hi.astype(F32)
    mid = r1.astype(BF16)
    lo = (r1 - mid.astype(F32)).astype(BF16)
    return hi, mid, lo


def _loop(n, body, unroll=1):
    lax.fori_loop(0, n, lambda j, c: (body(j), c)[1], 0, unroll=unroll)


def _aligned(v):
    return pl.multiple_of(v, RUN_ALIGN)


def _run_copy(src, src_row, dst, dst_row, length, sem):
    length = _aligned(length)

    @pl.when(length > 0)
    def _():
        pltpu.make_async_copy(src.at[pl.ds(_aligned(src_row), length)],
                              dst.at[pl.ds(_aligned(dst_row), length)], sem).start()


def _combine_fetch(meta_ref, tile, next_tile, first, has_next, slot, ys_hbm, stage, sem):
    def fetch(t, sl):
        for g in range(N_GROUPS):
            _run_copy(ys_hbm, meta_ref[t * META_W + g],
                      stage.at[sl], meta_ref[t * META_W + 2 * N_GROUPS + g],
                      meta_ref[t * META_W + N_GROUPS + g], sem.at[sl])

    @pl.when(first)
    def _():
        stage[...] = jnp.zeros(stage.shape, stage.dtype)
        fetch(tile, slot)

    @pl.when(has_next)
    def _():
        fetch(next_tile, 1 - slot)

    total = _aligned(meta_ref[tile * META_W + 3 * N_GROUPS])

    @pl.when(total > 0)
    def _():
        pltpu.make_async_copy(ys_hbm.at[pl.ds(0, total)], stage.at[slot, pl.ds(0, total)],
                              sem.at[slot]).wait()


def _combine(x2, tail, sorted_rows, lng, lnb):
    ts = x2.shape[0]
    pos = tail[:, STAGE_LANE:STAGE_LANE + 1].astype(I32)
    lane = lax.broadcasted_iota(I32, (ts, STAGE_ROWS_PADDED), 1)
    unsort = jnp.where(lane == pos, 1.0, 0.0).astype(BF16)
    return _layer_norm(ALPHA * x2 + _dot(unsort, sorted_rows), lng, lnb)


def _combine_scratch(d, lanes):
    return [pltpu.VMEM((lanes, 2, STAGE_ROWS_PADDED, d), BF16), pltpu.SemaphoreType.DMA((lanes, 2))]


def _final_kernel(meta_ref, x2_ref, tail_ref, ys_hbm, lng_ref, lnb_ref, o_ref, stage, sem):
    i = pl.program_id(0)
    ts = o_ref.shape[0] // FINAL_TILES
    for lane in range(FINAL_TILES):
        tile = i * FINAL_TILES + lane
        _combine_fetch(meta_ref, tile, tile + FINAL_TILES, i == 0, i + 1 < pl.num_programs(0), i % 2,
                       ys_hbm, stage.at[lane], sem.at[lane])
    for lane in range(FINAL_TILES):
        rows = slice(lane * ts, (lane + 1) * ts)
        o_ref[rows, :] = _combine(x2_ref[rows, :], tail_ref[rows, :], stage[lane, i % 2],
                                  lng_ref[...], lnb_ref[...])


def _final_combine(meta, x2, tail, ys, lng, lnb):
    n_tok, d = x2.shape
    ts = FINAL_TILES * (tail.shape[0] // (meta.shape[0] // META_W))
    grid_spec = pltpu.PrefetchScalarGridSpec(
        num_scalar_prefetch=1,
        grid=(n_tok // ts,),
        in_specs=[
            pl.BlockSpec((ts, d), lambda i, m: (i, 0)),
            pl.BlockSpec((ts, TAIL_LANES), lambda i, m: (i, 0)),
            pl.BlockSpec(memory_space=pl.ANY),
            pl.BlockSpec(lng.shape, lambda i, m: (0, 0)),
            pl.BlockSpec(lnb.shape, lambda i, m: (0, 0)),
        ],
        out_specs=pl.BlockSpec((ts, d), lambda i, m: (i, 0)),
        scratch_shapes=_combine_scratch(d, FINAL_TILES),
    )
    return pl.pallas_call(
        _final_kernel,
        out_shape=jax.ShapeDtypeStruct((n_tok, d), F32),
        grid_spec=grid_spec,
        compiler_params=pltpu.CompilerParams(
            dimension_semantics=("arbitrary",), vmem_limit_bytes=VMEM_LIMIT),
        name="final_combine",
    )(meta, x2, tail, ys, lng, lnb)


def _mixer_kernel(*refs, ts, lc, combine):
    n_lead = 6 if combine else 1
    zq_ext, u_carry, c_st, m_st = refs[n_lead + 12:n_lead + 16]
    b = pl.program_id(0)
    s = pl.program_id(1)
    ns = pl.num_programs(1)

    @pl.when(s == 0)
    def _():
        zq_ext[:, 0:CONV_CARRY, :] = jnp.zeros((MIXER_SEQS, CONV_CARRY, zq_ext.shape[2]), F32)
        u_carry[...] = jnp.zeros(u_carry.shape, F32)
        c_st[...] = jnp.zeros(c_st.shape, F32)
        m_st[...] = jnp.zeros(m_st.shape, F32)

    step = b * ns + s
    if combine:
        meta_ref, ys_hbm = refs[0], refs[3]
        stage, sem = refs[n_lead + 16:n_lead + 18]
        for lane in range(MIXER_SEQS):
            tile = (b * MIXER_SEQS + lane) * ns + s
            next_tile = jnp.where(s + 1 < ns, tile + 1, tile + (MIXER_SEQS - 1) * ns + 1)
            _combine_fetch(meta_ref, tile, next_tile, step == 0, step + 1 < pl.num_programs(0) * ns,
                           step % 2, ys_hbm, stage.at[lane], sem.at[lane])

    lanes = [_mixer_lane(lane, step % 2, refs, ts, lc, combine) for lane in range(MIXER_SEQS)]
    for _ in zip(*lanes):
        pass


def _mixer_lane(lane, slot, refs, ts, lc, combine):
    if combine:
        (_, x_ref, tail_ref, _, cg_ref, cb_ref), refs = refs[:6], refs[6:]
    else:
        x_ref, refs = refs[0], refs[1:]
    (wa_ref, wu_ref, wif_ref, bif_ref, conv_ref, hng_ref, poolw_ref, pscale_ref, wout_ref,
     lng_ref, lnb_ref, o_ref, zq_ext, u_carry, c_st, m_st) = refs[:16]
    zq_ext, u_carry, c_st, m_st = zq_ext.at[lane], u_carry.at[lane], c_st.at[lane], m_st.at[lane]
    s = pl.program_id(1)
    mw = N_HEADS * HEAD_DIM

    x = x_ref[lane]
    if combine:
        stage = refs[16]
        x = _combine(x, tail_ref[lane], stage[lane, slot], cg_ref[...], cb_ref[...])
    xb = x.astype(BF16)
    z = _dot(xb, wa_ref[...])
    u = _dot(xb, wu_ref[...])
    gts = _dot(xb, wif_ref[...]) + bif_ref[...]
    yield

    zq_ext[CONV_CARRY:CONV_CARRY + ts, :] = z[:, :2 * mw]
    cw = conv_ref[...]
    acc = zq_ext[CONV_CARRY:CONV_CARRY + ts, :] * cw[CONV_WIDTH - 1:CONV_WIDTH, :]
    for j in range(1, CONV_WIDTH):
        acc = acc + zq_ext[CONV_CARRY - j:CONV_CARRY - j + ts, :] * cw[CONV_WIDTH - 1 - j:CONV_WIDTH - j, :]
    zq_ext[0:CONV_CARRY, :] = zq_ext[ts:ts + CONV_CARRY, :]
    qk = acc * _sigmoid(acc)
    q_all = qk[:, :mw] * (HEAD_DIM ** -0.5)
    k_all = qk[:, mw:]
    v_all = z[:, 2 * mw:3 * mw]
    o_all = z[:, 3 * mw:4 * mw]

    lf_all = jnp.minimum(gts, 0.0) - jnp.log1p(jnp.exp(-jnp.abs(gts)))
    yield

    row_i = lax.broadcasted_iota(I32, (lc, lc), 0)
    col_i = lax.broadcasted_iota(I32, (lc, lc), 1)
    causal = col_i <= row_i
    tri = jnp.where(causal, 1.0, 0.0).astype(BF16)
    ones_col = jnp.where(lax.broadcasted_iota(I32, (lc, HEAD_DIM), 1) == 0, 1.0, 0.0).astype(BF16)

    head_out = [[] for _ in range(N_HEADS)]
    for c in range(ts // lc):
        rows = slice(c * lc, (c + 1) * lc)
        hi, mid, lo = _split3(lf_all[rows, :])
        b_all = _dot(tri, hi) + _dot(tri, mid) + _dot(tri, lo)
        g_c = gts[rows, :]
        r_all = g_c - pltpu.roll(b_all, LANES - N_HEADS, 1)
        r_t = r_all.T
        def head(h, rows=rows, b_all=b_all, g_c=g_c, r_t=r_t):
            hs = slice(h * HEAD_DIM, (h + 1) * HEAD_DIM)
            qh = q_all[rows, hs]
            kh = k_all[rows, hs]
            vh = jnp.concatenate([v_all[rows, hs].astype(BF16), ones_col], axis=1)
            qhb = qh.astype(BF16)
            bc = b_all[:, N_HEADS + h:N_HEADS + h + 1]
            igc = g_c[:, h:h + 1]
            r_row = r_t[h:h + 1, :]
            c_prev = c_st[h]
            m_prev = m_st[h][:, 0:1]
            qk = _dot_nt(qhb, kh.astype(BF16))
            qc = _dot(qhb, c_prev.astype(BF16))
            yield

            log_d = jnp.where(causal, bc + r_row, -jnp.inf)
            m_intra = jnp.max(log_d, axis=1, keepdims=True)
            log_inter = bc + m_prev
            m_t = jnp.maximum(m_intra, log_inter)
            p = jnp.exp(log_d - m_t) * qk
            inter = jnp.exp(log_inter - m_t)
            b_last = bc[lc - 1:lc, :]
            w_state = b_last - bc + igc
            m_loc = jnp.max(w_state, axis=0, keepdims=True)
            ka = kh * jnp.exp(w_state - m_loc)
            yield

            nd = _dot(p.astype(BF16), vh) + inter * qc
            c_loc = _dot(ka.T.astype(BF16), vh)
            yield

            den = nd[:, HEAD_DIM:HEAD_DIM + 1]
            hh = nd[:, :HEAD_DIM] * (1.0 / jnp.maximum(jnp.abs(den), jnp.exp(-m_t)))
            m_new = jnp.maximum(b_last + m_prev, m_loc)
            s_old = jnp.exp(b_last + m_prev - m_new)
            s_new = jnp.exp(m_loc - m_new)
            c_st[h] = s_old * c_prev + s_new * c_loc
            m_st[h] = jnp.broadcast_to(m_new, (1, LANES))

            mu = jnp.mean(hh, axis=1, keepdims=True)
            dlt = hh - mu
            var = jnp.mean(dlt * dlt, axis=1, keepdims=True)
            hn = dlt * lax.rsqrt(var + LN_EPS) * hng_ref[:, hs]
            head_out[h].append(hn * _sigmoid(o_all[rows, hs]))
            yield

        for h0 in range(0, N_HEADS, HEADS_TOGETHER):
            for _ in zip(*[head(h) for h in range(h0, h0 + HEADS_TOGETHER)]):
                yield

    mixed = [jnp.concatenate(ho, axis=0) if len(ho) > 1 else ho[0] for ho in head_out]

    ue = jnp.concatenate([u_carry[...], u], axis=0)
    u_carry[...] = u[ts - POOL_CARRY:ts, :]
    pos = (lax.broadcasted_iota(I32, (ts, 1), 0) + s * ts + 1).astype(F32)
    for g, w in enumerate(POOL_WINDOWS):
        cs = slice(g * POOL_GROUP, (g + 1) * POOL_GROUP)
        win = ue[:, cs]
        shift = 1
        while shift < w:
            win = win + pltpu.roll(win, shift, 0)
            shift *= 2
        ug = u[:, cs]
        pooled = win[POOL_CARRY:, :] / jnp.minimum(pos, float(w)) - ug
        pm = _dot(pooled.astype(BF16), poolw_ref[g]) * pscale_ref[:, cs]
        mixed.append(pm)
    yield

    mixed = jnp.concatenate(mixed, axis=1).astype(BF16)
    y = _dot(mixed, wout_ref[...])
    o_ref[lane] = _layer_norm(ALPHA * x + y, lng_ref[...], lnb_ref[...])
    yield


def _mixer(x, weights, combine=None):
    bsz, seq, d = x.shape
    ts = min(SEQ_TILE, seq)
    lc = min(MLSTM_CHUNK, ts)
    ns = seq // ts
    mw = N_HEADS * HEAD_DIM
    pw = weights[1].shape[1]
    const = lambda a: pl.BlockSpec(a.shape, lambda b, s, *_: (0,) * a.ndim)
    in_specs = [pl.BlockSpec((MIXER_SEQS, ts, d), lambda b, s, *_: (b, s, 0))]
    args = [x]
    scratch = [
        pltpu.VMEM((MIXER_SEQS, ts + CONV_CARRY, 2 * mw), F32),
        pltpu.VMEM((MIXER_SEQS, POOL_CARRY, pw), F32),
        pltpu.VMEM((MIXER_SEQS, N_HEADS, HEAD_DIM, 2 * HEAD_DIM), F32),
        pltpu.VMEM((MIXER_SEQS, N_HEADS, 1, LANES), F32),
    ]
    prefetch = []
    if combine is not None:
        meta, tail, ys, cg, cb = combine
        prefetch = [meta]
        in_specs += [pl.BlockSpec((MIXER_SEQS, ts, TAIL_LANES), lambda b, s, *_: (b, s, 0)),
                     pl.BlockSpec(memory_space=pl.ANY), const(cg), const(cb)]
        args += [tail, ys, cg, cb]
        scratch += _combine_scratch(d, MIXER_SEQS)
    in_specs += [const(w) for w in weights]
    args += list(weights)
    grid_spec = pltpu.PrefetchScalarGridSpec(
        num_scalar_prefetch=len(prefetch),
        grid=(bsz // MIXER_SEQS, ns),
        in_specs=in_specs,
        out_specs=pl.BlockSpec((MIXER_SEQS, ts, d), lambda b, s, *_: (b, s, 0)),
        scratch_shapes=scratch,
    )
    return pl.pallas_call(
        functools.partial(_mixer_kernel, ts=ts, lc=lc, combine=combine is not None),
        out_shape=jax.ShapeDtypeStruct((bsz, seq, d), F32),
        grid_spec=grid_spec,
        compiler_params=pltpu.CompilerParams(
            dimension_semantics=("arbitrary", "arbitrary"), vmem_limit_bytes=VMEM_LIMIT),
        name="mixer",
    )(*prefetch, *args)


def _top2_sum(a, b, c, d):
    hi1, lo1 = jnp.maximum(a, b), jnp.minimum(a, b)
    hi2, lo2 = jnp.maximum(c, d), jnp.minimum(c, d)
    return jnp.maximum(hi1, hi2) + jnp.maximum(jnp.minimum(hi1, hi2), jnp.maximum(lo1, lo2))


def _xattn_kernel(x_ref, mem_ref, wq_ref, wkv_ref, wo_ref, lng_ref, lnb_ref,
                  rw2_ref, rbias_ref,
                  x2_ref, tail_ref, meta_ref, xs_hbm, tls_hbm,
                  k_scr, v_scr, carry, stx, stt, mvec, msm, prev_total, zx, zt, sem, ssem, zsem,
                  *, ts, cap, blk):
    b = pl.program_id(0)
    s = pl.program_id(1)
    step = b * pl.num_programs(1) + s
    last = step == pl.num_programs(0) * pl.num_programs(1) - 1
    d = x_ref.shape[2]

    @pl.when(s == 0)
    def _():
        for lane in range(XATTN_SEQS):
            kv = _dot(mem_ref[lane].astype(BF16), wkv_ref[...])
            k_scr[lane] = kv[:, :d].astype(BF16)
            v_scr[lane] = kv[:, d:].astype(BF16)

    @pl.when(step == 0)
    def _():
        carry[...] = jnp.zeros(carry.shape, F32)

    results = [None] * XATTN_SEQS
    lanes = [_xattn_lane(lane, results, x_ref, wq_ref, wo_ref, lng_ref, lnb_ref, rw2_ref, rbias_ref,
                         x2_ref, tail_ref, k_scr, v_scr, ts) for lane in range(XATTN_SEQS)]
    for _ in zip(*lanes):
        pass

    sub1 = lax.broadcasted_iota(I32, (SUBLANES, 1), 0)
    lane8 = lax.broadcasted_iota(I32, (SUBLANES, LANES), 1)
    base = carry[:, 0:1]
    for lane in range(XATTN_SEQS):
        len8, off8, total, _, _ = results[lane]
        new_base = base + len8
        mv = jnp.where(lane8 == M_SLOT, sub1.astype(F32) * float(cap) + base, 0.0)
        mv = jnp.where(lane8 == M_LEN, len8, mv)
        mv = jnp.where(lane8 == M_OFF, off8, mv)
        mv = jnp.where(lane8 == M_TOTAL, total, mv)
        mv = jnp.where(lane8 == M_END, new_base, mv).astype(I32)
        meta_ref[lane, 0] = mv
        mvec[lane * SUBLANES:(lane + 1) * SUBLANES, :] = mv
        base = new_base
    carry[...] = jnp.broadcast_to(base, carry.shape)

    def wait_runs(lane, n_rows):
        pltpu.make_async_copy(stx.at[lane, pl.ds(0, n_rows)], xs_hbm.at[pl.ds(0, n_rows)],
                              sem.at[lane, 0]).wait()
        pltpu.make_async_copy(stt.at[lane, pl.ds(0, n_rows)], tls_hbm.at[pl.ds(0, n_rows)],
                              sem.at[lane, 1]).wait()

    @pl.when(step > 0)
    def _():
        for lane in range(XATTN_SEQS):
            n_prev = _aligned(prev_total[lane])

            @pl.when(n_prev > 0)
            def _(lane=lane, n_prev=n_prev):
                wait_runs(lane, n_prev)

    for lane in range(XATTN_SEQS):
        stx[lane] = results[lane][3]
        stt[lane] = results[lane][4]
    to_smem = pltpu.make_async_copy(mvec, msm, ssem)
    to_smem.start()
    to_smem.wait()

    for lane in range(XATTN_SEQS):
        for g in range(N_GROUPS):
            r = lane * SUBLANES + g
            _run_copy(stx.at[lane], msm[r, M_OFF], xs_hbm, msm[r, M_SLOT], msm[r, M_LEN], sem.at[lane, 0])
            _run_copy(stt.at[lane], msm[r, M_OFF], tls_hbm, msm[r, M_SLOT], msm[r, M_LEN], sem.at[lane, 1])
        prev_total[lane] = msm[lane * SUBLANES, M_TOTAL]

    @pl.when(last)
    def _():
        for lane in range(XATTN_SEQS):
            n_own = _aligned(msm[lane * SUBLANES, M_TOTAL])

            @pl.when(n_own > 0)
            def _(lane=lane, n_own=n_own):
                wait_runs(lane, n_own)

        zx[...] = jnp.zeros(zx.shape, BF16)
        zt[...] = jnp.zeros(zt.shape, F32)
        for g in range(N_GROUPS):
            end = msm[(XATTN_SEQS - 1) * SUBLANES + g, M_END]
            n_pad = _aligned((blk - end % blk) % blk)
            _run_copy(zx, 0, xs_hbm, g * cap + end, n_pad, zsem.at[0])
            _run_copy(zt, 0, tls_hbm, g * cap + end, n_pad, zsem.at[1])

            @pl.when(n_pad > 0)
            def _(n_pad=n_pad):
                pltpu.make_async_copy(zx.at[pl.ds(0, n_pad)], xs_hbm.at[pl.ds(0, n_pad)], zsem.at[0]).wait()
                pltpu.make_async_copy(zt.at[pl.ds(0, n_pad)], tls_hbm.at[pl.ds(0, n_pad)], zsem.at[1]).wait()


def _xattn_lane(lane, results, x_ref, wq_ref, wo_ref, lng_ref, lnb_ref, rw2_ref, rbias_ref,
                x2_ref, tail_ref, k_scr, v_scr, ts):
    d = x_ref.shape[2]
    dh = d // XATTN_HEADS
    x = x_ref[lane]
    q = (_dot(x.astype(BF16), wq_ref[...]) * (dh ** -0.5)).astype(BF16)
    yield
    outs = []
    for h in range(XATTN_HEADS):
        hs = slice(h * dh, (h + 1) * dh)
        sc = _dot_nt(q[:, hs], k_scr[lane, :, hs])
        e = jnp.exp(sc - jnp.max(sc, axis=1, keepdims=True))
        l = jnp.sum(e, axis=1, keepdims=True)
        outs.append(_dot(e.astype(BF16), v_scr[lane, :, hs]) * (1.0 / l))
        yield
    o = jnp.concatenate(outs, axis=1).astype(BF16)
    x2 = _layer_norm(ALPHA * x + _dot(o, wo_ref[...]), lng_ref[...], lnb_ref[...])
    x2_ref[lane] = x2
    yield

    xh, xm, _ = _split3(x2)
    both = _dot(xh, rw2_ref[...])
    logits = (both[:, :LANES] + both[:, LANES:]) + _dot(xm, rw2_ref[:, :LANES])
    yield
    lt = logits.T[0:N_EXPERTS, :]
    score = _sigmoid(lt)
    sel = score + rbias_ref[...]

    sel_r = [sel[e:e + 1, :] for e in range(N_EXPERTS)]
    score_r = [score[e:e + 1, :] for e in range(N_EXPERTS)]
    gs = [_top2_sum(*sel_r[EXPERTS_PER_GROUP * g:EXPERTS_PER_GROUP * (g + 1)]) for g in range(N_GROUPS)]
    best = jnp.zeros((1, ts), I32)
    bestv = gs[0]
    for g in range(1, N_GROUPS):
        better = gs[g] > bestv
        best = jnp.where(better, g, best)
        bestv = jnp.where(better, gs[g], bestv)
    in_g = [best == g for g in range(N_GROUPS)]

    def pick(rows, j):
        out = rows[j]
        for g in range(1, N_GROUPS):
            out = jnp.where(in_g[g], rows[EXPERTS_PER_GROUP * g + j], out)
        return out

    vsel = [pick(sel_r, j) for j in range(EXPERTS_PER_GROUP)]
    vsc = [pick(score_r, j) for j in range(EXPERTS_PER_GROUP)]
    gates = []
    for j in range(EXPERTS_PER_GROUP):
        beaten = jnp.zeros((1, ts), I32)
        for k in range(EXPERTS_PER_GROUP):
            if k == j:
                continue
            wins = (vsel[k] > vsel[j]) | ((vsel[k] == vsel[j]) & (k < j))
            beaten = beaten + wins.astype(I32)
        gates.append(jnp.where(beaten < 2, vsc[j], 0.0))
    gsum = gates[0] + gates[1] + gates[2] + gates[3]
    gates = [g / gsum for g in gates]

    sub = lax.broadcasted_iota(I32, (SUBLANES, ts), 0)
    oh8 = jnp.zeros((SUBLANES, ts), F32)
    for g in range(N_GROUPS):
        oh8 = jnp.where((sub == g) & in_g[g], 1.0, oh8)
    r_i = lax.broadcasted_iota(I32, (ts, ts), 0)
    c_i = lax.broadcasted_iota(I32, (ts, ts), 1)
    upper = jnp.where(r_i < c_i, 1.0, 0.0).astype(BF16)
    excl = _dot(oh8.astype(BF16), upper)
    n8 = jnp.sum(oh8, axis=1, keepdims=True)
    len8 = jnp.floor((n8 + (RUN_ALIGN - 1)) * (1.0 / RUN_ALIGN)) * RUN_ALIGN
    sub1 = lax.broadcasted_iota(I32, (SUBLANES, 1), 0)
    off8 = jnp.zeros((SUBLANES, 1), F32)
    run_off = jnp.zeros((1, 1), F32)
    for g in range(N_GROUPS):
        off8 = jnp.where(sub1 == g, run_off, off8)
        run_off = run_off + len8[g:g + 1, :]
    pos = jnp.sum(jnp.where(oh8 > 0.0, off8 + excl, 0.0), axis=0, keepdims=True)

    t8 = jnp.where(sub == STAGE_LANE, pos, 0.0)
    for j in range(EXPERTS_PER_GROUP):
        t8 = jnp.where(sub == j, gates[j], t8)
    tail = jnp.concatenate([t8, jnp.zeros((TAIL_LANES - SUBLANES, ts), F32)], axis=0).T
    tail_ref[lane] = tail
    yield

    srow = lax.broadcasted_iota(I32, (STAGE_ROWS, ts), 0)
    sort = jnp.where(srow == pos.astype(I32), 1.0, 0.0).astype(BF16)
    xs_sorted = _dot(sort, xh).astype(BF16)
    pieces = _dot(sort, jnp.concatenate(_split3(tail), axis=1))
    tail_sorted = (pieces[:, :LANES] + pieces[:, LANES:2 * LANES]) + pieces[:, 2 * LANES:]
    results[lane] = (len8, off8, run_off, xs_sorted, tail_sorted)
    yield


def _xattn_router(x, mem, wq, wkv, wo, lng, lnb, rw2, rbias, cap, blk):
    bsz, seq, d = x.shape
    mlen = mem.shape[1]
    ts = min(SEQ_TILE, seq)
    assert ts == SEQ_TILE and blk % RUN_ALIGN == 0
    ns = seq // ts
    nl = XATTN_SEQS
    const = lambda a: pl.BlockSpec(a.shape, lambda b, s: (0,) * a.ndim)
    zero_rows = blk
    return pl.pallas_call(
        functools.partial(_xattn_kernel, ts=ts, cap=cap, blk=blk),
        out_shape=(
            jax.ShapeDtypeStruct((bsz, seq, d), F32),
            jax.ShapeDtypeStruct((bsz, seq, TAIL_LANES), F32),
            jax.ShapeDtypeStruct((bsz, ns, SUBLANES, LANES), I32),
            jax.ShapeDtypeStruct((N_GROUPS * cap, d), BF16),
            jax.ShapeDtypeStruct((N_GROUPS * cap, TAIL_LANES), F32),
        ),
        grid=(bsz // nl, ns),
        in_specs=[
            pl.BlockSpec((nl, ts, d), lambda b, s: (b, s, 0)),
            pl.BlockSpec((nl, mlen, d), lambda b, s: (b, 0, 0)),
            const(wq), const(wkv), const(wo), const(lng), const(lnb),
            const(rw2), const(rbias),
        ],
        out_specs=(
            pl.BlockSpec((nl, ts, d), lambda b, s: (b, s, 0)),
            pl.BlockSpec((nl, ts, TAIL_LANES), lambda b, s: (b, s, 0)),
            pl.BlockSpec((nl, 1, SUBLANES, LANES), lambda b, s: (b, s, 0, 0)),
            pl.BlockSpec(memory_space=pl.ANY),
            pl.BlockSpec(memory_space=pl.ANY),
        ),
        scratch_shapes=[
            pltpu.VMEM((nl, mlen, d), BF16),
            pltpu.VMEM((nl, mlen, d), BF16),
            pltpu.VMEM((SUBLANES, LANES), F32),
            pltpu.VMEM((nl, STAGE_ROWS, d), BF16),
            pltpu.VMEM((nl, STAGE_ROWS, TAIL_LANES), F32),
            pltpu.VMEM((nl * SUBLANES, LANES), I32),
            pltpu.SMEM((nl * SUBLANES, LANES), I32),
            pltpu.SMEM((nl,), I32),
            pltpu.VMEM((zero_rows, d), BF16),
            pltpu.VMEM((zero_rows, TAIL_LANES), F32),
            pltpu.SemaphoreType.DMA((nl, 2)),
            pltpu.SemaphoreType.DMA,
            pltpu.SemaphoreType.DMA((2,)),
        ],
        compiler_params=pltpu.CompilerParams(
            dimension_semantics=("arbitrary", "arbitrary"), vmem_limit_bytes=VMEM_LIMIT,
            has_side_effects=True),
        name="xattn_router",
    )(x, mem, wq, wkv, wo, lng, lnb, rw2, rbias)


def _ffn_kernel(blk_in_ref, grp_ref, used_ref, xs_ref, tl_ref, wg_ref, wu_ref, wd_ref, o_ref):
    @pl.when(used_ref[pl.program_id(0)] == 1)
    def _():
        xb = xs_ref[...]
        parts = []
        for j in range(EXPERTS_PER_GROUP):
            hg = _dot(xb, wg_ref[j])
            hu = _dot(xb, wu_ref[j])
            gate = tl_ref[:, j:j + 1]
            parts.append(jnp.where(gate != 0.0, hg * _sigmoid(hg) * hu * gate, 0.0))
        hid = jnp.concatenate(parts, axis=1).astype(BF16)
        o_ref[...] = _dot(hid, wd_ref[0]).astype(BF16)


def _ffn(blk_in, blk_grp, used, xs, tls, wg, wu, wd):
    d = xs.shape[1]
    blk = FFN_BLOCK
    grid_spec = pltpu.PrefetchScalarGridSpec(
        num_scalar_prefetch=3,
        grid=(blk_in.shape[0],),
        in_specs=[
            pl.BlockSpec((blk, d), lambda i, bi, grp, us: (bi[i], 0)),
            pl.BlockSpec((blk, TAIL_LANES), lambda i, bi, grp, us: (bi[i], 0)),
            pl.BlockSpec((EXPERTS_PER_GROUP,) + wg.shape[1:], lambda i, bi, grp, us: (grp[i], 0, 0)),
            pl.BlockSpec((EXPERTS_PER_GROUP,) + wu.shape[1:], lambda i, bi, grp, us: (grp[i], 0, 0)),
            pl.BlockSpec((1,) + wd.shape[1:], lambda i, bi, grp, us: (grp[i], 0, 0)),
        ],
        out_specs=pl.BlockSpec((blk, d), lambda i, bi, grp, us: (bi[i], 0)),
    )
    return pl.pallas_call(
        _ffn_kernel,
        out_shape=jax.ShapeDtypeStruct(xs.shape, BF16),
        grid_spec=grid_spec,
        compiler_params=pltpu.CompilerParams(
            dimension_semantics=("arbitrary",), vmem_limit_bytes=VMEM_LIMIT),
        name="group_ffn",
    )(blk_in, blk_grp, used, xs, tls, wg, wu, wd)


def _block_tables(seg_rows, cap, blk, n_steps):
    nblk = (seg_rows + blk - 1) // blk
    bend = jnp.cumsum(nblk)
    bstart = bend - nblk
    step = jnp.arange(n_steps, dtype=I32)
    used = step < bend[-1]
    grp = jnp.minimum(jnp.sum(step[:, None] >= bend[None, :], axis=1), N_GROUPS - 1).astype(I32)
    blk_in = grp * (cap // blk) + step - bstart[grp]
    last_real = jnp.maximum(bend[-1] - 1, 0)
    blk_in = jnp.where(used, blk_in, blk_in[last_real])
    grp = jnp.where(used, grp, grp[last_real])
    return blk_in.astype(I32), grp.astype(I32), used.astype(I32)


def _flat_meta(meta):
    m = meta.reshape(-1, SUBLANES, LANES)[:, :N_GROUPS, :]
    rec = jnp.concatenate([m[:, :, M_SLOT], m[:, :, M_LEN], m[:, :, M_OFF], m[:, :1, M_TOTAL],
                           jnp.zeros((m.shape[0], META_W - 3 * N_GROUPS - 1), I32)], axis=1)
    return rec.reshape(-1)


def kernel(x, mem, w_in, b_i, b_f, conv_qk, head_norm_g, pool_w, pool_scale, w_mix_out,
           ln_mix_g, ln_mix_b, w_xq, w_xkv, w_xo, ln_x_g, ln_x_b, router_w, router_bias,
           w_gate, w_up, w_down, ln_moe_g, ln_moe_b):
    bsz, seq, d = x.shape
    n_tok = bsz * seq
    n_tiles = n_tok // SEQ_TILE
    mw = N_HEADS * HEAD_DIM
    n_gate = 2 * N_HEADS
    blk = FFN_BLOCK
    cap = -(-(n_tok + RUN_ALIGN * n_tiles) // blk) * blk
    n_steps = (n_tok + N_GROUPS * RUN_ALIGN * n_tiles) // blk + N_GROUPS

    rw = jnp.pad(router_w, ((0, 0), (0, LANES - N_EXPERTS)))
    rwh, rwm, _ = _split3(rw)
    rw2 = jnp.concatenate([rwh, rwm], axis=1)
    rbias = router_bias.reshape(N_EXPERTS, 1).astype(F32)
    row = lambda v: v.reshape(1, -1).astype(F32)

    combine = None
    for l in range(DEPTH):
        wa = w_in[l][:, :4 * mw].astype(BF16)
        wu = w_in[l][:, 4 * mw + n_gate:].astype(BF16)
        wif = jnp.pad(w_in[l][:, 4 * mw:4 * mw + n_gate], ((0, 0), (0, LANES - n_gate))).astype(BF16)
        bif = jnp.pad(jnp.concatenate([b_i[l], b_f[l]]), (0, LANES - n_gate)).reshape(1, LANES)
        weights = (wa, wu, wif, bif, conv_qk[l], row(head_norm_g[l]), pool_w[l].astype(BF16),
                   row(pool_scale[l]), w_mix_out[l].astype(BF16), row(ln_mix_g[l]), row(ln_mix_b[l]))
        x = _mixer(x, weights, combine)

        x2, tail, meta, xs, tls = _xattn_router(
            x, mem, w_xq[l].astype(BF16), w_xkv[l].astype(BF16), w_xo[l].astype(BF16),
            row(ln_x_g[l]), row(ln_x_b[l]), rw2, rbias, cap, blk)

        seg_rows = meta[-1, -1, :N_GROUPS, M_END]
        blk_in, blk_grp, used = _block_tables(seg_rows, cap, blk, n_steps)
        ys = _ffn(blk_in, blk_grp, used, xs, tls, w_gate[l].astype(BF16), w_up[l].astype(BF16),
                  w_down[l].reshape(N_GROUPS, EXPERTS_PER_GROUP * w_down.shape[2], d).astype(BF16))
        x = x2
        combine = (_flat_meta(meta), tail, ys, row(ln_moe_g[l]), row(ln_moe_b[l]))

    meta, tail, ys, cg, cb = combine
    return _final_combine(meta, x.reshape(n_tok, d), tail.reshape(n_tok, TAIL_LANES), ys, cg, cb).reshape(bsz, seq, d)
```

```python
import functools

import jax
import jax.numpy as jnp
from jax import lax
from jax.experimental import pallas as pl
from jax.experimental.pallas import tpu as pltpu

F32 = jnp.float32
BF16 = jnp.bfloat16
I32 = jnp.int32

N_HEADS = 4
HEAD_DIM = 128
POOL_WINDOWS = (2, 4, 8, 16)
POOL_GROUP = 128
CONV_WIDTH = 4
XATTN_HEADS = 4
N_EXPERTS = 16
N_GROUPS = 4
EXPERTS_PER_GROUP = 4
DEPTH = 2
ALPHA = (2 * DEPTH) ** 0.25
LN_EPS = 1e-5

LANES = 128
SUBLANES = 8
BF16_TILE_ROWS = 16
VMEM_LIMIT = 56 * 1024 * 1024

SEQ_TILE = 256
MIXER_SEQS = 2
XATTN_SEQS = 2
FINAL_TILES = 2
HEADS_TOGETHER = 2
MLSTM_CHUNK = 256
FFN_BLOCK = 512
CONV_CARRY = 8
POOL_CARRY = 16

RUN_ALIGN = BF16_TILE_ROWS
STAGE_ROWS = SEQ_TILE + N_GROUPS * RUN_ALIGN
STAGE_ROWS_PADDED = 384
TAIL_LANES = 128
STAGE_LANE = 5
M_SLOT, M_LEN, M_OFF, M_TOTAL, M_END = 0, 1, 2, 3, 4
META_W = 16


def _layer_norm(y, g, b):
    mu = jnp.mean(y, axis=-1, keepdims=True)
    d = y - mu
    var = jnp.mean(d * d, axis=-1, keepdims=True)
    return d * lax.rsqrt(var + LN_EPS) * g + b


def _sigmoid(v):
    return 1.0 / (1.0 + jnp.exp(-v))


def _dot(a, b):
    return jnp.dot(a, b, preferred_element_type=F32)


def _dot_nt(a, b):
    return lax.dot_general(a, b, (((1,), (1,)), ((), ())), preferred_element_type=F32)


def _split3(v):
    hi = v.astype(BF16)
    r1 = v - hi.astype(F32)
    mid = r1.astype(BF16)
    lo = (r1 - mid.astype(F32)).astype(BF16)
    return hi, mid, lo


def _loop(n, body, unroll=1):
    lax.fori_loop(0, n, lambda j, c: (body(j), c)[1], 0, unroll=unroll)


def _aligned(v):
    return pl.multiple_of(v, RUN_ALIGN)


def _run_copy(src, src_row, dst, dst_row, length, sem):
    length = _aligned(length)

    @pl.when(length > 0)
    def _():
        pltpu.make_async_copy(src.at[pl.ds(_aligned(src_row), length)],
                              dst.at[pl.ds(_aligned(dst_row), length)], sem).start()


def _combine_fetch(meta_ref, tile, next_tile, first, has_next, slot, ys_hbm, stage, sem):
    def fetch(t, sl):
        for g in range(N_GROUPS):
            _run_copy(ys_hbm, meta_ref[t * META_W + g],
                      stage.at[sl], meta_ref[t * META_W + 2 * N_GROUPS + g],
                      meta_ref[t * META_W + N_GROUPS + g], sem.at[sl])

    @pl.when(first)
    def _():
        stage[...] = jnp.zeros(stage.shape, stage.dtype)
        fetch(tile, slot)

    @pl.when(has_next)
    def _():
        fetch(next_tile, 1 - slot)

    total = _aligned(meta_ref[tile * META_W + 3 * N_GROUPS])

    @pl.when(total > 0)
    def _():
        pltpu.make_async_copy(ys_hbm.at[pl.ds(0, total)], stage.at[slot, pl.ds(0, total)],
                              sem.at[slot]).wait()


def _combine(x2, tail, sorted_rows, lng, lnb):
    ts = x2.shape[0]
    pos = tail[:, STAGE_LANE:STAGE_LANE + 1].astype(I32)
    lane = lax.broadcasted_iota(I32, (ts, STAGE_ROWS_PADDED), 1)
    unsort = jnp.where(lane == pos, 1.0, 0.0).astype(BF16)
    return _layer_norm(ALPHA * x2 + _dot(unsort, sorted_rows), lng, lnb)


def _combine_scratch(d, lanes):
    return [pltpu.VMEM((lanes, 2, STAGE_ROWS_PADDED, d), BF16), pltpu.SemaphoreType.DMA((lanes, 2))]


def _final_kernel(meta_ref, x2_ref, tail_ref, ys_hbm, lng_ref, lnb_ref, o_ref, stage, sem):
    i = pl.program_id(0)
    ts = o_ref.shape[0] // FINAL_TILES
    for lane in range(FINAL_TILES):
        tile = i * FINAL_TILES + lane
        _combine_fetch(meta_ref, tile, tile + FINAL_TILES, i == 0, i + 1 < pl.num_programs(0), i % 2,
                       ys_hbm, stage.at[lane], sem.at[lane])
    for lane in range(FINAL_TILES):
        rows = slice(lane * ts, (lane + 1) * ts)
        o_ref[rows, :] = _combine(x2_ref[rows, :], tail_ref[rows, :], stage[lane, i % 2],
                                  lng_ref[...], lnb_ref[...])


def _final_combine(meta, x2, tail, ys, lng, lnb):
    n_tok, d = x2.shape
    ts = FINAL_TILES * (tail.shape[0] // (meta.shape[0] // META_W))
    grid_spec = pltpu.PrefetchScalarGridSpec(
        num_scalar_prefetch=1,
        grid=(n_tok // ts,),
        in_specs=[
            pl.BlockSpec((ts, d), lambda i, m: (i, 0)),
            pl.BlockSpec((ts, TAIL_LANES), lambda i, m: (i, 0)),
            pl.BlockSpec(memory_space=pl.ANY),
            pl.BlockSpec(lng.shape, lambda i, m: (0, 0)),
            pl.BlockSpec(lnb.shape, lambda i, m: (0, 0)),
        ],
        out_specs=pl.BlockSpec((ts, d), lambda i, m: (i, 0)),
        scratch_shapes=_combine_scratch(d, FINAL_TILES),
    )
    return pl.pallas_call(
        _final_kernel,
        out_shape=jax.ShapeDtypeStruct((n_tok, d), F32),
        grid_spec=grid_spec,
        compiler_params=pltpu.CompilerParams(
            dimension_semantics=("arbitrary",), vmem_limit_bytes=VMEM_LIMIT),
        name="final_combine",
    )(meta, x2, tail, ys, lng, lnb)


def _mixer_kernel(*refs, ts, lc, combine):
    n_lead = 6 if combine else 1
    zq_ext, u_carry, c_st, m_st = refs[n_lead + 12:n_lead + 16]
    b = pl.program_id(0)
    s = pl.program_id(1)
    ns = pl.num_programs(1)

    @pl.when(s == 0)
    def _():
        zq_ext[...] = jnp.zeros(zq_ext.shape, F32)
        u_carry[...] = jnp.zeros(u_carry.shape, F32)
        c_st[...] = jnp.zeros(c_st.shape, F32)
        m_st[...] = jnp.zeros(m_st.shape, F32)

    step = b * ns + s
    if combine:
        meta_ref, ys_hbm = refs[0], refs[3]
        stage, sem = refs[n_lead + 16:n_lead + 18]
        for lane in range(MIXER_SEQS):
            tile = (b * MIXER_SEQS + lane) * ns + s
            next_tile = jnp.where(s + 1 < ns, tile + 1, tile + (MIXER_SEQS - 1) * ns + 1)
            _combine_fetch(meta_ref, tile, next_tile, step == 0, step + 1 < pl.num_programs(0) * ns,
                           step % 2, ys_hbm, stage.at[lane], sem.at[lane])

    lanes = [_mixer_lane(lane, step % 2, refs, ts, lc, combine) for lane in range(MIXER_SEQS)]
    for _ in zip(*lanes):
        pass


def _mixer_lane(lane, slot, refs, ts, lc, combine):
    if combine:
        (_, x_ref, tail_ref, _, cg_ref, cb_ref), refs = refs[:6], refs[6:]
    else:
        x_ref, refs = refs[0], refs[1:]
    (wa_ref, wu_ref, wif_ref, bif_ref, conv_ref, hng_ref, poolw_ref, pscale_ref, wout_ref,
     lng_ref, lnb_ref, o_ref, zq_ext, u_carry, c_st, m_st) = refs[:16]
    zq_ext, u_carry, c_st, m_st = zq_ext.at[lane], u_carry.at[lane], c_st.at[lane], m_st.at[lane]
    s = pl.program_id(1)
    mw = N_HEADS * HEAD_DIM

    x = x_ref[lane]
    if combine:
        stage = refs[16]
        x = _combine(x, tail_ref[lane], stage[lane, slot], cg_ref[...], cb_ref[...])
    xb = x.astype(BF16)
    z = _dot(xb, wa_ref[...])
    u = _dot(xb, wu_ref[...])
    gts = _dot(xb, wif_ref[...]) + bif_ref[...]
    yield

    zq = z[:, :2 * mw]
    ze = jnp.concatenate([zq_ext[...], zq], axis=0)
    zq_ext[...] = zq[ts - CONV_CARRY:ts, :]
    cw = conv_ref[...]
    acc = zq * cw[CONV_WIDTH - 1:CONV_WIDTH, :]
    for j in range(1, CONV_WIDTH):
        acc = acc + pltpu.roll(ze, j, 0)[CONV_CARRY:, :] * cw[CONV_WIDTH - 1 - j:CONV_WIDTH - j, :]
    qk = acc * _sigmoid(acc)
    q_all = qk[:, :mw] * (HEAD_DIM ** -0.5)
    k_all = qk[:, mw:]
    v_all = z[:, 2 * mw:3 * mw]
    o_all = z[:, 3 * mw:4 * mw]

    lf_all = jnp.minimum(gts, 0.0) - jnp.log1p(jnp.exp(-jnp.abs(gts)))
    yield

    row_i = lax.broadcasted_iota(I32, (lc, lc), 0)
    col_i = lax.broadcasted_iota(I32, (lc, lc), 1)
    causal = col_i <= row_i
    tri = jnp.where(causal, 1.0, 0.0).astype(BF16)
    ones_col = jnp.where(lax.broadcasted_iota(I32, (lc, HEAD_DIM), 1) == 0, 1.0, 0.0).astype(BF16)

    head_out = [[] for _ in range(N_HEADS)]
    for c in range(ts // lc):
        rows = slice(c * lc, (c + 1) * lc)
        hi, mid, lo = _split3(lf_all[rows, :])
        b_all = _dot(tri, hi) + _dot(tri, mid) + _dot(tri, lo)
        g_c = gts[rows, :]
        r_all = g_c - pltpu.roll(b_all, LANES - N_HEADS, 1)
        r_t = r_all.T
        def head(h, rows=rows, b_all=b_all, g_c=g_c, r_t=r_t):
            hs = slice(h * HEAD_DIM, (h + 1) * HEAD_DIM)
            qh = q_all[rows, hs]
            kh = k_all[rows, hs]
            vh = jnp.concatenate([v_all[rows, hs].astype(BF16), ones_col], axis=1)
            qhb = qh.astype(BF16)
            bc = b_all[:, N_HEADS + h:N_HEADS + h + 1]
            igc = g_c[:, h:h + 1]
            r_row = r_t[h:h + 1, :]
            c_prev = c_st[h]
            m_prev = m_st[h][:, 0:1]
            qk = _dot_nt(qhb, kh.astype(BF16))
            qc = _dot(qhb, c_prev.astype(BF16))
            yield

            log_d = jnp.where(causal, bc + r_row, -jnp.inf)
            m_intra = jnp.max(log_d, axis=1, keepdims=True)
            log_inter = bc + m_prev
            m_t = jnp.maximum(m_intra, log_inter)
            p = jnp.exp(log_d - m_t) * qk
            inter = jnp.exp(log_inter - m_t)
            b_last = bc[lc - 1:lc, :]
            w_state = b_last - bc + igc
            m_loc = jnp.max(w_state, axis=0, keepdims=True)
            ka = kh * jnp.exp(w_state - m_loc)
            yield

            nd = _dot(p.astype(BF16), vh) + inter * qc
            c_loc = _dot(ka.T.astype(BF16), vh)
            yield

            den = nd[:, HEAD_DIM:HEAD_DIM + 1]
            hh = nd[:, :HEAD_DIM] * (1.0 / jnp.maximum(jnp.abs(den), jnp.exp(-m_t)))
            m_new = jnp.maximum(b_last + m_prev, m_loc)
            s_old = jnp.exp(b_last + m_prev - m_new)
            s_new = jnp.exp(m_loc - m_new)
            c_st[h] = s_old * c_prev + s_new * c_loc
            m_st[h] = jnp.broadcast_to(m_new, (1, LANES))

            mu = jnp.mean(hh, axis=1, keepdims=True)
            dlt = hh - mu
            var = jnp.mean(dlt * dlt, axis=1, keepdims=True)
            hn = dlt * lax.rsqrt(var + LN_EPS) * hng_ref[:, hs]
            head_out[h].append(hn * _sigmoid(o_all[rows, hs]))
            yield

        for h0 in range(0, N_HEADS, HEADS_TOGETHER):
            for _ in zip(*[head(h) for h in range(h0, h0 + HEADS_TOGETHER)]):
                yield

    mixed = [jnp.concatenate(ho, axis=0) if len(ho) > 1 else ho[0] for ho in head_out]

    ue = jnp.concatenate([u_carry[...], u], axis=0)
    u_carry[...] = u[ts - POOL_CARRY:ts, :]
    pos = (lax.broadcasted_iota(I32, (ts, 1), 0) + s * ts + 1).astype(F32)
    for g, w in enumerate(POOL_WINDOWS):
        cs = slice(g * POOL_GROUP, (g + 1) * POOL_GROUP)
        win = ue[:, cs]
        shift = 1
        while shift < w:
            win = win + pltpu.roll(win, shift, 0)
            shift *= 2
        ug = u[:, cs]
        pooled = win[POOL_CARRY:, :] / jnp.minimum(pos, float(w)) - ug
        pm = _dot(pooled.astype(BF16), poolw_ref[g]) * pscale_ref[:, cs]
        mixed.append(pm)
    yield

    mixed = jnp.concatenate(mixed, axis=1).astype(BF16)
    y = _dot(mixed, wout_ref[...])
    o_ref[lane] = _layer_norm(ALPHA * x + y, lng_ref[...], lnb_ref[...])
    yield


def _mixer(x, weights, combine=None):
    bsz, seq, d = x.shape
    ts = min(SEQ_TILE, seq)
    lc = min(MLSTM_CHUNK, ts)
    ns = seq // ts
    mw = N_HEADS * HEAD_DIM
    pw = weights[1].shape[1]
    const = lambda a: pl.BlockSpec(a.shape, lambda b, s, *_: (0,) * a.ndim)
    in_specs = [pl.BlockSpec((MIXER_SEQS, ts, d), lambda b, s, *_: (b, s, 0))]
    args = [x]
    scratch = [
        pltpu.VMEM((MIXER_SEQS, CONV_CARRY, 2 * mw), F32),
        pltpu.VMEM((MIXER_SEQS, POOL_CARRY, pw), F32),
        pltpu.VMEM((MIXER_SEQS, N_HEADS, HEAD_DIM, 2 * HEAD_DIM), F32),
        pltpu.VMEM((MIXER_SEQS, N_HEADS, 1, LANES), F32),
    ]
    prefetch = []
    if combine is not None:
        meta, tail, ys, cg, cb = combine
        prefetch = [meta]
        in_specs += [pl.BlockSpec((MIXER_SEQS, ts, TAIL_LANES), lambda b, s, *_: (b, s, 0)),
                     pl.BlockSpec(memory_space=pl.ANY), const(cg), const(cb)]
        args += [tail, ys, cg, cb]
        scratch += _combine_scratch(d, MIXER_SEQS)
    in_specs += [const(w) for w in weights]
    args += list(weights)
    grid_spec = pltpu.PrefetchScalarGridSpec(
        num_scalar_prefetch=len(prefetch),
        grid=(bsz // MIXER_SEQS, ns),
        in_specs=in_specs,
        out_specs=pl.BlockSpec((MIXER_SEQS, ts, d), lambda b, s, *_: (b, s, 0)),
        scratch_shapes=scratch,
    )
    return pl.pallas_call(
        functools.partial(_mixer_kernel, ts=ts, lc=lc, combine=combine is not None),
        out_shape=jax.ShapeDtypeStruct((bsz, seq, d), F32),
        grid_spec=grid_spec,
        compiler_params=pltpu.CompilerParams(
            dimension_semantics=("arbitrary", "arbitrary"), vmem_limit_bytes=VMEM_LIMIT),
        name="mixer",
    )(*prefetch, *args)


def _top2_sum(a, b, c, d):
    hi1, lo1 = jnp.maximum(a, b), jnp.minimum(a, b)
    hi2, lo2 = jnp.maximum(c, d), jnp.minimum(c, d)
    return jnp.maximum(hi1, hi2) + jnp.maximum(jnp.minimum(hi1, hi2), jnp.maximum(lo1, lo2))


def _xattn_kernel(x_ref, mem_ref, wq_ref, wkv_ref, wo_ref, lng_ref, lnb_ref,
                  rw2_ref, rbias_ref,
                  x2_ref, tail_ref, meta_ref, xs_hbm, tls_hbm,
                  k_scr, v_scr, carry, stx, stt, mvec, msm, prev_total, zx, zt, sem, ssem, zsem,
                  *, ts, cap, blk):
    b = pl.program_id(0)
    s = pl.program_id(1)
    step = b * pl.num_programs(1) + s
    last = step == pl.num_programs(0) * pl.num_programs(1) - 1
    d = x_ref.shape[2]

    @pl.when(s == 0)
    def _():
        for lane in range(XATTN_SEQS):
            kv = _dot(mem_ref[lane].astype(BF16), wkv_ref[...])
            k_scr[lane] = kv[:, :d].astype(BF16)
            v_scr[lane] = kv[:, d:].astype(BF16)

    @pl.when(step == 0)
    def _():
        carry[...] = jnp.zeros(carry.shape, F32)

    results = [None] * XATTN_SEQS
    lanes = [_xattn_lane(lane, results, x_ref, wq_ref, wo_ref, lng_ref, lnb_ref, rw2_ref, rbias_ref,
                         x2_ref, tail_ref, k_scr, v_scr, ts) for lane in range(XATTN_SEQS)]
    for _ in zip(*lanes):
        pass

    sub1 = lax.broadcasted_iota(I32, (SUBLANES, 1), 0)
    lane8 = lax.broadcasted_iota(I32, (SUBLANES, LANES), 1)
    base = carry[:, 0:1]
    for lane in range(XATTN_SEQS):
        len8, off8, total, _, _ = results[lane]
        new_base = base + len8
        mv = jnp.where(lane8 == M_SLOT, sub1.astype(F32) * float(cap) + base, 0.0)
        mv = jnp.where(lane8 == M_LEN, len8, mv)
        mv = jnp.where(lane8 == M_OFF, off8, mv)
        mv = jnp.where(lane8 == M_TOTAL, total, mv)
        mv = jnp.where(lane8 == M_END, new_base, mv).astype(I32)
        meta_ref[lane, 0] = mv
        mvec[lane * SUBLANES:(lane + 1) * SUBLANES, :] = mv
        base = new_base
    carry[...] = jnp.broadcast_to(base, carry.shape)

    def wait_runs(lane, n_rows):
        pltpu.make_async_copy(stx.at[lane, pl.ds(0, n_rows)], xs_hbm.at[pl.ds(0, n_rows)],
                              sem.at[lane, 0]).wait()
        pltpu.make_async_copy(stt.at[lane, pl.ds(0, n_rows)], tls_hbm.at[pl.ds(0, n_rows)],
                              sem.at[lane, 1]).wait()

    @pl.when(step > 0)
    def _():
        for lane in range(XATTN_SEQS):
            n_prev = _aligned(prev_total[lane])

            @pl.when(n_prev > 0)
            def _(lane=lane, n_prev=n_prev):
                wait_runs(lane, n_prev)

    for lane in range(XATTN_SEQS):
        stx[lane] = results[lane][3]
        stt[lane] = results[lane][4]
    to_smem = pltpu.make_async_copy(mvec, msm, ssem)
    to_smem.start()
    to_smem.wait()

    for lane in range(XATTN_SEQS):
        for g in range(N_GROUPS):
            r = lane * SUBLANES + g
            _run_copy(stx.at[lane], msm[r, M_OFF], xs_hbm, msm[r, M_SLOT], msm[r, M_LEN], sem.at[lane, 0])
            _run_copy(stt.at[lane], msm[r, M_OFF], tls_hbm, msm[r, M_SLOT], msm[r, M_LEN], sem.at[lane, 1])
        prev_total[lane] = msm[lane * SUBLANES, M_TOTAL]

    @pl.when(last)
    def _():
        for lane in range(XATTN_SEQS):
            n_own = _aligned(msm[lane * SUBLANES, M_TOTAL])

            @pl.when(n_own > 0)
            def _(lane=lane, n_own=n_own):
                wait_runs(lane, n_own)

        zx[...] = jnp.zeros(zx.shape, BF16)
        zt[...] = jnp.zeros(zt.shape, F32)
        for g in range(N_GROUPS):
            end = msm[(XATTN_SEQS - 1) * SUBLANES + g, M_END]
            n_pad = _aligned((blk - end % blk) % blk)
            _run_copy(zx, 0, xs_hbm, g * cap + end, n_pad, zsem.at[0])
            _run_copy(zt, 0, tls_hbm, g * cap + end, n_pad, zsem.at[1])

            @pl.when(n_pad > 0)
            def _(n_pad=n_pad):
                pltpu.make_async_copy(zx.at[pl.ds(0, n_pad)], xs_hbm.at[pl.ds(0, n_pad)], zsem.at[0]).wait()
                pltpu.make_async_copy(zt.at[pl.ds(0, n_pad)], tls_hbm.at[pl.ds(0, n_pad)], zsem.at[1]).wait()


def _xattn_lane(lane, results, x_ref, wq_ref, wo_ref, lng_ref, lnb_ref, rw2_ref, rbias_ref,
                x2_ref, tail_ref, k_scr, v_scr, ts):
    d = x_ref.shape[2]
    dh = d // XATTN_HEADS
    x = x_ref[lane]
    q = (_dot(x.astype(BF16), wq_ref[...]) * (dh ** -0.5)).astype(BF16)
    yield
    outs = []
    for h in range(XATTN_HEADS):
        hs = slice(h * dh, (h + 1) * dh)
        sc = _dot_nt(q[:, hs], k_scr[lane, :, hs])
        e = jnp.exp(sc - jnp.max(sc, axis=1, keepdims=True))
        l = jnp.sum(e, axis=1, keepdims=True)
        outs.append(_dot(e.astype(BF16), v_scr[lane, :, hs]) * (1.0 / l))
        yield
    o = jnp.concatenate(outs, axis=1).astype(BF16)
    x2 = _layer_norm(ALPHA * x + _dot(o, wo_ref[...]), lng_ref[...], lnb_ref[...])
    x2_ref[lane] = x2
    yield

    xh, xm, _ = _split3(x2)
    both = _dot(xh, rw2_ref[...])
    logits = (both[:, :LANES] + both[:, LANES:]) + _dot(xm, rw2_ref[:, :LANES])
    yield
    lt = logits.T[0:N_EXPERTS, :]
    score = _sigmoid(lt)
    sel = score + rbias_ref[...]

    sel_r = [sel[e:e + 1, :] for e in range(N_EXPERTS)]
    score_r = [score[e:e + 1, :] for e in range(N_EXPERTS)]
    gs = [_top2_sum(*sel_r[EXPERTS_PER_GROUP * g:EXPERTS_PER_GROUP * (g + 1)]) for g in range(N_GROUPS)]
    best = jnp.zeros((1, ts), I32)
    bestv = gs[0]
    for g in range(1, N_GROUPS):
        better = gs[g] > bestv
        best = jnp.where(better, g, best)
        bestv = jnp.where(better, gs[g], bestv)
    in_g = [best == g for g in range(N_GROUPS)]

    def pick(rows, j):
        out = rows[j]
        for g in range(1, N_GROUPS):
            out = jnp.where(in_g[g], rows[EXPERTS_PER_GROUP * g + j], out)
        return out

    vsel = [pick(sel_r, j) for j in range(EXPERTS_PER_GROUP)]
    vsc = [pick(score_r, j) for j in range(EXPERTS_PER_GROUP)]
    gates = []
    for j in range(EXPERTS_PER_GROUP):
        beaten = jnp.zeros((1, ts), I32)
        for k in range(EXPERTS_PER_GROUP):
            if k == j:
                continue
            wins = (vsel[k] > vsel[j]) | ((vsel[k] == vsel[j]) & (k < j))
            beaten = beaten + wins.astype(I32)
        gates.append(jnp.where(beaten < 2, vsc[j], 0.0))
    gsum = gates[0] + gates[1] + gates[2] + gates[3]
    gates = [g / gsum for g in gates]

    sub = lax.broadcasted_iota(I32, (SUBLANES, ts), 0)
    oh8 = jnp.zeros((SUBLANES, ts), F32)
    for g in range(N_GROUPS):
        oh8 = jnp.where((sub == g) & in_g[g], 1.0, oh8)
    r_i = lax.broadcasted_iota(I32, (ts, ts), 0)
    c_i = lax.broadcasted_iota(I32, (ts, ts), 1)
    upper = jnp.where(r_i < c_i, 1.0, 0.0).astype(BF16)
    excl = _dot(oh8.astype(BF16), upper)
    n8 = jnp.sum(oh8, axis=1, keepdims=True)
    len8 = jnp.floor((n8 + (RUN_ALIGN - 1)) * (1.0 / RUN_ALIGN)) * RUN_ALIGN
    sub1 = lax.broadcasted_iota(I32, (SUBLANES, 1), 0)
    off8 = jnp.zeros((SUBLANES, 1), F32)
    run_off = jnp.zeros((1, 1), F32)
    for g in range(N_GROUPS):
        off8 = jnp.where(sub1 == g, run_off, off8)
        run_off = run_off + len8[g:g + 1, :]
    pos = jnp.sum(jnp.where(oh8 > 0.0, off8 + excl, 0.0), axis=0, keepdims=True)

    t8 = jnp.where(sub == STAGE_LANE, pos, 0.0)
    for j in range(EXPERTS_PER_GROUP):
        t8 = jnp.where(sub == j, gates[j], t8)
    tail = jnp.concatenate([t8, jnp.zeros((TAIL_LANES - SUBLANES, ts), F32)], axis=0).T
    tail_ref[lane] = tail
    yield

    srow = lax.broadcasted_iota(I32, (STAGE_ROWS, ts), 0)
    sort = jnp.where(srow == pos.astype(I32), 1.0, 0.0).astype(BF16)
    xs_sorted = _dot(sort, xh).astype(BF16)
    pieces = _dot(sort, jnp.concatenate(_split3(tail), axis=1))
    tail_sorted = (pieces[:, :LANES] + pieces[:, LANES:2 * LANES]) + pieces[:, 2 * LANES:]
    results[lane] = (len8, off8, run_off, xs_sorted, tail_sorted)
    yield


def _xattn_router(x, mem, wq, wkv, wo, lng, lnb, rw2, rbias, cap, blk):
    bsz, seq, d = x.shape
    mlen = mem.shape[1]
    ts = min(SEQ_TILE, seq)
    assert ts == SEQ_TILE and blk % RUN_ALIGN == 0
    ns = seq // ts
    nl = XATTN_SEQS
    const = lambda a: pl.BlockSpec(a.shape, lambda b, s: (0,) * a.ndim)
    zero_rows = blk
    return pl.pallas_call(
        functools.partial(_xattn_kernel, ts=ts, cap=cap, blk=blk),
        out_shape=(
            jax.ShapeDtypeStruct((bsz, seq, d), F32),
            jax.ShapeDtypeStruct((bsz, seq, TAIL_LANES), F32),
            jax.ShapeDtypeStruct((bsz, ns, SUBLANES, LANES), I32),
            jax.ShapeDtypeStruct((N_GROUPS * cap, d), BF16),
            jax.ShapeDtypeStruct((N_GROUPS * cap, TAIL_LANES), F32),
        ),
        grid=(bsz // nl, ns),
        in_specs=[
            pl.BlockSpec((nl, ts, d), lambda b, s: (b, s, 0)),
            pl.BlockSpec((nl, mlen, d), lambda b, s: (b, 0, 0)),
            const(wq), const(wkv), const(wo), const(lng), const(lnb),
            const(rw2), const(rbias),
        ],
        out_specs=(
            pl.BlockSpec((nl, ts, d), lambda b, s: (b, s, 0)),
            pl.BlockSpec((nl, ts, TAIL_LANES), lambda b, s: (b, s, 0)),
            pl.BlockSpec((nl, 1, SUBLANES, LANES), lambda b, s: (b, s, 0, 0)),
            pl.BlockSpec(memory_space=pl.ANY),
            pl.BlockSpec(memory_space=pl.ANY),
        ),
        scratch_shapes=[
            pltpu.VMEM((nl, mlen, d), BF16),
            pltpu.VMEM((nl, mlen, d), BF16),
            pltpu.VMEM((SUBLANES, LANES), F32),
            pltpu.VMEM((nl, STAGE_ROWS, d), BF16),
            pltpu.VMEM((nl, STAGE_ROWS, TAIL_LANES), F32),
            pltpu.VMEM((nl * SUBLANES, LANES), I32),
            pltpu.SMEM((nl * SUBLANES, LANES), I32),
            pltpu.SMEM((nl,), I32),
            pltpu.VMEM((zero_rows, d), BF16),
            pltpu.VMEM((zero_rows, TAIL_LANES), F32),
            pltpu.SemaphoreType.DMA((nl, 2)),
            pltpu.SemaphoreType.DMA,
            pltpu.SemaphoreType.DMA((2,)),
        ],
        compiler_params=pltpu.CompilerParams(
            dimension_semantics=("arbitrary", "arbitrary"), vmem_limit_bytes=VMEM_LIMIT,
            has_side_effects=True),
        name="xattn_router",
    )(x, mem, wq, wkv, wo, lng, lnb, rw2, rbias)


def _ffn_kernel(blk_in_ref, grp_ref, used_ref, xs_ref, tl_ref, wg_ref, wu_ref, wd_ref, o_ref):
    @pl.when(used_ref[pl.program_id(0)] == 1)
    def _():
        xb = xs_ref[...]
        parts = []
        for j in range(EXPERTS_PER_GROUP):
            hg = _dot(xb, wg_ref[j])
            hu = _dot(xb, wu_ref[j])
            gate = tl_ref[:, j:j + 1]
            parts.append(jnp.where(gate != 0.0, hg * _sigmoid(hg) * hu * gate, 0.0))
        hid = jnp.concatenate(parts, axis=1).astype(BF16)
        o_ref[...] = _dot(hid, wd_ref[0]).astype(BF16)


def _ffn(blk_in, blk_grp, used, xs, tls, wg, wu, wd):
    d = xs.shape[1]
    blk = FFN_BLOCK
    grid_spec = pltpu.PrefetchScalarGridSpec(
        num_scalar_prefetch=3,
        grid=(blk_in.shape[0],),
        in_specs=[
            pl.BlockSpec((blk, d), lambda i, bi, grp, us: (bi[i], 0)),
            pl.BlockSpec((blk, TAIL_LANES), lambda i, bi, grp, us: (bi[i], 0)),
            pl.BlockSpec((EXPERTS_PER_GROUP,) + wg.shape[1:], lambda i, bi, grp, us: (grp[i], 0, 0)),
            pl.BlockSpec((EXPERTS_PER_GROUP,) + wu.shape[1:], lambda i, bi, grp, us: (grp[i], 0, 0)),
            pl.BlockSpec((1,) + wd.shape[1:], lambda i, bi, grp, us: (grp[i], 0, 0)),
        ],
        out_specs=pl.BlockSpec((blk, d), lambda i, bi, grp, us: (bi[i], 0)),
    )
    return pl.pallas_call(
        _ffn_kernel,
        out_shape=jax.ShapeDtypeStruct(xs.shape, BF16),
        grid_spec=grid_spec,
        compiler_params=pltpu.CompilerParams(
            dimension_semantics=("arbitrary",), vmem_limit_bytes=VMEM_LIMIT),
        name="group_ffn",
    )(blk_in, blk_grp, used, xs, tls, wg, wu, wd)


def _block_tables(seg_rows, cap, blk, n_steps):
    nblk = (seg_rows + blk - 1) // blk
    bend = jnp.cumsum(nblk)
    bstart = bend - nblk
    step = jnp.arange(n_steps, dtype=I32)
    used = step < bend[-1]
    grp = jnp.minimum(jnp.sum(step[:, None] >= bend[None, :], axis=1), N_GROUPS - 1).astype(I32)
    blk_in = grp * (cap // blk) + step - bstart[grp]
    last_real = jnp.maximum(bend[-1] - 1, 0)
    blk_in = jnp.where(used, blk_in, blk_in[last_real])
    grp = jnp.where(used, grp, grp[last_real])
    return blk_in.astype(I32), grp.astype(I32), used.astype(I32)


def _flat_meta(meta):
    m = meta.reshape(-1, SUBLANES, LANES)[:, :N_GROUPS, :]
    rec = jnp.concatenate([m[:, :, M_SLOT], m[:, :, M_LEN], m[:, :, M_OFF], m[:, :1, M_TOTAL],
                           jnp.zeros((m.shape[0], META_W - 3 * N_GROUPS - 1), I32)], axis=1)
    return rec.reshape(-1)


def kernel(x, mem, w_in, b_i, b_f, conv_qk, head_norm_g, pool_w, pool_scale, w_mix_out,
           ln_mix_g, ln_mix_b, w_xq, w_xkv, w_xo, ln_x_g, ln_x_b, router_w, router_bias,
           w_gate, w_up, w_down, ln_moe_g, ln_moe_b):
    bsz, seq, d = x.shape
    n_tok = bsz * seq
    n_tiles = n_tok // SEQ_TILE
    mw = N_HEADS * HEAD_DIM
    n_gate = 2 * N_HEADS
    blk = FFN_BLOCK
    cap = -(-(n_tok + RUN_ALIGN * n_tiles) // blk) * blk
    n_steps = (n_tok + N_GROUPS * RUN_ALIGN * n_tiles) // blk + N_GROUPS

    rw = jnp.pad(router_w, ((0, 0), (0, LANES - N_EXPERTS)))
    rwh, rwm, _ = _split3(rw)
    rw2 = jnp.concatenate([rwh, rwm], axis=1)
    rbias = router_bias.reshape(N_EXPERTS, 1).astype(F32)
    row = lambda v: v.reshape(1, -1).astype(F32)

    combine = None
    for l in range(DEPTH):
        wa = w_in[l][:, :4 * mw].astype(BF16)
        wu = w_in[l][:, 4 * mw + n_gate:].astype(BF16)
        wif = jnp.pad(w_in[l][:, 4 * mw:4 * mw + n_gate], ((0, 0), (0, LANES - n_gate))).astype(BF16)
        bif = jnp.pad(jnp.concatenate([b_i[l], b_f[l]]), (0, LANES - n_gate)).reshape(1, LANES)
        weights = (wa, wu, wif, bif, conv_qk[l], row(head_norm_g[l]), pool_w[l].astype(BF16),
                   row(pool_scale[l]), w_mix_out[l].astype(BF16), row(ln_mix_g[l]), row(ln_mix_b[l]))
        x = _mixer(x, weights, combine)

        x2, tail, meta, xs, tls = _xattn_router(
            x, mem, w_xq[l].astype(BF16), w_xkv[l].astype(BF16), w_xo[l].astype(BF16),
            row(ln_x_g[l]), row(ln_x_b[l]), rw2, rbias, cap, blk)

        seg_rows = meta[-1, -1, :N_GROUPS, M_END]
        blk_in, blk_grp, used = _block_tables(seg_rows, cap, blk, n_steps)
        ys = _ffn(blk_in, blk_grp, used, xs, tls, w_gate[l].astype(BF16), w_up[l].astype(BF16),
                  w_down[l].reshape(N_GROUPS, EXPERTS_PER_GROUP * w_down.shape[2], d).astype(BF16))
        x = x2
        combine = (_flat_meta(meta), tail, ys, row(ln_moe_g[l]), row(ln_moe_b[l]))

    meta, tail, ys, cg, cb = combine
    return _final_combine(meta, x.reshape(n_tok, d), tail.reshape(n_tok, TAIL_LANES), ys, cg, cb).reshape(bsz, seq, d)
```

```python
import functools

import jax
import jax.numpy as jnp
from jax import lax
from jax.experimental import pallas as pl
from jax.experimental.pallas import tpu as pltpu

F32 = jnp.float32
BF16 = jnp.bfloat16
I32 = jnp.int32

N_HEADS = 4
HEAD_DIM = 128
POOL_WINDOWS = (2, 4, 8, 16)
POOL_GROUP = 128
CONV_WIDTH = 4
XATTN_HEADS = 4
N_EXPERTS = 16
N_GROUPS = 4
EXPERTS_PER_GROUP = 4
DEPTH = 2
ALPHA = (2 * DEPTH) ** 0.25
LN_EPS = 1e-5

LANES = 128
SUBLANES = 8
BF16_TILE_ROWS = 16
VMEM_LIMIT = 56 * 1024 * 1024

SEQ_TILE = 256
MIXER_SEQS = 2
XATTN_SEQS = 4
FINAL_TILES = 2
HEADS_TOGETHER = 2
MLSTM_CHUNK = 256
FFN_BLOCK = 512
CONV_CARRY = 8
POOL_CARRY = 16

RUN_ALIGN = BF16_TILE_ROWS
STAGE_ROWS = SEQ_TILE + N_GROUPS * RUN_ALIGN
STAGE_ROWS_PADDED = 384
TAIL_LANES = 128
STAGE_LANE = 5
M_SLOT, M_LEN, M_OFF, M_TOTAL, M_END = 0, 1, 2, 3, 4
META_W = 16


def _layer_norm(y, g, b):
    mu = jnp.mean(y, axis=-1, keepdims=True)
    d = y - mu
    var = jnp.mean(d * d, axis=-1, keepdims=True)
    return d * lax.rsqrt(var + LN_EPS) * g + b


def _sigmoid(v):
    return 1.0 / (1.0 + jnp.exp(-v))


def _dot(a, b):
    return jnp.dot(a, b, preferred_element_type=F32)


def _dot_nt(a, b):
    return lax.dot_general(a, b, (((1,), (1,)), ((), ())), preferred_element_type=F32)


def _split3(v):
    hi = v.astype(BF16)
    r1 = v - hi.astype(F32)
    mid = r1.astype(BF16)
    lo = (r1 - mid.astype(F32)).astype(BF16)
    return hi, mid, lo


def _loop(n, body, unroll=1):
    lax.fori_loop(0, n, lambda j, c: (body(j), c)[1], 0, unroll=unroll)


def _aligned(v):
    return pl.multiple_of(v, RUN_ALIGN)


def _run_copy(src, src_row, dst, dst_row, length, sem):
    length = _aligned(length)

    @pl.when(length > 0)
    def _():
        pltpu.make_async_copy(src.at[pl.ds(_aligned(src_row), length)],
                              dst.at[pl.ds(_aligned(dst_row), length)], sem).start()


def _combine_fetch(meta_ref, tile, next_tile, first, has_next, slot, ys_hbm, stage, sem):
    def fetch(t, sl):
        for g in range(N_GROUPS):
            _run_copy(ys_hbm, meta_ref[t * META_W + g],
                      stage.at[sl], meta_ref[t * META_W + 2 * N_GROUPS + g],
                      meta_ref[t * META_W + N_GROUPS + g], sem.at[sl])

    @pl.when(first)
    def _():
        stage[...] = jnp.zeros(stage.shape, stage.dtype)
        fetch(tile, slot)

    @pl.when(has_next)
    def _():
        fetch(next_tile, 1 - slot)

    total = _aligned(meta_ref[tile * META_W + 3 * N_GROUPS])

    @pl.when(total > 0)
    def _():
        pltpu.make_async_copy(ys_hbm.at[pl.ds(0, total)], stage.at[slot, pl.ds(0, total)],
                              sem.at[slot]).wait()


def _combine(x2, tail, sorted_rows, lng, lnb):
    ts = x2.shape[0]
    pos = tail[:, STAGE_LANE:STAGE_LANE + 1].astype(I32)
    lane = lax.broadcasted_iota(I32, (ts, STAGE_ROWS_PADDED), 1)
    unsort = jnp.where(lane == pos, 1.0, 0.0).astype(BF16)
    return _layer_norm(ALPHA * x2 + _dot(unsort, sorted_rows), lng, lnb)


def _combine_scratch(d, lanes):
    return [pltpu.VMEM((lanes, 2, STAGE_ROWS_PADDED, d), BF16), pltpu.SemaphoreType.DMA((lanes, 2))]


def _final_kernel(meta_ref, x2_ref, tail_ref, ys_hbm, lng_ref, lnb_ref, o_ref, stage, sem):
    i = pl.program_id(0)
    ts = o_ref.shape[0] // FINAL_TILES
    for lane in range(FINAL_TILES):
        tile = i * FINAL_TILES + lane
        _combine_fetch(meta_ref, tile, tile + FINAL_TILES, i == 0, i + 1 < pl.num_programs(0), i % 2,
                       ys_hbm, stage.at[lane], sem.at[lane])
    for lane in range(FINAL_TILES):
        rows = slice(lane * ts, (lane + 1) * ts)
        o_ref[rows, :] = _combine(x2_ref[rows, :], tail_ref[rows, :], stage[lane, i % 2],
                                  lng_ref[...], lnb_ref[...])


def _final_combine(meta, x2, tail, ys, lng, lnb):
    n_tok, d = x2.shape
    ts = FINAL_TILES * (tail.shape[0] // (meta.shape[0] // META_W))
    grid_spec = pltpu.PrefetchScalarGridSpec(
        num_scalar_prefetch=1,
        grid=(n_tok // ts,),
        in_specs=[
            pl.BlockSpec((ts, d), lambda i, m: (i, 0)),
            pl.BlockSpec((ts, TAIL_LANES), lambda i, m: (i, 0)),
            pl.BlockSpec(memory_space=pl.ANY),
            pl.BlockSpec(lng.shape, lambda i, m: (0, 0)),
            pl.BlockSpec(lnb.shape, lambda i, m: (0, 0)),
        ],
        out_specs=pl.BlockSpec((ts, d), lambda i, m: (i, 0)),
        scratch_shapes=_combine_scratch(d, FINAL_TILES),
    )
    return pl.pallas_call(
        _final_kernel,
        out_shape=jax.ShapeDtypeStruct((n_tok, d), F32),
        grid_spec=grid_spec,
        compiler_params=pltpu.CompilerParams(
            dimension_semantics=("arbitrary",), vmem_limit_bytes=VMEM_LIMIT),
        name="final_combine",
    )(meta, x2, tail, ys, lng, lnb)


def _mixer_kernel(*refs, ts, lc, combine):
    n_lead = 6 if combine else 1
    zq_ext, u_carry, c_st, m_st = refs[n_lead + 12:n_lead + 16]
    b = pl.program_id(0)
    s = pl.program_id(1)
    ns = pl.num_programs(1)

    @pl.when(s == 0)
    def _():
        zq_ext[...] = jnp.zeros(zq_ext.shape, F32)
        u_carry[...] = jnp.zeros(u_carry.shape, F32)
        c_st[...] = jnp.zeros(c_st.shape, F32)
        m_st[...] = jnp.zeros(m_st.shape, F32)

    step = b * ns + s
    if combine:
        meta_ref, ys_hbm = refs[0], refs[3]
        stage, sem = refs[n_lead + 16:n_lead + 18]
        for lane in range(MIXER_SEQS):
            tile = (b * MIXER_SEQS + lane) * ns + s
            next_tile = jnp.where(s + 1 < ns, tile + 1, tile + (MIXER_SEQS - 1) * ns + 1)
            _combine_fetch(meta_ref, tile, next_tile, step == 0, step + 1 < pl.num_programs(0) * ns,
                           step % 2, ys_hbm, stage.at[lane], sem.at[lane])

    lanes = [_mixer_lane(lane, step % 2, refs, ts, lc, combine) for lane in range(MIXER_SEQS)]
    for _ in zip(*lanes):
        pass


def _mixer_lane(lane, slot, refs, ts, lc, combine):
    if combine:
        (_, x_ref, tail_ref, _, cg_ref, cb_ref), refs = refs[:6], refs[6:]
    else:
        x_ref, refs = refs[0], refs[1:]
    (wa_ref, wu_ref, wif_ref, bif_ref, conv_ref, hng_ref, poolw_ref, pscale_ref, wout_ref,
     lng_ref, lnb_ref, o_ref, zq_ext, u_carry, c_st, m_st) = refs[:16]
    zq_ext, u_carry, c_st, m_st = zq_ext.at[lane], u_carry.at[lane], c_st.at[lane], m_st.at[lane]
    s = pl.program_id(1)
    mw = N_HEADS * HEAD_DIM

    x = x_ref[lane]
    if combine:
        stage = refs[16]
        x = _combine(x, tail_ref[lane], stage[lane, slot], cg_ref[...], cb_ref[...])
    xb = x.astype(BF16)
    z = _dot(xb, wa_ref[...])
    u = _dot(xb, wu_ref[...])
    gts = _dot(xb, wif_ref[...]) + bif_ref[...]
    yield

    zq = z[:, :2 * mw]
    ze = jnp.concatenate([zq_ext[...], zq], axis=0)
    zq_ext[...] = zq[ts - CONV_CARRY:ts, :]
    cw = conv_ref[...]
    acc = zq * cw[CONV_WIDTH - 1:CONV_WIDTH, :]
    for j in range(1, CONV_WIDTH):
        acc = acc + pltpu.roll(ze, j, 0)[CONV_CARRY:, :] * cw[CONV_WIDTH - 1 - j:CONV_WIDTH - j, :]
    qk = acc * _sigmoid(acc)
    q_all = qk[:, :mw] * (HEAD_DIM ** -0.5)
    k_all = qk[:, mw:]
    v_all = z[:, 2 * mw:3 * mw]
    o_all = z[:, 3 * mw:4 * mw]

    lf_all = jnp.minimum(gts, 0.0) - jnp.log1p(jnp.exp(-jnp.abs(gts)))
    yield

    row_i = lax.broadcasted_iota(I32, (lc, lc), 0)
    col_i = lax.broadcasted_iota(I32, (lc, lc), 1)
    causal = col_i <= row_i
    tri = jnp.where(causal, 1.0, 0.0).astype(BF16)
    ones_col = jnp.where(lax.broadcasted_iota(I32, (lc, HEAD_DIM), 1) == 0, 1.0, 0.0).astype(BF16)

    head_out = [[] for _ in range(N_HEADS)]
    for c in range(ts // lc):
        rows = slice(c * lc, (c + 1) * lc)
        hi, mid, lo = _split3(lf_all[rows, :])
        b_all = _dot(tri, hi) + _dot(tri, mid) + _dot(tri, lo)
        g_c = gts[rows, :]
        r_all = g_c - pltpu.roll(b_all, LANES - N_HEADS, 1)
        r_t = r_all.T
        def head(h, rows=rows, b_all=b_all, g_c=g_c, r_t=r_t):
            hs = slice(h * HEAD_DIM, (h + 1) * HEAD_DIM)
            qh = q_all[rows, hs]
            kh = k_all[rows, hs]
            vh = jnp.concatenate([v_all[rows, hs].astype(BF16), ones_col], axis=1)
            qhb = qh.astype(BF16)
            bc = b_all[:, N_HEADS + h:N_HEADS + h + 1]
            igc = g_c[:, h:h + 1]
            r_row = r_t[h:h + 1, :]
            c_prev = c_st[h]
            m_prev = m_st[h][:, 0:1]
            qk = _dot_nt(qhb, kh.astype(BF16))
            qc = _dot(qhb, c_prev.astype(BF16))
            yield

            log_d = jnp.where(causal, bc + r_row, -jnp.inf)
            m_intra = jnp.max(log_d, axis=1, keepdims=True)
            log_inter = bc + m_prev
            m_t = jnp.maximum(m_intra, log_inter)
            p = jnp.exp(log_d - m_t) * qk
            inter = jnp.exp(log_inter - m_t)
            b_last = bc[lc - 1:lc, :]
            w_state = b_last - bc + igc
            m_loc = jnp.max(w_state, axis=0, keepdims=True)
            ka = kh * jnp.exp(w_state - m_loc)
            yield

            nd = _dot(p.astype(BF16), vh) + inter * qc
            c_loc = _dot(ka.T.astype(BF16), vh)
            yield

            den = nd[:, HEAD_DIM:HEAD_DIM + 1]
            hh = nd[:, :HEAD_DIM] * (1.0 / jnp.maximum(jnp.abs(den), jnp.exp(-m_t)))
            m_new = jnp.maximum(b_last + m_prev, m_loc)
            s_old = jnp.exp(b_last + m_prev - m_new)
            s_new = jnp.exp(m_loc - m_new)
            c_st[h] = s_old * c_prev + s_new * c_loc
            m_st[h] = jnp.broadcast_to(m_new, (1, LANES))

            mu = jnp.mean(hh, axis=1, keepdims=True)
            dlt = hh - mu
            var = jnp.mean(dlt * dlt, axis=1, keepdims=True)
            hn = dlt * lax.rsqrt(var + LN_EPS) * hng_ref[:, hs]
            head_out[h].append(hn * _sigmoid(o_all[rows, hs]))
            yield

        for h0 in range(0, N_HEADS, HEADS_TOGETHER):
            for _ in zip(*[head(h) for h in range(h0, h0 + HEADS_TOGETHER)]):
                yield

    mixed = [jnp.concatenate(ho, axis=0) if len(ho) > 1 else ho[0] for ho in head_out]

    ue = jnp.concatenate([u_carry[...], u], axis=0)
    u_carry[...] = u[ts - POOL_CARRY:ts, :]
    pos = (lax.broadcasted_iota(I32, (ts, 1), 0) + s * ts + 1).astype(F32)
    for g, w in enumerate(POOL_WINDOWS):
        cs = slice(g * POOL_GROUP, (g + 1) * POOL_GROUP)
        win = ue[:, cs]
        shift = 1
        while shift < w:
            win = win + pltpu.roll(win, shift, 0)
            shift *= 2
        ug = u[:, cs]
        pooled = win[POOL_CARRY:, :] / jnp.minimum(pos, float(w)) - ug
        pm = _dot(pooled.astype(BF16), poolw_ref[g]) * pscale_ref[:, cs]
        mixed.append(pm)
    yield

    mixed = jnp.concatenate(mixed, axis=1).astype(BF16)
    y = _dot(mixed, wout_ref[...])
    o_ref[lane] = _layer_norm(ALPHA * x + y, lng_ref[...], lnb_ref[...])
    yield


def _mixer(x, weights, combine=None):
    bsz, seq, d = x.shape
    ts = min(SEQ_TILE, seq)
    lc = min(MLSTM_CHUNK, ts)
    ns = seq // ts
    mw = N_HEADS * HEAD_DIM
    pw = weights[1].shape[1]
    const = lambda a: pl.BlockSpec(a.shape, lambda b, s, *_: (0,) * a.ndim)
    in_specs = [pl.BlockSpec((MIXER_SEQS, ts, d), lambda b, s, *_: (b, s, 0))]
    args = [x]
    scratch = [
        pltpu.VMEM((MIXER_SEQS, CONV_CARRY, 2 * mw), F32),
        pltpu.VMEM((MIXER_SEQS, POOL_CARRY, pw), F32),
        pltpu.VMEM((MIXER_SEQS, N_HEADS, HEAD_DIM, 2 * HEAD_DIM), F32),
        pltpu.VMEM((MIXER_SEQS, N_HEADS, 1, LANES), F32),
    ]
    prefetch = []
    if combine is not None:
        meta, tail, ys, cg, cb = combine
        prefetch = [meta]
        in_specs += [pl.BlockSpec((MIXER_SEQS, ts, TAIL_LANES), lambda b, s, *_: (b, s, 0)),
                     pl.BlockSpec(memory_space=pl.ANY), const(cg), const(cb)]
        args += [tail, ys, cg, cb]
        scratch += _combine_scratch(d, MIXER_SEQS)
    in_specs += [const(w) for w in weights]
    args += list(weights)
    grid_spec = pltpu.PrefetchScalarGridSpec(
        num_scalar_prefetch=len(prefetch),
        grid=(bsz // MIXER_SEQS, ns),
        in_specs=in_specs,
        out_specs=pl.BlockSpec((MIXER_SEQS, ts, d), lambda b, s, *_: (b, s, 0)),
        scratch_shapes=scratch,
    )
    return pl.pallas_call(
        functools.partial(_mixer_kernel, ts=ts, lc=lc, combine=combine is not None),
        out_shape=jax.ShapeDtypeStruct((bsz, seq, d), F32),
        grid_spec=grid_spec,
        compiler_params=pltpu.CompilerParams(
            dimension_semantics=("arbitrary", "arbitrary"), vmem_limit_bytes=VMEM_LIMIT),
        name="mixer",
    )(*prefetch, *args)


def _top2_sum(a, b, c, d):
    hi1, lo1 = jnp.maximum(a, b), jnp.minimum(a, b)
    hi2, lo2 = jnp.maximum(c, d), jnp.minimum(c, d)
    return jnp.maximum(hi1, hi2) + jnp.maximum(jnp.minimum(hi1, hi2), jnp.maximum(lo1, lo2))


def _xattn_kernel(x_ref, mem_ref, wq_ref, wkv_ref, wo_ref, lng_ref, lnb_ref,
                  rw2_ref, rbias_ref,
                  x2_ref, tail_ref, meta_ref, xs_hbm, tls_hbm,
                  k_scr, v_scr, carry, stx, stt, mvec, msm, prev_total, zx, zt, sem, ssem, zsem,
                  *, ts, cap, blk):
    b = pl.program_id(0)
    s = pl.program_id(1)
    step = b * pl.num_programs(1) + s
    last = step == pl.num_programs(0) * pl.num_programs(1) - 1
    d = x_ref.shape[2]

    @pl.when(s == 0)
    def _():
        for lane in range(XATTN_SEQS):
            kv = _dot(mem_ref[lane].astype(BF16), wkv_ref[...])
            k_scr[lane] = kv[:, :d].astype(BF16)
            v_scr[lane] = kv[:, d:].astype(BF16)

    @pl.when(step == 0)
    def _():
        carry[...] = jnp.zeros(carry.shape, F32)

    results = [None] * XATTN_SEQS
    lanes = [_xattn_lane(lane, results, x_ref, wq_ref, wo_ref, lng_ref, lnb_ref, rw2_ref, rbias_ref,
                         x2_ref, tail_ref, k_scr, v_scr, ts) for lane in range(XATTN_SEQS)]
    for _ in zip(*lanes):
        pass

    sub1 = lax.broadcasted_iota(I32, (SUBLANES, 1), 0)
    lane8 = lax.broadcasted_iota(I32, (SUBLANES, LANES), 1)
    base = carry[:, 0:1]
    for lane in range(XATTN_SEQS):
        len8, off8, total, _, _ = results[lane]
        new_base = base + len8
        mv = jnp.where(lane8 == M_SLOT, sub1.astype(F32) * float(cap) + base, 0.0)
        mv = jnp.where(lane8 == M_LEN, len8, mv)
        mv = jnp.where(lane8 == M_OFF, off8, mv)
        mv = jnp.where(lane8 == M_TOTAL, total, mv)
        mv = jnp.where(lane8 == M_END, new_base, mv).astype(I32)
        meta_ref[lane, 0] = mv
        mvec[lane * SUBLANES:(lane + 1) * SUBLANES, :] = mv
        base = new_base
    carry[...] = jnp.broadcast_to(base, carry.shape)

    def wait_runs(lane, n_rows):
        pltpu.make_async_copy(stx.at[lane, pl.ds(0, n_rows)], xs_hbm.at[pl.ds(0, n_rows)],
                              sem.at[lane, 0]).wait()
        pltpu.make_async_copy(stt.at[lane, pl.ds(0, n_rows)], tls_hbm.at[pl.ds(0, n_rows)],
                              sem.at[lane, 1]).wait()

    @pl.when(step > 0)
    def _():
        for lane in range(XATTN_SEQS):
            n_prev = _aligned(prev_total[lane])

            @pl.when(n_prev > 0)
            def _(lane=lane, n_prev=n_prev):
                wait_runs(lane, n_prev)

    for lane in range(XATTN_SEQS):
        stx[lane] = results[lane][3]
        stt[lane] = results[lane][4]
    to_smem = pltpu.make_async_copy(mvec, msm, ssem)
    to_smem.start()
    to_smem.wait()

    for lane in range(XATTN_SEQS):
        for g in range(N_GROUPS):
            r = lane * SUBLANES + g
            _run_copy(stx.at[lane], msm[r, M_OFF], xs_hbm, msm[r, M_SLOT], msm[r, M_LEN], sem.at[lane, 0])
            _run_copy(stt.at[lane], msm[r, M_OFF], tls_hbm, msm[r, M_SLOT], msm[r, M_LEN], sem.at[lane, 1])
        prev_total[lane] = msm[lane * SUBLANES, M_TOTAL]

    @pl.when(last)
    def _():
        for lane in range(XATTN_SEQS):
            n_own = _aligned(msm[lane * SUBLANES, M_TOTAL])

            @pl.when(n_own > 0)
            def _(lane=lane, n_own=n_own):
                wait_runs(lane, n_own)

        zx[...] = jnp.zeros(zx.shape, BF16)
        zt[...] = jnp.zeros(zt.shape, F32)
        for g in range(N_GROUPS):
            end = msm[(XATTN_SEQS - 1) * SUBLANES + g, M_END]
            n_pad = _aligned((blk - end % blk) % blk)
            _run_copy(zx, 0, xs_hbm, g * cap + end, n_pad, zsem.at[0])
            _run_copy(zt, 0, tls_hbm, g * cap + end, n_pad, zsem.at[1])

            @pl.when(n_pad > 0)
            def _(n_pad=n_pad):
                pltpu.make_async_copy(zx.at[pl.ds(0, n_pad)], xs_hbm.at[pl.ds(0, n_pad)], zsem.at[0]).wait()
                pltpu.make_async_copy(zt.at[pl.ds(0, n_pad)], tls_hbm.at[pl.ds(0, n_pad)], zsem.at[1]).wait()


def _xattn_lane(lane, results, x_ref, wq_ref, wo_ref, lng_ref, lnb_ref, rw2_ref, rbias_ref,
                x2_ref, tail_ref, k_scr, v_scr, ts):
    d = x_ref.shape[2]
    dh = d // XATTN_HEADS
    x = x_ref[lane]
    q = (_dot(x.astype(BF16), wq_ref[...]) * (dh ** -0.5)).astype(BF16)
    yield
    outs = []
    for h in range(XATTN_HEADS):
        hs = slice(h * dh, (h + 1) * dh)
        sc = _dot_nt(q[:, hs], k_scr[lane, :, hs])
        e = jnp.exp(sc - jnp.max(sc, axis=1, keepdims=True))
        l = jnp.sum(e, axis=1, keepdims=True)
        outs.append(_dot(e.astype(BF16), v_scr[lane, :, hs]) * (1.0 / l))
        yield
    o = jnp.concatenate(outs, axis=1).astype(BF16)
    x2 = _layer_norm(ALPHA * x + _dot(o, wo_ref[...]), lng_ref[...], lnb_ref[...])
    x2_ref[lane] = x2
    yield

    xh, xm, _ = _split3(x2)
    both = _dot(xh, rw2_ref[...])
    logits = (both[:, :LANES] + both[:, LANES:]) + _dot(xm, rw2_ref[:, :LANES])
    yield
    lt = logits.T[0:N_EXPERTS, :]
    score = _sigmoid(lt)
    sel = score + rbias_ref[...]

    sel_r = [sel[e:e + 1, :] for e in range(N_EXPERTS)]
    score_r = [score[e:e + 1, :] for e in range(N_EXPERTS)]
    gs = [_top2_sum(*sel_r[EXPERTS_PER_GROUP * g:EXPERTS_PER_GROUP * (g + 1)]) for g in range(N_GROUPS)]
    best = jnp.zeros((1, ts), I32)
    bestv = gs[0]
    for g in range(1, N_GROUPS):
        better = gs[g] > bestv
        best = jnp.where(better, g, best)
        bestv = jnp.where(better, gs[g], bestv)
    in_g = [best == g for g in range(N_GROUPS)]

    def pick(rows, j):
        out = rows[j]
        for g in range(1, N_GROUPS):
            out = jnp.where(in_g[g], rows[EXPERTS_PER_GROUP * g + j], out)
        return out

    vsel = [pick(sel_r, j) for j in range(EXPERTS_PER_GROUP)]
    vsc = [pick(score_r, j) for j in range(EXPERTS_PER_GROUP)]
    gates = []
    for j in range(EXPERTS_PER_GROUP):
        beaten = jnp.zeros((1, ts), I32)
        for k in range(EXPERTS_PER_GROUP):
            if k == j:
                continue
            wins = (vsel[k] > vsel[j]) | ((vsel[k] == vsel[j]) & (k < j))
            beaten = beaten + wins.astype(I32)
        gates.append(jnp.where(beaten < 2, vsc[j], 0.0))
    gsum = gates[0] + gates[1] + gates[2] + gates[3]
    gates = [g / gsum for g in gates]

    sub = lax.broadcasted_iota(I32, (SUBLANES, ts), 0)
    oh8 = jnp.zeros((SUBLANES, ts), F32)
    for g in range(N_GROUPS):
        oh8 = jnp.where((sub == g) & in_g[g], 1.0, oh8)
    r_i = lax.broadcasted_iota(I32, (ts, ts), 0)
    c_i = lax.broadcasted_iota(I32, (ts, ts), 1)
    upper = jnp.where(r_i < c_i, 1.0, 0.0).astype(BF16)
    excl = _dot(oh8.astype(BF16), upper)
    n8 = jnp.sum(oh8, axis=1, keepdims=True)
    len8 = jnp.floor((n8 + (RUN_ALIGN - 1)) * (1.0 / RUN_ALIGN)) * RUN_ALIGN
    sub1 = lax.broadcasted_iota(I32, (SUBLANES, 1), 0)
    off8 = jnp.zeros((SUBLANES, 1), F32)
    run_off = jnp.zeros((1, 1), F32)
    for g in range(N_GROUPS):
        off8 = jnp.where(sub1 == g, run_off, off8)
        run_off = run_off + len8[g:g + 1, :]
    pos = jnp.sum(jnp.where(oh8 > 0.0, off8 + excl, 0.0), axis=0, keepdims=True)

    t8 = jnp.where(sub == STAGE_LANE, pos, 0.0)
    for j in range(EXPERTS_PER_GROUP):
        t8 = jnp.where(sub == j, gates[j], t8)
    tail = jnp.concatenate([t8, jnp.zeros((TAIL_LANES - SUBLANES, ts), F32)], axis=0).T
    tail_ref[lane] = tail
    yield

    srow = lax.broadcasted_iota(I32, (STAGE_ROWS, ts), 0)
    sort = jnp.where(srow == pos.astype(I32), 1.0, 0.0).astype(BF16)
    xs_sorted = _dot(sort, xh).astype(BF16)
    pieces = _dot(sort, jnp.concatenate(_split3(tail), axis=1))
    tail_sorted = (pieces[:, :LANES] + pieces[:, LANES:2 * LANES]) + pieces[:, 2 * LANES:]
    results[lane] = (len8, off8, run_off, xs_sorted, tail_sorted)
    yield


def _xattn_router(x, mem, wq, wkv, wo, lng, lnb, rw2, rbias, cap, blk):
    bsz, seq, d = x.shape
    mlen = mem.shape[1]
    ts = min(SEQ_TILE, seq)
    assert ts == SEQ_TILE and blk % RUN_ALIGN == 0
    ns = seq // ts
    nl = XATTN_SEQS
    const = lambda a: pl.BlockSpec(a.shape, lambda b, s: (0,) * a.ndim)
    zero_rows = blk
    return pl.pallas_call(
        functools.partial(_xattn_kernel, ts=ts, cap=cap, blk=blk),
        out_shape=(
            jax.ShapeDtypeStruct((bsz, seq, d), F32),
            jax.ShapeDtypeStruct((bsz, seq, TAIL_LANES), F32),
            jax.ShapeDtypeStruct((bsz, ns, SUBLANES, LANES), I32),
            jax.ShapeDtypeStruct((N_GROUPS * cap, d), BF16),
            jax.ShapeDtypeStruct((N_GROUPS * cap, TAIL_LANES), F32),
        ),
        grid=(bsz // nl, ns),
        in_specs=[
            pl.BlockSpec((nl, ts, d), lambda b, s: (b, s, 0)),
            pl.BlockSpec((nl, mlen, d), lambda b, s: (b, 0, 0)),
            const(wq), const(wkv), const(wo), const(lng), const(lnb),
            const(rw2), const(rbias),
        ],
        out_specs=(
            pl.BlockSpec((nl, ts, d), lambda b, s: (b, s, 0)),
            pl.BlockSpec((nl, ts, TAIL_LANES), lambda b, s: (b, s, 0)),
            pl.BlockSpec((nl, 1, SUBLANES, LANES), lambda b, s: (b, s, 0, 0)),
            pl.BlockSpec(memory_space=pl.ANY),
            pl.BlockSpec(memory_space=pl.ANY),
        ),
        scratch_shapes=[
            pltpu.VMEM((nl, mlen, d), BF16),
            pltpu.VMEM((nl, mlen, d), BF16),
            pltpu.VMEM((SUBLANES, LANES), F32),
            pltpu.VMEM((nl, STAGE_ROWS, d), BF16),
            pltpu.VMEM((nl, STAGE_ROWS, TAIL_LANES), F32),
            pltpu.VMEM((nl * SUBLANES, LANES), I32),
            pltpu.SMEM((nl * SUBLANES, LANES), I32),
            pltpu.SMEM((nl,), I32),
            pltpu.VMEM((zero_rows, d), BF16),
            pltpu.VMEM((zero_rows, TAIL_LANES), F32),
            pltpu.SemaphoreType.DMA((nl, 2)),
            pltpu.SemaphoreType.DMA,
            pltpu.SemaphoreType.DMA((2,)),
        ],
        compiler_params=pltpu.CompilerParams(
            dimension_semantics=("arbitrary", "arbitrary"), vmem_limit_bytes=VMEM_LIMIT,
            has_side_effects=True),
        name="xattn_router",
    )(x, mem, wq, wkv, wo, lng, lnb, rw2, rbias)


def _ffn_kernel(blk_in_ref, grp_ref, used_ref, xs_ref, tl_ref, wg_ref, wu_ref, wd_ref, o_ref):
    @pl.when(used_ref[pl.program_id(0)] == 1)
    def _():
        xb = xs_ref[...]
        parts = []
        for j in range(EXPERTS_PER_GROUP):
            hg = _dot(xb, wg_ref[j])
            hu = _dot(xb, wu_ref[j])
            gate = tl_ref[:, j:j + 1]
            parts.append(jnp.where(gate != 0.0, hg * _sigmoid(hg) * hu * gate, 0.0))
        hid = jnp.concatenate(parts, axis=1).astype(BF16)
        o_ref[...] = _dot(hid, wd_ref[0]).astype(BF16)


def _ffn(blk_in, blk_grp, used, xs, tls, wg, wu, wd):
    d = xs.shape[1]
    blk = FFN_BLOCK
    grid_spec = pltpu.PrefetchScalarGridSpec(
        num_scalar_prefetch=3,
        grid=(blk_in.shape[0],),
        in_specs=[
            pl.BlockSpec((blk, d), lambda i, bi, grp, us: (bi[i], 0)),
            pl.BlockSpec((blk, TAIL_LANES), lambda i, bi, grp, us: (bi[i], 0)),
            pl.BlockSpec((EXPERTS_PER_GROUP,) + wg.shape[1:], lambda i, bi, grp, us: (grp[i], 0, 0)),
            pl.BlockSpec((EXPERTS_PER_GROUP,) + wu.shape[1:], lambda i, bi, grp, us: (grp[i], 0, 0)),
            pl.BlockSpec((1,) + wd.shape[1:], lambda i, bi, grp, us: (grp[i], 0, 0)),
        ],
        out_specs=pl.BlockSpec((blk, d), lambda i, bi, grp, us: (bi[i], 0)),
    )
    return pl.pallas_call(
        _ffn_kernel,
        out_shape=jax.ShapeDtypeStruct(xs.shape, BF16),
        grid_spec=grid_spec,
        compiler_params=pltpu.CompilerParams(
            dimension_semantics=("arbitrary",), vmem_limit_bytes=VMEM_LIMIT),
        name="group_ffn",
    )(blk_in, blk_grp, used, xs, tls, wg, wu, wd)


def _block_tables(seg_rows, cap, blk, n_steps):
    nblk = (seg_rows + blk - 1) // blk
    bend = jnp.cumsum(nblk)
    bstart = bend - nblk
    step = jnp.arange(n_steps, dtype=I32)
    used = step < bend[-1]
    grp = jnp.minimum(jnp.sum(step[:, None] >= bend[None, :], axis=1), N_GROUPS - 1).astype(I32)
    blk_in = grp * (cap // blk) + step - bstart[grp]
    last_real = jnp.maximum(bend[-1] - 1, 0)
    blk_in = jnp.where(used, blk_in, blk_in[last_real])
    grp = jnp.where(used, grp, grp[last_real])
    return blk_in.astype(I32), grp.astype(I32), used.astype(I32)


def _flat_meta(meta):
    m = meta.reshape(-1, SUBLANES, LANES)[:, :N_GROUPS, :]
    rec = jnp.concatenate([m[:, :, M_SLOT], m[:, :, M_LEN], m[:, :, M_OFF], m[:, :1, M_TOTAL],
                           jnp.zeros((m.shape[0], META_W - 3 * N_GROUPS - 1), I32)], axis=1)
    return rec.reshape(-1)


def kernel(x, mem, w_in, b_i, b_f, conv_qk, head_norm_g, pool_w, pool_scale, w_mix_out,
           ln_mix_g, ln_mix_b, w_xq, w_xkv, w_xo, ln_x_g, ln_x_b, router_w, router_bias,
           w_gate, w_up, w_down, ln_moe_g, ln_moe_b):
    bsz, seq, d = x.shape
    n_tok = bsz * seq
    n_tiles = n_tok // SEQ_TILE
    mw = N_HEADS * HEAD_DIM
    n_gate = 2 * N_HEADS
    blk = FFN_BLOCK
    cap = -(-(n_tok + RUN_ALIGN * n_tiles) // blk) * blk
    n_steps = (n_tok + N_GROUPS * RUN_ALIGN * n_tiles) // blk + N_GROUPS

    rw = jnp.pad(router_w, ((0, 0), (0, LANES - N_EXPERTS)))
    rwh, rwm, _ = _split3(rw)
    rw2 = jnp.concatenate([rwh, rwm], axis=1)
    rbias = router_bias.reshape(N_EXPERTS, 1).astype(F32)
    row = lambda v: v.reshape(1, -1).astype(F32)

    combine = None
    for l in range(DEPTH):
        wa = w_in[l][:, :4 * mw].astype(BF16)
        wu = w_in[l][:, 4 * mw + n_gate:].astype(BF16)
        wif = jnp.pad(w_in[l][:, 4 * mw:4 * mw + n_gate], ((0, 0), (0, LANES - n_gate))).astype(BF16)
        bif = jnp.pad(jnp.concatenate([b_i[l], b_f[l]]), (0, LANES - n_gate)).reshape(1, LANES)
        weights = (wa, wu, wif, bif, conv_qk[l], row(head_norm_g[l]), pool_w[l].astype(BF16),
                   row(pool_scale[l]), w_mix_out[l].astype(BF16), row(ln_mix_g[l]), row(ln_mix_b[l]))
        x = _mixer(x, weights, combine)

        x2, tail, meta, xs, tls = _xattn_router(
            x, mem, w_xq[l].astype(BF16), w_xkv[l].astype(BF16), w_xo[l].astype(BF16),
            row(ln_x_g[l]), row(ln_x_b[l]), rw2, rbias, cap, blk)

        seg_rows = meta[-1, -1, :N_GROUPS, M_END]
        blk_in, blk_grp, used = _block_tables(seg_rows, cap, blk, n_steps)
        ys = _ffn(blk_in, blk_grp, used, xs, tls, w_gate[l].astype(BF16), w_up[l].astype(BF16),
                  w_down[l].reshape(N_GROUPS, EXPERTS_PER_GROUP * w_down.shape[2], d).astype(BF16))
        x = x2
        combine = (_flat_meta(meta), tail, ys, row(ln_moe_g[l]), row(ln_moe_b[l]))

    meta, tail, ys, cg, cb = combine
    return _final_combine(meta, x.reshape(n_tok, d), tail.reshape(n_tok, TAIL_LANES), ys, cg, cb).reshape(bsz, seq, d)
```

```python
import functools

import jax
import jax.numpy as jnp
from jax import lax
from jax.experimental import pallas as pl
from jax.experimental.pallas import tpu as pltpu

F32 = jnp.float32
BF16 = jnp.bfloat16
I32 = jnp.int32

N_HEADS = 4
HEAD_DIM = 128
POOL_WINDOWS = (2, 4, 8, 16)
POOL_GROUP = 128
CONV_WIDTH = 4
XATTN_HEADS = 4
N_EXPERTS = 16
N_GROUPS = 4
EXPERTS_PER_GROUP = 4
DEPTH = 2
ALPHA = (2 * DEPTH) ** 0.25
LN_EPS = 1e-5

LANES = 128
SUBLANES = 8
BF16_TILE_ROWS = 16
VMEM_LIMIT = 56 * 1024 * 1024

SEQ_TILE = 256
MIXER_SEQS = 2
XATTN_SEQS = 4
FINAL_TILES = 4
HEADS_TOGETHER = 2
MLSTM_CHUNK = 256
FFN_BLOCK = 512
CONV_CARRY = 8
POOL_CARRY = 16

RUN_ALIGN = BF16_TILE_ROWS
STAGE_ROWS = SEQ_TILE + N_GROUPS * RUN_ALIGN
STAGE_ROWS_PADDED = 384
TAIL_LANES = 128
STAGE_LANE = 5
M_SLOT, M_LEN, M_OFF, M_TOTAL, M_END = 0, 1, 2, 3, 4
META_W = 16


def _layer_norm(y, g, b):
    mu = jnp.mean(y, axis=-1, keepdims=True)
    d = y - mu
    var = jnp.mean(d * d, axis=-1, keepdims=True)
    return d * lax.rsqrt(var + LN_EPS) * g + b


def _sigmoid(v):
    return 1.0 / (1.0 + jnp.exp(-v))


def _dot(a, b):
    return jnp.dot(a, b, preferred_element_type=F32)


def _dot_nt(a, b):
    return lax.dot_general(a, b, (((1,), (1,)), ((), ())), preferred_element_type=F32)


def _split3(v):
    hi = v.astype(BF16)
    r1 = v - hi.astype(F32)
    mid = r1.astype(BF16)
    lo = (r1 - mid.astype(F32)).astype(BF16)
    return hi, mid, lo


def _loop(n, body, unroll=1):
    lax.fori_loop(0, n, lambda j, c: (body(j), c)[1], 0, unroll=unroll)


def _aligned(v):
    return pl.multiple_of(v, RUN_ALIGN)


def _run_copy(src, src_row, dst, dst_row, length, sem):
    length = _aligned(length)

    @pl.when(length > 0)
    def _():
        pltpu.make_async_copy(src.at[pl.ds(_aligned(src_row), length)],
                              dst.at[pl.ds(_aligned(dst_row), length)], sem).start()


def _combine_fetch(meta_ref, tile, next_tile, first, has_next, slot, ys_hbm, stage, sem):
    def fetch(t, sl):
        for g in range(N_GROUPS):
            _run_copy(ys_hbm, meta_ref[t * META_W + g],
                      stage.at[sl], meta_ref[t * META_W + 2 * N_GROUPS + g],
                      meta_ref[t * META_W + N_GROUPS + g], sem.at[sl])

    @pl.when(first)
    def _():
        stage[...] = jnp.zeros(stage.shape, stage.dtype)
        fetch(tile, slot)

    @pl.when(has_next)
    def _():
        fetch(next_tile, 1 - slot)

    total = _aligned(meta_ref[tile * META_W + 3 * N_GROUPS])

    @pl.when(total > 0)
    def _():
        pltpu.make_async_copy(ys_hbm.at[pl.ds(0, total)], stage.at[slot, pl.ds(0, total)],
                              sem.at[slot]).wait()


def _combine(x2, tail, sorted_rows, lng, lnb):
    ts = x2.shape[0]
    pos = tail[:, STAGE_LANE:STAGE_LANE + 1].astype(I32)
    lane = lax.broadcasted_iota(I32, (ts, STAGE_ROWS_PADDED), 1)
    unsort = jnp.where(lane == pos, 1.0, 0.0).astype(BF16)
    return _layer_norm(ALPHA * x2 + _dot(unsort, sorted_rows), lng, lnb)


def _combine_scratch(d, lanes):
    return [pltpu.VMEM((lanes, 2, STAGE_ROWS_PADDED, d), BF16), pltpu.SemaphoreType.DMA((lanes, 2))]


def _final_kernel(meta_ref, x2_ref, tail_ref, ys_hbm, lng_ref, lnb_ref, o_ref, stage, sem):
    i = pl.program_id(0)
    ts = o_ref.shape[0] // FINAL_TILES
    for lane in range(FINAL_TILES):
        tile = i * FINAL_TILES + lane
        _combine_fetch(meta_ref, tile, tile + FINAL_TILES, i == 0, i + 1 < pl.num_programs(0), i % 2,
                       ys_hbm, stage.at[lane], sem.at[lane])
    for lane in range(FINAL_TILES):
        rows = slice(lane * ts, (lane + 1) * ts)
        o_ref[rows, :] = _combine(x2_ref[rows, :], tail_ref[rows, :], stage[lane, i % 2],
                                  lng_ref[...], lnb_ref[...])


def _final_combine(meta, x2, tail, ys, lng, lnb):
    n_tok, d = x2.shape
    ts = FINAL_TILES * (tail.shape[0] // (meta.shape[0] // META_W))
    grid_spec = pltpu.PrefetchScalarGridSpec(
        num_scalar_prefetch=1,
        grid=(n_tok // ts,),
        in_specs=[
            pl.BlockSpec((ts, d), lambda i, m: (i, 0)),
            pl.BlockSpec((ts, TAIL_LANES), lambda i, m: (i, 0)),
            pl.BlockSpec(memory_space=pl.ANY),
            pl.BlockSpec(lng.shape, lambda i, m: (0, 0)),
            pl.BlockSpec(lnb.shape, lambda i, m: (0, 0)),
        ],
        out_specs=pl.BlockSpec((ts, d), lambda i, m: (i, 0)),
        scratch_shapes=_combine_scratch(d, FINAL_TILES),
    )
    return pl.pallas_call(
        _final_kernel,
        out_shape=jax.ShapeDtypeStruct((n_tok, d), F32),
        grid_spec=grid_spec,
        compiler_params=pltpu.CompilerParams(
            dimension_semantics=("arbitrary",), vmem_limit_bytes=VMEM_LIMIT),
        name="final_combine",
    )(meta, x2, tail, ys, lng, lnb)


def _mixer_kernel(*refs, ts, lc, combine):
    n_lead = 6 if combine else 1
    zq_ext, u_carry, c_st, m_st = refs[n_lead + 12:n_lead + 16]
    b = pl.program_id(0)
    s = pl.program_id(1)
    ns = pl.num_programs(1)

    @pl.when(s == 0)
    def _():
        zq_ext[...] = jnp.zeros(zq_ext.shape, F32)
        u_carry[...] = jnp.zeros(u_carry.shape, F32)
        c_st[...] = jnp.zeros(c_st.shape, F32)
        m_st[...] = jnp.zeros(m_st.shape, F32)

    step = b * ns + s
    if combine:
        meta_ref, ys_hbm = refs[0], refs[3]
        stage, sem = refs[n_lead + 16:n_lead + 18]
        for lane in range(MIXER_SEQS):
            tile = (b * MIXER_SEQS + lane) * ns + s
            next_tile = jnp.where(s + 1 < ns, tile + 1, tile + (MIXER_SEQS - 1) * ns + 1)
            _combine_fetch(meta_ref, tile, next_tile, step == 0, step + 1 < pl.num_programs(0) * ns,
                           step % 2, ys_hbm, stage.at[lane], sem.at[lane])

    lanes = [_mixer_lane(lane, step % 2, refs, ts, lc, combine) for lane in range(MIXER_SEQS)]
    for _ in zip(*lanes):
        pass


def _mixer_lane(lane, slot, refs, ts, lc, combine):
    if combine:
        (_, x_ref, tail_ref, _, cg_ref, cb_ref), refs = refs[:6], refs[6:]
    else:
        x_ref, refs = refs[0], refs[1:]
    (wa_ref, wu_ref, wif_ref, bif_ref, conv_ref, hng_ref, poolw_ref, pscale_ref, wout_ref,
     lng_ref, lnb_ref, o_ref, zq_ext, u_carry, c_st, m_st) = refs[:16]
    zq_ext, u_carry, c_st, m_st = zq_ext.at[lane], u_carry.at[lane], c_st.at[lane], m_st.at[lane]
    s = pl.program_id(1)
    mw = N_HEADS * HEAD_DIM

    x = x_ref[lane]
    if combine:
        stage = refs[16]
        x = _combine(x, tail_ref[lane], stage[lane, slot], cg_ref[...], cb_ref[...])
    xb = x.astype(BF16)
    z = _dot(xb, wa_ref[...])
    u = _dot(xb, wu_ref[...])
    gts = _dot(xb, wif_ref[...]) + bif_ref[...]
    yield

    zq = z[:, :2 * mw]
    ze = jnp.concatenate([zq_ext[...], zq], axis=0)
    zq_ext[...] = zq[ts - CONV_CARRY:ts, :]
    cw = conv_ref[...]
    acc = zq * cw[CONV_WIDTH - 1:CONV_WIDTH, :]
    for j in range(1, CONV_WIDTH):
        acc = acc + pltpu.roll(ze, j, 0)[CONV_CARRY:, :] * cw[CONV_WIDTH - 1 - j:CONV_WIDTH - j, :]
    qk = acc * _sigmoid(acc)
    q_all = qk[:, :mw] * (HEAD_DIM ** -0.5)
    k_all = qk[:, mw:]
    v_all = z[:, 2 * mw:3 * mw]
    o_all = z[:, 3 * mw:4 * mw]

    lf_all = jnp.minimum(gts, 0.0) - jnp.log1p(jnp.exp(-jnp.abs(gts)))
    yield

    row_i = lax.broadcasted_iota(I32, (lc, lc), 0)
    col_i = lax.broadcasted_iota(I32, (lc, lc), 1)
    causal = col_i <= row_i
    tri = jnp.where(causal, 1.0, 0.0).astype(BF16)
    ones_col = jnp.where(lax.broadcasted_iota(I32, (lc, HEAD_DIM), 1) == 0, 1.0, 0.0).astype(BF16)

    head_out = [[] for _ in range(N_HEADS)]
    for c in range(ts // lc):
        rows = slice(c * lc, (c + 1) * lc)
        hi, mid, lo = _split3(lf_all[rows, :])
        b_all = _dot(tri, hi) + _dot(tri, mid) + _dot(tri, lo)
        g_c = gts[rows, :]
        r_all = g_c - pltpu.roll(b_all, LANES - N_HEADS, 1)
        r_t = r_all.T
        def head(h, rows=rows, b_all=b_all, g_c=g_c, r_t=r_t):
            hs = slice(h * HEAD_DIM, (h + 1) * HEAD_DIM)
            qh = q_all[rows, hs]
            kh = k_all[rows, hs]
            vh = jnp.concatenate([v_all[rows, hs].astype(BF16), ones_col], axis=1)
            qhb = qh.astype(BF16)
            bc = b_all[:, N_HEADS + h:N_HEADS + h + 1]
            igc = g_c[:, h:h + 1]
            r_row = r_t[h:h + 1, :]
            c_prev = c_st[h]
            m_prev = m_st[h][:, 0:1]
            qk = _dot_nt(qhb, kh.astype(BF16))
            qc = _dot(qhb, c_prev.astype(BF16))
            yield

            log_d = jnp.where(causal, bc + r_row, -jnp.inf)
            m_intra = jnp.max(log_d, axis=1, keepdims=True)
            log_inter = bc + m_prev
            m_t = jnp.maximum(m_intra, log_inter)
            p = jnp.exp(log_d - m_t) * qk
            inter = jnp.exp(log_inter - m_t)
            b_last = bc[lc - 1:lc, :]
            w_state = b_last - bc + igc
            m_loc = jnp.max(w_state, axis=0, keepdims=True)
            ka = kh * jnp.exp(w_state - m_loc)
            yield

            nd = _dot(p.astype(BF16), vh) + inter * qc
            c_loc = _dot(ka.T.astype(BF16), vh)
            yield

            den = nd[:, HEAD_DIM:HEAD_DIM + 1]
            hh = nd[:, :HEAD_DIM] * (1.0 / jnp.maximum(jnp.abs(den), jnp.exp(-m_t)))
            m_new = jnp.maximum(b_last + m_prev, m_loc)
            s_old = jnp.exp(b_last + m_prev - m_new)
            s_new = jnp.exp(m_loc - m_new)
            c_st[h] = s_old * c_prev + s_new * c_loc
            m_st[h] = jnp.broadcast_to(m_new, (1, LANES))

            mu = jnp.mean(hh, axis=1, keepdims=True)
            dlt = hh - mu
            var = jnp.mean(dlt * dlt, axis=1, keepdims=True)
            hn = dlt * lax.rsqrt(var + LN_EPS) * hng_ref[:, hs]
            head_out[h].append(hn * _sigmoid(o_all[rows, hs]))
            yield

        for h0 in range(0, N_HEADS, HEADS_TOGETHER):
            for _ in zip(*[head(h) for h in range(h0, h0 + HEADS_TOGETHER)]):
                yield

    mixed = [jnp.concatenate(ho, axis=0) if len(ho) > 1 else ho[0] for ho in head_out]

    ue = jnp.concatenate([u_carry[...], u], axis=0)
    u_carry[...] = u[ts - POOL_CARRY:ts, :]
    pos = (lax.broadcasted_iota(I32, (ts, 1), 0) + s * ts + 1).astype(F32)
    for g, w in enumerate(POOL_WINDOWS):
        cs = slice(g * POOL_GROUP, (g + 1) * POOL_GROUP)
        win = ue[:, cs]
        shift = 1
        while shift < w:
            win = win + pltpu.roll(win, shift, 0)
            shift *= 2
        ug = u[:, cs]
        pooled = win[POOL_CARRY:, :] / jnp.minimum(pos, float(w)) - ug
        pm = _dot(pooled.astype(BF16), poolw_ref[g]) * pscale_ref[:, cs]
        mixed.append(pm)
    yield

    mixed = jnp.concatenate(mixed, axis=1).astype(BF16)
    y = _dot(mixed, wout_ref[...])
    o_ref[lane] = _layer_norm(ALPHA * x + y, lng_ref[...], lnb_ref[...])
    yield


def _mixer(x, weights, combine=None):
    bsz, seq, d = x.shape
    ts = min(SEQ_TILE, seq)
    lc = min(MLSTM_CHUNK, ts)
    ns = seq // ts
    mw = N_HEADS * HEAD_DIM
    pw = weights[1].shape[1]
    const = lambda a: pl.BlockSpec(a.shape, lambda b, s, *_: (0,) * a.ndim)
    in_specs = [pl.BlockSpec((MIXER_SEQS, ts, d), lambda b, s, *_: (b, s, 0))]
    args = [x]
    scratch = [
        pltpu.VMEM((MIXER_SEQS, CONV_CARRY, 2 * mw), F32),
        pltpu.VMEM((MIXER_SEQS, POOL_CARRY, pw), F32),
        pltpu.VMEM((MIXER_SEQS, N_HEADS, HEAD_DIM, 2 * HEAD_DIM), F32),
        pltpu.VMEM((MIXER_SEQS, N_HEADS, 1, LANES), F32),
    ]
    prefetch = []
    if combine is not None:
        meta, tail, ys, cg, cb = combine
        prefetch = [meta]
        in_specs += [pl.BlockSpec((MIXER_SEQS, ts, TAIL_LANES), lambda b, s, *_: (b, s, 0)),
                     pl.BlockSpec(memory_space=pl.ANY), const(cg), const(cb)]
        args += [tail, ys, cg, cb]
        scratch += _combine_scratch(d, MIXER_SEQS)
    in_specs += [const(w) for w in weights]
    args += list(weights)
    grid_spec = pltpu.PrefetchScalarGridSpec(
        num_scalar_prefetch=len(prefetch),
        grid=(bsz // MIXER_SEQS, ns),
        in_specs=in_specs,
        out_specs=pl.BlockSpec((MIXER_SEQS, ts, d), lambda b, s, *_: (b, s, 0)),
        scratch_shapes=scratch,
    )
    return pl.pallas_call(
        functools.partial(_mixer_kernel, ts=ts, lc=lc, combine=combine is not None),
        out_shape=jax.ShapeDtypeStruct((bsz, seq, d), F32),
        grid_spec=grid_spec,
        compiler_params=pltpu.CompilerParams(
            dimension_semantics=("arbitrary", "arbitrary"), vmem_limit_bytes=VMEM_LIMIT),
        name="mixer",
    )(*prefetch, *args)


def _top2_sum(a, b, c, d):
    hi1, lo1 = jnp.maximum(a, b), jnp.minimum(a, b)
    hi2, lo2 = jnp.maximum(c, d), jnp.minimum(c, d)
    return jnp.maximum(hi1, hi2) + jnp.maximum(jnp.minimum(hi1, hi2), jnp.maximum(lo1, lo2))


def _xattn_kernel(x_ref, mem_ref, wq_ref, wkv_ref, wo_ref, lng_ref, lnb_ref,
                  rw2_ref, rbias_ref,
                  x2_ref, tail_ref, meta_ref, xs_hbm, tls_hbm,
                  k_scr, v_scr, carry, stx, stt, mvec, msm, prev_total, zx, zt, sem, ssem, zsem,
                  *, ts, cap, blk):
    b = pl.program_id(0)
    s = pl.program_id(1)
    step = b * pl.num_programs(1) + s
    last = step == pl.num_programs(0) * pl.num_programs(1) - 1
    d = x_ref.shape[2]

    @pl.when(s == 0)
    def _():
        for lane in range(XATTN_SEQS):
            kv = _dot(mem_ref[lane].astype(BF16), wkv_ref[...])
            k_scr[lane] = kv[:, :d].astype(BF16)
            v_scr[lane] = kv[:, d:].astype(BF16)

    @pl.when(step == 0)
    def _():
        carry[...] = jnp.zeros(carry.shape, F32)

    results = [None] * XATTN_SEQS
    lanes = [_xattn_lane(lane, results, x_ref, wq_ref, wo_ref, lng_ref, lnb_ref, rw2_ref, rbias_ref,
                         x2_ref, tail_ref, k_scr, v_scr, ts) for lane in range(XATTN_SEQS)]
    for _ in zip(*lanes):
        pass

    sub1 = lax.broadcasted_iota(I32, (SUBLANES, 1), 0)
    lane8 = lax.broadcasted_iota(I32, (SUBLANES, LANES), 1)
    base = carry[:, 0:1]
    for lane in range(XATTN_SEQS):
        len8, off8, total, _, _ = results[lane]
        new_base = base + len8
        mv = jnp.where(lane8 == M_SLOT, sub1.astype(F32) * float(cap) + base, 0.0)
        mv = jnp.where(lane8 == M_LEN, len8, mv)
        mv = jnp.where(lane8 == M_OFF, off8, mv)
        mv = jnp.where(lane8 == M_TOTAL, total, mv)
        mv = jnp.where(lane8 == M_END, new_base, mv).astype(I32)
        meta_ref[lane, 0] = mv
        mvec[lane * SUBLANES:(lane + 1) * SUBLANES, :] = mv
        base = new_base
    carry[...] = jnp.broadcast_to(base, carry.shape)

    def wait_runs(lane, n_rows):
        pltpu.make_async_copy(stx.at[lane, pl.ds(0, n_rows)], xs_hbm.at[pl.ds(0, n_rows)],
                              sem.at[lane, 0]).wait()
        pltpu.make_async_copy(stt.at[lane, pl.ds(0, n_rows)], tls_hbm.at[pl.ds(0, n_rows)],
                              sem.at[lane, 1]).wait()

    @pl.when(step > 0)
    def _():
        for lane in range(XATTN_SEQS):
            n_prev = _aligned(prev_total[lane])

            @pl.when(n_prev > 0)
            def _(lane=lane, n_prev=n_prev):
                wait_runs(lane, n_prev)

    for lane in range(XATTN_SEQS):
        stx[lane] = results[lane][3]
        stt[lane] = results[lane][4]
    to_smem = pltpu.make_async_copy(mvec, msm, ssem)
    to_smem.start()
    to_smem.wait()

    for lane in range(XATTN_SEQS):
        for g in range(N_GROUPS):
            r = lane * SUBLANES + g
            _run_copy(stx.at[lane], msm[r, M_OFF], xs_hbm, msm[r, M_SLOT], msm[r, M_LEN], sem.at[lane, 0])
            _run_copy(stt.at[lane], msm[r, M_OFF], tls_hbm, msm[r, M_SLOT], msm[r, M_LEN], sem.at[lane, 1])
        prev_total[lane] = msm[lane * SUBLANES, M_TOTAL]

    @pl.when(last)
    def _():
        for lane in range(XATTN_SEQS):
            n_own = _aligned(msm[lane * SUBLANES, M_TOTAL])

            @pl.when(n_own > 0)
            def _(lane=lane, n_own=n_own):
                wait_runs(lane, n_own)

        zx[...] = jnp.zeros(zx.shape, BF16)
        zt[...] = jnp.zeros(zt.shape, F32)
        for g in range(N_GROUPS):
            end = msm[(XATTN_SEQS - 1) * SUBLANES + g, M_END]
            n_pad = _aligned((blk - end % blk) % blk)
            _run_copy(zx, 0, xs_hbm, g * cap + end, n_pad, zsem.at[0])
            _run_copy(zt, 0, tls_hbm, g * cap + end, n_pad, zsem.at[1])

            @pl.when(n_pad > 0)
            def _(n_pad=n_pad):
                pltpu.make_async_copy(zx.at[pl.ds(0, n_pad)], xs_hbm.at[pl.ds(0, n_pad)], zsem.at[0]).wait()
                pltpu.make_async_copy(zt.at[pl.ds(0, n_pad)], tls_hbm.at[pl.ds(0, n_pad)], zsem.at[1]).wait()


def _xattn_lane(lane, results, x_ref, wq_ref, wo_ref, lng_ref, lnb_ref, rw2_ref, rbias_ref,
                x2_ref, tail_ref, k_scr, v_scr, ts):
    d = x_ref.shape[2]
    dh = d // XATTN_HEADS
    x = x_ref[lane]
    q = (_dot(x.astype(BF16), wq_ref[...]) * (dh ** -0.5)).astype(BF16)
    yield
    outs = []
    for h in range(XATTN_HEADS):
        hs = slice(h * dh, (h + 1) * dh)
        sc = _dot_nt(q[:, hs], k_scr[lane, :, hs])
        e = jnp.exp(sc - jnp.max(sc, axis=1, keepdims=True))
        l = jnp.sum(e, axis=1, keepdims=True)
        outs.append(_dot(e.astype(BF16), v_scr[lane, :, hs]) * (1.0 / l))
        yield
    o = jnp.concatenate(outs, axis=1).astype(BF16)
    x2 = _layer_norm(ALPHA * x + _dot(o, wo_ref[...]), lng_ref[...], lnb_ref[...])
    x2_ref[lane] = x2
    yield

    xh, xm, _ = _split3(x2)
    both = _dot(xh, rw2_ref[...])
    logits = (both[:, :LANES] + both[:, LANES:]) + _dot(xm, rw2_ref[:, :LANES])
    yield
    lt = logits.T[0:N_EXPERTS, :]
    score = _sigmoid(lt)
    sel = score + rbias_ref[...]

    sel_r = [sel[e:e + 1, :] for e in range(N_EXPERTS)]
    score_r = [score[e:e + 1, :] for e in range(N_EXPERTS)]
    gs = [_top2_sum(*sel_r[EXPERTS_PER_GROUP * g:EXPERTS_PER_GROUP * (g + 1)]) for g in range(N_GROUPS)]
    best = jnp.zeros((1, ts), I32)
    bestv = gs[0]
    for g in range(1, N_GROUPS):
        better = gs[g] > bestv
        best = jnp.where(better, g, best)
        bestv = jnp.where(better, gs[g], bestv)
    in_g = [best == g for g in range(N_GROUPS)]

    def pick(rows, j):
        out = rows[j]
        for g in range(1, N_GROUPS):
            out = jnp.where(in_g[g], rows[EXPERTS_PER_GROUP * g + j], out)
        return out

    vsel = [pick(sel_r, j) for j in range(EXPERTS_PER_GROUP)]
    vsc = [pick(score_r, j) for j in range(EXPERTS_PER_GROUP)]
    gates = []
    for j in range(EXPERTS_PER_GROUP):
        beaten = jnp.zeros((1, ts), I32)
        for k in range(EXPERTS_PER_GROUP):
            if k == j:
                continue
            wins = (vsel[k] > vsel[j]) | ((vsel[k] == vsel[j]) & (k < j))
            beaten = beaten + wins.astype(I32)
        gates.append(jnp.where(beaten < 2, vsc[j], 0.0))
    gsum = gates[0] + gates[1] + gates[2] + gates[3]
    gates = [g / gsum for g in gates]

    sub = lax.broadcasted_iota(I32, (SUBLANES, ts), 0)
    oh8 = jnp.zeros((SUBLANES, ts), F32)
    for g in range(N_GROUPS):
        oh8 = jnp.where((sub == g) & in_g[g], 1.0, oh8)
    r_i = lax.broadcasted_iota(I32, (ts, ts), 0)
    c_i = lax.broadcasted_iota(I32, (ts, ts), 1)
    upper = jnp.where(r_i < c_i, 1.0, 0.0).astype(BF16)
    excl = _dot(oh8.astype(BF16), upper)
    n8 = jnp.sum(oh8, axis=1, keepdims=True)
    len8 = jnp.floor((n8 + (RUN_ALIGN - 1)) * (1.0 / RUN_ALIGN)) * RUN_ALIGN
    sub1 = lax.broadcasted_iota(I32, (SUBLANES, 1), 0)
    off8 = jnp.zeros((SUBLANES, 1), F32)
    run_off = jnp.zeros((1, 1), F32)
    for g in range(N_GROUPS):
        off8 = jnp.where(sub1 == g, run_off, off8)
        run_off = run_off + len8[g:g + 1, :]
    pos = jnp.sum(jnp.where(oh8 > 0.0, off8 + excl, 0.0), axis=0, keepdims=True)

    t8 = jnp.where(sub == STAGE_LANE, pos, 0.0)
    for j in range(EXPERTS_PER_GROUP):
        t8 = jnp.where(sub == j, gates[j], t8)
    tail = jnp.concatenate([t8, jnp.zeros((TAIL_LANES - SUBLANES, ts), F32)], axis=0).T
    tail_ref[lane] = tail
    yield

    srow = lax.broadcasted_iota(I32, (STAGE_ROWS, ts), 0)
    sort = jnp.where(srow == pos.astype(I32), 1.0, 0.0).astype(BF16)
    xs_sorted = _dot(sort, xh).astype(BF16)
    pieces = _dot(sort, jnp.concatenate(_split3(tail), axis=1))
    tail_sorted = (pieces[:, :LANES] + pieces[:, LANES:2 * LANES]) + pieces[:, 2 * LANES:]
    results[lane] = (len8, off8, run_off, xs_sorted, tail_sorted)
    yield


def _xattn_router(x, mem, wq, wkv, wo, lng, lnb, rw2, rbias, cap, blk):
    bsz, seq, d = x.shape
    mlen = mem.shape[1]
    ts = min(SEQ_TILE, seq)
    assert ts == SEQ_TILE and blk % RUN_ALIGN == 0
    ns = seq // ts
    nl = XATTN_SEQS
    const = lambda a: pl.BlockSpec(a.shape, lambda b, s: (0,) * a.ndim)
    zero_rows = blk
    return pl.pallas_call(
        functools.partial(_xattn_kernel, ts=ts, cap=cap, blk=blk),
        out_shape=(
            jax.ShapeDtypeStruct((bsz, seq, d), F32),
            jax.ShapeDtypeStruct((bsz, seq, TAIL_LANES), F32),
            jax.ShapeDtypeStruct((bsz, ns, SUBLANES, LANES), I32),
            jax.ShapeDtypeStruct((N_GROUPS * cap, d), BF16),
            jax.ShapeDtypeStruct((N_GROUPS * cap, TAIL_LANES), F32),
        ),
        grid=(bsz // nl, ns),
        in_specs=[
            pl.BlockSpec((nl, ts, d), lambda b, s: (b, s, 0)),
            pl.BlockSpec((nl, mlen, d), lambda b, s: (b, 0, 0)),
            const(wq), const(wkv), const(wo), const(lng), const(lnb),
            const(rw2), const(rbias),
        ],
        out_specs=(
            pl.BlockSpec((nl, ts, d), lambda b, s: (b, s, 0)),
            pl.BlockSpec((nl, ts, TAIL_LANES), lambda b, s: (b, s, 0)),
            pl.BlockSpec((nl, 1, SUBLANES, LANES), lambda b, s: (b, s, 0, 0)),
            pl.BlockSpec(memory_space=pl.ANY),
            pl.BlockSpec(memory_space=pl.ANY),
        ),
        scratch_shapes=[
            pltpu.VMEM((nl, mlen, d), BF16),
            pltpu.VMEM((nl, mlen, d), BF16),
            pltpu.VMEM((SUBLANES, LANES), F32),
            pltpu.VMEM((nl, STAGE_ROWS, d), BF16),
            pltpu.VMEM((nl, STAGE_ROWS, TAIL_LANES), F32),
            pltpu.VMEM((nl * SUBLANES, LANES), I32),
            pltpu.SMEM((nl * SUBLANES, LANES), I32),
            pltpu.SMEM((nl,), I32),
            pltpu.VMEM((zero_rows, d), BF16),
            pltpu.VMEM((zero_rows, TAIL_LANES), F32),
            pltpu.SemaphoreType.DMA((nl, 2)),
            pltpu.SemaphoreType.DMA,
            pltpu.SemaphoreType.DMA((2,)),
        ],
        compiler_params=pltpu.CompilerParams(
            dimension_semantics=("arbitrary", "arbitrary"), vmem_limit_bytes=VMEM_LIMIT,
            has_side_effects=True),
        name="xattn_router",
    )(x, mem, wq, wkv, wo, lng, lnb, rw2, rbias)


def _ffn_kernel(blk_in_ref, grp_ref, used_ref, xs_ref, tl_ref, wg_ref, wu_ref, wd_ref, o_ref):
    @pl.when(used_ref[pl.program_id(0)] == 1)
    def _():
        xb = xs_ref[...]
        parts = []
        for j in range(EXPERTS_PER_GROUP):
            hg = _dot(xb, wg_ref[j])
            hu = _dot(xb, wu_ref[j])
            gate = tl_ref[:, j:j + 1]
            parts.append(jnp.where(gate != 0.0, hg * _sigmoid(hg) * hu * gate, 0.0))
        hid = jnp.concatenate(parts, axis=1).astype(BF16)
        o_ref[...] = _dot(hid, wd_ref[0]).astype(BF16)


def _ffn(blk_in, blk_grp, used, xs, tls, wg, wu, wd):
    d = xs.shape[1]
    blk = FFN_BLOCK
    grid_spec = pltpu.PrefetchScalarGridSpec(
        num_scalar_prefetch=3,
        grid=(blk_in.shape[0],),
        in_specs=[
            pl.BlockSpec((blk, d), lambda i, bi, grp, us: (bi[i], 0)),
            pl.BlockSpec((blk, TAIL_LANES), lambda i, bi, grp, us: (bi[i], 0)),
            pl.BlockSpec((EXPERTS_PER_GROUP,) + wg.shape[1:], lambda i, bi, grp, us: (grp[i], 0, 0)),
            pl.BlockSpec((EXPERTS_PER_GROUP,) + wu.shape[1:], lambda i, bi, grp, us: (grp[i], 0, 0)),
            pl.BlockSpec((1,) + wd.shape[1:], lambda i, bi, grp, us: (grp[i], 0, 0)),
        ],
        out_specs=pl.BlockSpec((blk, d), lambda i, bi, grp, us: (bi[i], 0)),
    )
    return pl.pallas_call(
        _ffn_kernel,
        out_shape=jax.ShapeDtypeStruct(xs.shape, BF16),
        grid_spec=grid_spec,
        compiler_params=pltpu.CompilerParams(
            dimension_semantics=("arbitrary",), vmem_limit_bytes=VMEM_LIMIT),
        name="group_ffn",
    )(blk_in, blk_grp, used, xs, tls, wg, wu, wd)


def _block_tables(seg_rows, cap, blk, n_steps):
    nblk = (seg_rows + blk - 1) // blk
    bend = jnp.cumsum(nblk)
    bstart = bend - nblk
    step = jnp.arange(n_steps, dtype=I32)
    used = step < bend[-1]
    grp = jnp.minimum(jnp.sum(step[:, None] >= bend[None, :], axis=1), N_GROUPS - 1).astype(I32)
    blk_in = grp * (cap // blk) + step - bstart[grp]
    last_real = jnp.maximum(bend[-1] - 1, 0)
    blk_in = jnp.where(used, blk_in, blk_in[last_real])
    grp = jnp.where(used, grp, grp[last_real])
    return blk_in.astype(I32), grp.astype(I32), used.astype(I32)


def _flat_meta(meta):
    m = meta.reshape(-1, SUBLANES, LANES)[:, :N_GROUPS, :]
    rec = jnp.concatenate([m[:, :, M_SLOT], m[:, :, M_LEN], m[:, :, M_OFF], m[:, :1, M_TOTAL],
                           jnp.zeros((m.shape[0], META_W - 3 * N_GROUPS - 1), I32)], axis=1)
    return rec.reshape(-1)


def kernel(x, mem, w_in, b_i, b_f, conv_qk, head_norm_g, pool_w, pool_scale, w_mix_out,
           ln_mix_g, ln_mix_b, w_xq, w_xkv, w_xo, ln_x_g, ln_x_b, router_w, router_bias,
           w_gate, w_up, w_down, ln_moe_g, ln_moe_b):
    bsz, seq, d = x.shape
    n_tok = bsz * seq
    n_tiles = n_tok // SEQ_TILE
    mw = N_HEADS * HEAD_DIM
    n_gate = 2 * N_HEADS
    blk = FFN_BLOCK
    cap = -(-(n_tok + RUN_ALIGN * n_tiles) // blk) * blk
    n_steps = (n_tok + N_GROUPS * RUN_ALIGN * n_tiles) // blk + N_GROUPS

    rw = jnp.pad(router_w, ((0, 0), (0, LANES - N_EXPERTS)))
    rwh, rwm, _ = _split3(rw)
    rw2 = jnp.concatenate([rwh, rwm], axis=1)
    rbias = router_bias.reshape(N_EXPERTS, 1).astype(F32)
    row = lambda v: v.reshape(1, -1).astype(F32)

    combine = None
    for l in range(DEPTH):
        wa = w_in[l][:, :4 * mw].astype(BF16)
        wu = w_in[l][:, 4 * mw + n_gate:].astype(BF16)
        wif = jnp.pad(w_in[l][:, 4 * mw:4 * mw + n_gate], ((0, 0), (0, LANES - n_gate))).astype(BF16)
        bif = jnp.pad(jnp.concatenate([b_i[l], b_f[l]]), (0, LANES - n_gate)).reshape(1, LANES)
        weights = (wa, wu, wif, bif, conv_qk[l], row(head_norm_g[l]), pool_w[l].astype(BF16),
                   row(pool_scale[l]), w_mix_out[l].astype(BF16), row(ln_mix_g[l]), row(ln_mix_b[l]))
        x = _mixer(x, weights, combine)

        x2, tail, meta, xs, tls = _xattn_router(
            x, mem, w_xq[l].astype(BF16), w_xkv[l].astype(BF16), w_xo[l].astype(BF16),
            row(ln_x_g[l]), row(ln_x_b[l]), rw2, rbias, cap, blk)

        seg_rows = meta[-1, -1, :N_GROUPS, M_END]
        blk_in, blk_grp, used = _block_tables(seg_rows, cap, blk, n_steps)
        ys = _ffn(blk_in, blk_grp, used, xs, tls, w_gate[l].astype(BF16), w_up[l].astype(BF16),
                  w_down[l].reshape(N_GROUPS, EXPERTS_PER_GROUP * w_down.shape[2], d).astype(BF16))
        x = x2
        combine = (_flat_meta(meta), tail, ys, row(ln_moe_g[l]), row(ln_moe_b[l]))

    meta, tail, ys, cg, cb = combine
    return _final_combine(meta, x.reshape(n_tok, d), tail.reshape(n_tok, TAIL_LANES), ys, cg, cb).reshape(bsz, seq, d)
```

```python
import functools

import jax
import jax.numpy as jnp
from jax import lax
from jax.experimental import pallas as pl
from jax.experimental.pallas import tpu as pltpu

F32 = jnp.float32
BF16 = jnp.bfloat16
I32 = jnp.int32

N_HEADS = 4
HEAD_DIM = 128
POOL_WINDOWS = (2, 4, 8, 16)
POOL_GROUP = 128
CONV_WIDTH = 4
XATTN_HEADS = 4
N_EXPERTS = 16
N_GROUPS = 4
EXPERTS_PER_GROUP = 4
DEPTH = 2
ALPHA = (2 * DEPTH) ** 0.25
LN_EPS = 1e-5

LANES = 128
SUBLANES = 8
BF16_TILE_ROWS = 16
VMEM_LIMIT = 56 * 1024 * 1024

SEQ_TILE = 256
MIXER_SEQS = 2
XATTN_SEQS = 4
FINAL_TILES = 2
HEADS_TOGETHER = 2
MLSTM_CHUNK = 256
FFN_BLOCK = 512
CONV_CARRY = 8
POOL_CARRY = 16

RUN_ALIGN = BF16_TILE_ROWS
STAGE_ROWS = SEQ_TILE + N_GROUPS * RUN_ALIGN
STAGE_ROWS_PADDED = 384
TAIL_LANES = 128
STAGE_LANE = 5
M_SLOT, M_LEN, M_OFF, M_TOTAL, M_END = 0, 1, 2, 3, 4
META_W = 16


def _layer_norm(y, g, b):
    mu = jnp.mean(y, axis=-1, keepdims=True)
    d = y - mu
    var = jnp.mean(d * d, axis=-1, keepdims=True)
    return d * lax.rsqrt(var + LN_EPS) * g + b


def _sigmoid(v):
    return 1.0 / (1.0 + jnp.exp(-v))


def _dot(a, b):
    return jnp.dot(a, b, preferred_element_type=F32)


def _dot_nt(a, b):
    return lax.dot_general(a, b, (((1,), (1,)), ((), ())), preferred_element_type=F32)


def _split3(v):
    hi = v.astype(BF16)
    r1 = v - hi.astype(F32)
    mid = r1.astype(BF16)
    lo = (r1 - mid.astype(F32)).astype(BF16)
    return hi, mid, lo


def _loop(n, body, unroll=1):
    lax.fori_loop(0, n, lambda j, c: (body(j), c)[1], 0, unroll=unroll)


def _aligned(v):
    return pl.multiple_of(v, RUN_ALIGN)


def _run_copy(src, src_row, dst, dst_row, length, sem):
    length = _aligned(length)

    @pl.when(length > 0)
    def _():
        pltpu.make_async_copy(src.at[pl.ds(_aligned(src_row), length)],
                              dst.at[pl.ds(_aligned(dst_row), length)], sem).start()


def _combine_fetch(meta_ref, tile, next_tile, first, has_next, slot, ys_hbm, stage, sem):
    def fetch(t, sl):
        for g in range(N_GROUPS):
            _run_copy(ys_hbm, meta_ref[t * META_W + g],
                      stage.at[sl], meta_ref[t * META_W + 2 * N_GROUPS + g],
                      meta_ref[t * META_W + N_GROUPS + g], sem.at[sl])

    @pl.when(first)
    def _():
        stage[...] = jnp.zeros(stage.shape, stage.dtype)
        fetch(tile, slot)

    @pl.when(has_next)
    def _():
        fetch(next_tile, 1 - slot)

    total = _aligned(meta_ref[tile * META_W + 3 * N_GROUPS])

    @pl.when(total > 0)
    def _():
        pltpu.make_async_copy(ys_hbm.at[pl.ds(0, total)], stage.at[slot, pl.ds(0, total)],
                              sem.at[slot]).wait()


def _combine(x2, tail, sorted_rows, lng, lnb):
    ts = x2.shape[0]
    pos = tail[:, STAGE_LANE:STAGE_LANE + 1].astype(I32)
    lane = lax.broadcasted_iota(I32, (ts, STAGE_ROWS_PADDED), 1)
    unsort = jnp.where(lane == pos, 1.0, 0.0).astype(BF16)
    return _layer_norm(ALPHA * x2 + _dot(unsort, sorted_rows), lng, lnb)


def _combine_scratch(d, lanes):
    return [pltpu.VMEM((lanes, 2, STAGE_ROWS_PADDED, d), BF16), pltpu.SemaphoreType.DMA((lanes, 2))]


def _final_kernel(meta_ref, x2_ref, tail_ref, ys_hbm, lng_ref, lnb_ref, o_ref, stage, sem):
    i = pl.program_id(0)
    ts = o_ref.shape[0] // FINAL_TILES
    for lane in range(FINAL_TILES):
        tile = i * FINAL_TILES + lane
        _combine_fetch(meta_ref, tile, tile + FINAL_TILES, i == 0, i + 1 < pl.num_programs(0), i % 2,
                       ys_hbm, stage.at[lane], sem.at[lane])
    for lane in range(FINAL_TILES):
        rows = slice(lane * ts, (lane + 1) * ts)
        o_ref[rows, :] = _combine(x2_ref[rows, :], tail_ref[rows, :], stage[lane, i % 2],
                                  lng_ref[...], lnb_ref[...])


def _final_combine(meta, x2, tail, ys, lng, lnb):
    n_tok, d = x2.shape
    ts = FINAL_TILES * (tail.shape[0] // (meta.shape[0] // META_W))
    grid_spec = pltpu.PrefetchScalarGridSpec(
        num_scalar_prefetch=1,
        grid=(n_tok // ts,),
        in_specs=[
            pl.BlockSpec((ts, d), lambda i, m: (i, 0)),
            pl.BlockSpec((ts, TAIL_LANES), lambda i, m: (i, 0)),
            pl.BlockSpec(memory_space=pl.ANY),
            pl.BlockSpec(lng.shape, lambda i, m: (0, 0)),
            pl.BlockSpec(lnb.shape, lambda i, m: (0, 0)),
        ],
        out_specs=pl.BlockSpec((ts, d), lambda i, m: (i, 0)),
        scratch_shapes=_combine_scratch(d, FINAL_TILES),
    )
    return pl.pallas_call(
        _final_kernel,
        out_shape=jax.ShapeDtypeStruct((n_tok, d), F32),
        grid_spec=grid_spec,
        compiler_params=pltpu.CompilerParams(
            dimension_semantics=("arbitrary",), vmem_limit_bytes=VMEM_LIMIT),
        name="final_combine",
    )(meta, x2, tail, ys, lng, lnb)


def _mixer_kernel(*refs, ts, lc, combine):
    n_lead = 6 if combine else 1
    zq_ext, u_carry, c_st, m_st = refs[n_lead + 12:n_lead + 16]
    b = pl.program_id(0)
    s = pl.program_id(1)
    ns = pl.num_programs(1)

    @pl.when(s == 0)
    def _():
        zq_ext[...] = jnp.zeros(zq_ext.shape, F32)
        u_carry[...] = jnp.zeros(u_carry.shape, F32)
        c_st[...] = jnp.zeros(c_st.shape, F32)
        m_st[...] = jnp.zeros(m_st.shape, F32)

    step = b * ns + s
    if combine:
        meta_ref, ys_hbm = refs[0], refs[3]
        stage, sem = refs[n_lead + 16:n_lead + 18]
        for lane in range(MIXER_SEQS):
            tile = (b * MIXER_SEQS + lane) * ns + s
            next_tile = jnp.where(s + 1 < ns, tile + 1, tile + (MIXER_SEQS - 1) * ns + 1)
            _combine_fetch(meta_ref, tile, next_tile, step == 0, step + 1 < pl.num_programs(0) * ns,
                           step % 2, ys_hbm, stage.at[lane], sem.at[lane])

    lanes = [_mixer_lane(lane, step % 2, refs, ts, lc, combine) for lane in range(MIXER_SEQS)]
    for _ in zip(*lanes):
        pass


def _mixer_lane(lane, slot, refs, ts, lc, combine):
    if combine:
        (_, x_ref, tail_ref, _, cg_ref, cb_ref), refs = refs[:6], refs[6:]
    else:
        x_ref, refs = refs[0], refs[1:]
    (wa_ref, wu_ref, wif_ref, bif_ref, conv_ref, hng_ref, poolw_ref, pscale_ref, wout_ref,
     lng_ref, lnb_ref, o_ref, zq_ext, u_carry, c_st, m_st) = refs[:16]
    zq_ext, u_carry, c_st, m_st = zq_ext.at[lane], u_carry.at[lane], c_st.at[lane], m_st.at[lane]
    s = pl.program_id(1)
    mw = N_HEADS * HEAD_DIM

    x = x_ref[lane]
    if combine:
        stage = refs[16]
        x = _combine(x, tail_ref[lane], stage[lane, slot], cg_ref[...], cb_ref[...])
    xb = x.astype(BF16)
    u = _dot(xb, wu_ref[...])
    gts = _dot(xb, wif_ref[...]) + bif_ref[...]
    yield

    cw = conv_ref[...]
    projected = {}

    def project(h):
        if h not in projected:
            zh = _dot(xb, wa_ref[:, 4 * HEAD_DIM * h:4 * HEAD_DIM * (h + 1)])
            cs2 = slice(2 * HEAD_DIM * h, 2 * HEAD_DIM * (h + 1))
            zq = zh[:, :2 * HEAD_DIM]
            ze = jnp.concatenate([zq_ext[:, cs2], zq], axis=0)
            zq_ext[:, cs2] = zq[ts - CONV_CARRY:ts, :]
            acc = zq * cw[CONV_WIDTH - 1:CONV_WIDTH, cs2]
            for j in range(1, CONV_WIDTH):
                acc = acc + pltpu.roll(ze, j, 0)[CONV_CARRY:, :] * cw[CONV_WIDTH - 1 - j:CONV_WIDTH - j, cs2]
            qk_h = acc * _sigmoid(acc)
            projected[h] = (qk_h[:, :HEAD_DIM] * (HEAD_DIM ** -0.5), qk_h[:, HEAD_DIM:],
                            zh[:, 2 * HEAD_DIM:3 * HEAD_DIM], zh[:, 3 * HEAD_DIM:])
        return projected[h]

    lf_all = jnp.minimum(gts, 0.0) - jnp.log1p(jnp.exp(-jnp.abs(gts)))
    yield

    row_i = lax.broadcasted_iota(I32, (lc, lc), 0)
    col_i = lax.broadcasted_iota(I32, (lc, lc), 1)
    causal = col_i <= row_i
    tri = jnp.where(causal, 1.0, 0.0).astype(BF16)
    ones_col = jnp.where(lax.broadcasted_iota(I32, (lc, HEAD_DIM), 1) == 0, 1.0, 0.0).astype(BF16)

    head_out = [[] for _ in range(N_HEADS)]
    for c in range(ts // lc):
        rows = slice(c * lc, (c + 1) * lc)
        hi, mid, lo = _split3(lf_all[rows, :])
        b_all = _dot(tri, hi) + _dot(tri, mid) + _dot(tri, lo)
        g_c = gts[rows, :]
        r_all = g_c - pltpu.roll(b_all, LANES - N_HEADS, 1)
        r_t = r_all.T
        def head(h, rows=rows, b_all=b_all, g_c=g_c, r_t=r_t):
            hs = slice(h * HEAD_DIM, (h + 1) * HEAD_DIM)
            q_h, k_h, v_h, o_h = project(h)
            yield
            qh = q_h[rows, :]
            kh = k_h[rows, :]
            vh = jnp.concatenate([v_h[rows, :].astype(BF16), ones_col], axis=1)
            qhb = qh.astype(BF16)
            bc = b_all[:, N_HEADS + h:N_HEADS + h + 1]
            igc = g_c[:, h:h + 1]
            r_row = r_t[h:h + 1, :]
            c_prev = c_st[h]
            m_prev = m_st[h][:, 0:1]
            qk = _dot_nt(qhb, kh.astype(BF16))
            qc = _dot(qhb, c_prev.astype(BF16))
            yield

            log_d = jnp.where(causal, bc + r_row, -jnp.inf)
            m_intra = jnp.max(log_d, axis=1, keepdims=True)
            log_inter = bc + m_prev
            m_t = jnp.maximum(m_intra, log_inter)
            p = jnp.exp(log_d - m_t) * qk
            inter = jnp.exp(log_inter - m_t)
            b_last = bc[lc - 1:lc, :]
            w_state = b_last - bc + igc
            m_loc = jnp.max(w_state, axis=0, keepdims=True)
            ka = kh * jnp.exp(w_state - m_loc)
            yield

            nd = _dot(p.astype(BF16), vh) + inter * qc
            c_loc = _dot(ka.T.astype(BF16), vh)
            yield

            den = nd[:, HEAD_DIM:HEAD_DIM + 1]
            hh = nd[:, :HEAD_DIM] * (1.0 / jnp.maximum(jnp.abs(den), jnp.exp(-m_t)))
            m_new = jnp.maximum(b_last + m_prev, m_loc)
            s_old = jnp.exp(b_last + m_prev - m_new)
            s_new = jnp.exp(m_loc - m_new)
            c_st[h] = s_old * c_prev + s_new * c_loc
            m_st[h] = jnp.broadcast_to(m_new, (1, LANES))

            mu = jnp.mean(hh, axis=1, keepdims=True)
            dlt = hh - mu
            var = jnp.mean(dlt * dlt, axis=1, keepdims=True)
            hn = dlt * lax.rsqrt(var + LN_EPS) * hng_ref[:, hs]
            head_out[h].append(hn * _sigmoid(o_h[rows, :]))
            yield

        for h0 in range(0, N_HEADS, HEADS_TOGETHER):
            for _ in zip(*[head(h) for h in range(h0, h0 + HEADS_TOGETHER)]):
                yield

    mixed = [jnp.concatenate(ho, axis=0) if len(ho) > 1 else ho[0] for ho in head_out]

    ue = jnp.concatenate([u_carry[...], u], axis=0)
    u_carry[...] = u[ts - POOL_CARRY:ts, :]
    pos = (lax.broadcasted_iota(I32, (ts, 1), 0) + s * ts + 1).astype(F32)
    for g, w in enumerate(POOL_WINDOWS):
        cs = slice(g * POOL_GROUP, (g + 1) * POOL_GROUP)
        win = ue[:, cs]
        shift = 1
        while shift < w:
            win = win + pltpu.roll(win, shift, 0)
            shift *= 2
        ug = u[:, cs]
        pooled = win[POOL_CARRY:, :] / jnp.minimum(pos, float(w)) - ug
        pm = _dot(pooled.astype(BF16), poolw_ref[g]) * pscale_ref[:, cs]
        mixed.append(pm)
    yield

    mixed = jnp.concatenate(mixed, axis=1).astype(BF16)
    y = _dot(mixed, wout_ref[...])
    o_ref[lane] = _layer_norm(ALPHA * x + y, lng_ref[...], lnb_ref[...])
    yield


def _mixer(x, weights, combine=None):
    bsz, seq, d = x.shape
    ts = min(SEQ_TILE, seq)
    lc = min(MLSTM_CHUNK, ts)
    ns = seq // ts
    mw = N_HEADS * HEAD_DIM
    pw = weights[1].shape[1]
    const = lambda a: pl.BlockSpec(a.shape, lambda b, s, *_: (0,) * a.ndim)
    in_specs = [pl.BlockSpec((MIXER_SEQS, ts, d), lambda b, s, *_: (b, s, 0))]
    args = [x]
    scratch = [
        pltpu.VMEM((MIXER_SEQS, CONV_CARRY, 2 * mw), F32),
        pltpu.VMEM((MIXER_SEQS, POOL_CARRY, pw), F32),
        pltpu.VMEM((MIXER_SEQS, N_HEADS, HEAD_DIM, 2 * HEAD_DIM), F32),
        pltpu.VMEM((MIXER_SEQS, N_HEADS, 1, LANES), F32),
    ]
    prefetch = []
    if combine is not None:
        meta, tail, ys, cg, cb = combine
        prefetch = [meta]
        in_specs += [pl.BlockSpec((MIXER_SEQS, ts, TAIL_LANES), lambda b, s, *_: (b, s, 0)),
                     pl.BlockSpec(memory_space=pl.ANY), const(cg), const(cb)]
        args += [tail, ys, cg, cb]
        scratch += _combine_scratch(d, MIXER_SEQS)
    in_specs += [const(w) for w in weights]
    args += list(weights)
    grid_spec = pltpu.PrefetchScalarGridSpec(
        num_scalar_prefetch=len(prefetch),
        grid=(bsz // MIXER_SEQS, ns),
        in_specs=in_specs,
        out_specs=pl.BlockSpec((MIXER_SEQS, ts, d), lambda b, s, *_: (b, s, 0)),
        scratch_shapes=scratch,
    )
    return pl.pallas_call(
        functools.partial(_mixer_kernel, ts=ts, lc=lc, combine=combine is not None),
        out_shape=jax.ShapeDtypeStruct((bsz, seq, d), F32),
        grid_spec=grid_spec,
        compiler_params=pltpu.CompilerParams(
            dimension_semantics=("arbitrary", "arbitrary"), vmem_limit_bytes=VMEM_LIMIT),
        name="mixer",
    )(*prefetch, *args)


def _top2_sum(a, b, c, d):
    hi1, lo1 = jnp.maximum(a, b), jnp.minimum(a, b)
    hi2, lo2 = jnp.maximum(c, d), jnp.minimum(c, d)
    return jnp.maximum(hi1, hi2) + jnp.maximum(jnp.minimum(hi1, hi2), jnp.maximum(lo1, lo2))


def _xattn_kernel(x_ref, mem_ref, wq_ref, wkv_ref, wo_ref, lng_ref, lnb_ref,
                  rw2_ref, rbias_ref,
                  x2_ref, tail_ref, meta_ref, xs_hbm, tls_hbm,
                  k_scr, v_scr, carry, stx, stt, mvec, msm, prev_total, zx, zt, sem, ssem, zsem,
                  *, ts, cap, blk):
    b = pl.program_id(0)
    s = pl.program_id(1)
    step = b * pl.num_programs(1) + s
    last = step == pl.num_programs(0) * pl.num_programs(1) - 1
    d = x_ref.shape[2]

    @pl.when(s == 0)
    def _():
        for lane in range(XATTN_SEQS):
            kv = _dot(mem_ref[lane].astype(BF16), wkv_ref[...])
            k_scr[lane] = kv[:, :d].astype(BF16)
            v_scr[lane] = kv[:, d:].astype(BF16)

    @pl.when(step == 0)
    def _():
        carry[...] = jnp.zeros(carry.shape, F32)

    results = [None] * XATTN_SEQS
    lanes = [_xattn_lane(lane, results, x_ref, wq_ref, wo_ref, lng_ref, lnb_ref, rw2_ref, rbias_ref,
                         x2_ref, tail_ref, k_scr, v_scr, ts) for lane in range(XATTN_SEQS)]
    for _ in zip(*lanes):
        pass

    sub1 = lax.broadcasted_iota(I32, (SUBLANES, 1), 0)
    lane8 = lax.broadcasted_iota(I32, (SUBLANES, LANES), 1)
    base = carry[:, 0:1]
    for lane in range(XATTN_SEQS):
        len8, off8, total, _, _ = results[lane]
        new_base = base + len8
        mv = jnp.where(lane8 == M_SLOT, sub1.astype(F32) * float(cap) + base, 0.0)
        mv = jnp.where(lane8 == M_LEN, len8, mv)
        mv = jnp.where(lane8 == M_OFF, off8, mv)
        mv = jnp.where(lane8 == M_TOTAL, total, mv)
        mv = jnp.where(lane8 == M_END, new_base, mv).astype(I32)
        meta_ref[lane, 0] = mv
        mvec[lane * SUBLANES:(lane + 1) * SUBLANES, :] = mv
        base = new_base
    carry[...] = jnp.broadcast_to(base, carry.shape)

    def wait_runs(lane, n_rows):
        pltpu.make_async_copy(stx.at[lane, pl.ds(0, n_rows)], xs_hbm.at[pl.ds(0, n_rows)],
                              sem.at[lane, 0]).wait()
        pltpu.make_async_copy(stt.at[lane, pl.ds(0, n_rows)], tls_hbm.at[pl.ds(0, n_rows)],
                              sem.at[lane, 1]).wait()

    @pl.when(step > 0)
    def _():
        for lane in range(XATTN_SEQS):
            n_prev = _aligned(prev_total[lane])

            @pl.when(n_prev > 0)
            def _(lane=lane, n_prev=n_prev):
                wait_runs(lane, n_prev)

    for lane in range(XATTN_SEQS):
        stx[lane] = results[lane][3]
        stt[lane] = results[lane][4]
    to_smem = pltpu.make_async_copy(mvec, msm, ssem)
    to_smem.start()
    to_smem.wait()

    for lane in range(XATTN_SEQS):
        for g in range(N_GROUPS):
            r = lane * SUBLANES + g
            _run_copy(stx.at[lane], msm[r, M_OFF], xs_hbm, msm[r, M_SLOT], msm[r, M_LEN], sem.at[lane, 0])
            _run_copy(stt.at[lane], msm[r, M_OFF], tls_hbm, msm[r, M_SLOT], msm[r, M_LEN], sem.at[lane, 1])
        prev_total[lane] = msm[lane * SUBLANES, M_TOTAL]

    @pl.when(last)
    def _():
        for lane in range(XATTN_SEQS):
            n_own = _aligned(msm[lane * SUBLANES, M_TOTAL])

            @pl.when(n_own > 0)
            def _(lane=lane, n_own=n_own):
                wait_runs(lane, n_own)

        zx[...] = jnp.zeros(zx.shape, BF16)
        zt[...] = jnp.zeros(zt.shape, F32)
        for g in range(N_GROUPS):
            end = msm[(XATTN_SEQS - 1) * SUBLANES + g, M_END]
            n_pad = _aligned((blk - end % blk) % blk)
            _run_copy(zx, 0, xs_hbm, g * cap + end, n_pad, zsem.at[0])
            _run_copy(zt, 0, tls_hbm, g * cap + end, n_pad, zsem.at[1])

            @pl.when(n_pad > 0)
            def _(n_pad=n_pad):
                pltpu.make_async_copy(zx.at[pl.ds(0, n_pad)], xs_hbm.at[pl.ds(0, n_pad)], zsem.at[0]).wait()
                pltpu.make_async_copy(zt.at[pl.ds(0, n_pad)], tls_hbm.at[pl.ds(0, n_pad)], zsem.at[1]).wait()


def _xattn_lane(lane, results, x_ref, wq_ref, wo_ref, lng_ref, lnb_ref, rw2_ref, rbias_ref,
                x2_ref, tail_ref, k_scr, v_scr, ts):
    d = x_ref.shape[2]
    dh = d // XATTN_HEADS
    x = x_ref[lane]
    q = (_dot(x.astype(BF16), wq_ref[...]) * (dh ** -0.5)).astype(BF16)
    yield
    outs = []
    for h in range(XATTN_HEADS):
        hs = slice(h * dh, (h + 1) * dh)
        sc = _dot_nt(q[:, hs], k_scr[lane, :, hs])
        e = jnp.exp(sc - jnp.max(sc, axis=1, keepdims=True))
        l = jnp.sum(e, axis=1, keepdims=True)
        outs.append(_dot(e.astype(BF16), v_scr[lane, :, hs]) * (1.0 / l))
        yield
    o = jnp.concatenate(outs, axis=1).astype(BF16)
    x2 = _layer_norm(ALPHA * x + _dot(o, wo_ref[...]), lng_ref[...], lnb_ref[...])
    x2_ref[lane] = x2
    yield

    xh, xm, _ = _split3(x2)
    both = _dot(xh, rw2_ref[...])
    logits = (both[:, :LANES] + both[:, LANES:]) + _dot(xm, rw2_ref[:, :LANES])
    yield
    lt = logits.T[0:N_EXPERTS, :]
    score = _sigmoid(lt)
    sel = score + rbias_ref[...]

    sel_r = [sel[e:e + 1, :] for e in range(N_EXPERTS)]
    score_r = [score[e:e + 1, :] for e in range(N_EXPERTS)]
    gs = [_top2_sum(*sel_r[EXPERTS_PER_GROUP * g:EXPERTS_PER_GROUP * (g + 1)]) for g in range(N_GROUPS)]
    best = jnp.zeros((1, ts), I32)
    bestv = gs[0]
    for g in range(1, N_GROUPS):
        better = gs[g] > bestv
        best = jnp.where(better, g, best)
        bestv = jnp.where(better, gs[g], bestv)
    in_g = [best == g for g in range(N_GROUPS)]

    def pick(rows, j):
        out = rows[j]
        for g in range(1, N_GROUPS):
            out = jnp.where(in_g[g], rows[EXPERTS_PER_GROUP * g + j], out)
        return out

    vsel = [pick(sel_r, j) for j in range(EXPERTS_PER_GROUP)]
    vsc = [pick(score_r, j) for j in range(EXPERTS_PER_GROUP)]
    gates = []
    for j in range(EXPERTS_PER_GROUP):
        beaten = jnp.zeros((1, ts), I32)
        for k in range(EXPERTS_PER_GROUP):
            if k == j:
                continue
            wins = (vsel[k] > vsel[j]) | ((vsel[k] == vsel[j]) & (k < j))
            beaten = beaten + wins.astype(I32)
        gates.append(jnp.where(beaten < 2, vsc[j], 0.0))
    gsum = gates[0] + gates[1] + gates[2] + gates[3]
    gates = [g / gsum for g in gates]

    sub = lax.broadcasted_iota(I32, (SUBLANES, ts), 0)
    oh8 = jnp.zeros((SUBLANES, ts), F32)
    for g in range(N_GROUPS):
        oh8 = jnp.where((sub == g) & in_g[g], 1.0, oh8)
    r_i = lax.broadcasted_iota(I32, (ts, ts), 0)
    c_i = lax.broadcasted_iota(I32, (ts, ts), 1)
    upper = jnp.where(r_i < c_i, 1.0, 0.0).astype(BF16)
    excl = _dot(oh8.astype(BF16), upper)
    n8 = jnp.sum(oh8, axis=1, keepdims=True)
    len8 = jnp.floor((n8 + (RUN_ALIGN - 1)) * (1.0 / RUN_ALIGN)) * RUN_ALIGN
    sub1 = lax.broadcasted_iota(I32, (SUBLANES, 1), 0)
    off8 = jnp.zeros((SUBLANES, 1), F32)
    run_off = jnp.zeros((1, 1), F32)
    for g in range(N_GROUPS):
        off8 = jnp.where(sub1 == g, run_off, off8)
        run_off = run_off + len8[g:g + 1, :]
    pos = jnp.sum(jnp.where(oh8 > 0.0, off8 + excl, 0.0), axis=0, keepdims=True)

    t8 = jnp.where(sub == STAGE_LANE, pos, 0.0)
    for j in range(EXPERTS_PER_GROUP):
        t8 = jnp.where(sub == j, gates[j], t8)
    tail = jnp.concatenate([t8, jnp.zeros((TAIL_LANES - SUBLANES, ts), F32)], axis=0).T
    tail_ref[lane] = tail
    yield

    srow = lax.broadcasted_iota(I32, (STAGE_ROWS, ts), 0)
    sort = jnp.where(srow == pos.astype(I32), 1.0, 0.0).astype(BF16)
    xs_sorted = _dot(sort, xh).astype(BF16)
    pieces = _dot(sort, jnp.concatenate(_split3(tail), axis=1))
    tail_sorted = (pieces[:, :LANES] + pieces[:, LANES:2 * LANES]) + pieces[:, 2 * LANES:]
    results[lane] = (len8, off8, run_off, xs_sorted, tail_sorted)
    yield


def _xattn_router(x, mem, wq, wkv, wo, lng, lnb, rw2, rbias, cap, blk):
    bsz, seq, d = x.shape
    mlen = mem.shape[1]
    ts = min(SEQ_TILE, seq)
    assert ts == SEQ_TILE and blk % RUN_ALIGN == 0
    ns = seq // ts
    nl = XATTN_SEQS
    const = lambda a: pl.BlockSpec(a.shape, lambda b, s: (0,) * a.ndim)
    zero_rows = blk
    return pl.pallas_call(
        functools.partial(_xattn_kernel, ts=ts, cap=cap, blk=blk),
        out_shape=(
            jax.ShapeDtypeStruct((bsz, seq, d), F32),
            jax.ShapeDtypeStruct((bsz, seq, TAIL_LANES), F32),
            jax.ShapeDtypeStruct((bsz, ns, SUBLANES, LANES), I32),
            jax.ShapeDtypeStruct((N_GROUPS * cap, d), BF16),
            jax.ShapeDtypeStruct((N_GROUPS * cap, TAIL_LANES), F32),
        ),
        grid=(bsz // nl, ns),
        in_specs=[
            pl.BlockSpec((nl, ts, d), lambda b, s: (b, s, 0)),
            pl.BlockSpec((nl, mlen, d), lambda b, s: (b, 0, 0)),
            const(wq), const(wkv), const(wo), const(lng), const(lnb),
            const(rw2), const(rbias),
        ],
        out_specs=(
            pl.BlockSpec((nl, ts, d), lambda b, s: (b, s, 0)),
            pl.BlockSpec((nl, ts, TAIL_LANES), lambda b, s: (b, s, 0)),
            pl.BlockSpec((nl, 1, SUBLANES, LANES), lambda b, s: (b, s, 0, 0)),
            pl.BlockSpec(memory_space=pl.ANY),
            pl.BlockSpec(memory_space=pl.ANY),
        ),
        scratch_shapes=[
            pltpu.VMEM((nl, mlen, d), BF16),
            pltpu.VMEM((nl, mlen, d), BF16),
            pltpu.VMEM((SUBLANES, LANES), F32),
            pltpu.VMEM((nl, STAGE_ROWS, d), BF16),
            pltpu.VMEM((nl, STAGE_ROWS, TAIL_LANES), F32),
            pltpu.VMEM((nl * SUBLANES, LANES), I32),
            pltpu.SMEM((nl * SUBLANES, LANES), I32),
            pltpu.SMEM((nl,), I32),
            pltpu.VMEM((zero_rows, d), BF16),
            pltpu.VMEM((zero_rows, TAIL_LANES), F32),
            pltpu.SemaphoreType.DMA((nl, 2)),
            pltpu.SemaphoreType.DMA,
            pltpu.SemaphoreType.DMA((2,)),
        ],
        compiler_params=pltpu.CompilerParams(
            dimension_semantics=("arbitrary", "arbitrary"), vmem_limit_bytes=VMEM_LIMIT,
            has_side_effects=True),
        name="xattn_router",
    )(x, mem, wq, wkv, wo, lng, lnb, rw2, rbias)


def _ffn_kernel(blk_in_ref, grp_ref, used_ref, xs_ref, tl_ref, wg_ref, wu_ref, wd_ref, o_ref):
    @pl.when(used_ref[pl.program_id(0)] == 1)
    def _():
        xb = xs_ref[...]
        parts = []
        for j in range(EXPERTS_PER_GROUP):
            hg = _dot(xb, wg_ref[j])
            hu = _dot(xb, wu_ref[j])
            gate = tl_ref[:, j:j + 1]
            parts.append(jnp.where(gate != 0.0, hg * _sigmoid(hg) * hu * gate, 0.0))
        hid = jnp.concatenate(parts, axis=1).astype(BF16)
        o_ref[...] = _dot(hid, wd_ref[0]).astype(BF16)


def _ffn(blk_in, blk_grp, used, xs, tls, wg, wu, wd):
    d = xs.shape[1]
    blk = FFN_BLOCK
    grid_spec = pltpu.PrefetchScalarGridSpec(
        num_scalar_prefetch=3,
        grid=(blk_in.shape[0],),
        in_specs=[
            pl.BlockSpec((blk, d), lambda i, bi, grp, us: (bi[i], 0)),
            pl.BlockSpec((blk, TAIL_LANES), lambda i, bi, grp, us: (bi[i], 0)),
            pl.BlockSpec((EXPERTS_PER_GROUP,) + wg.shape[1:], lambda i, bi, grp, us: (grp[i], 0, 0)),
            pl.BlockSpec((EXPERTS_PER_GROUP,) + wu.shape[1:], lambda i, bi, grp, us: (grp[i], 0, 0)),
            pl.BlockSpec((1,) + wd.shape[1:], lambda i, bi, grp, us: (grp[i], 0, 0)),
        ],
        out_specs=pl.BlockSpec((blk, d), lambda i, bi, grp, us: (bi[i], 0)),
    )
    return pl.pallas_call(
        _ffn_kernel,
        out_shape=jax.ShapeDtypeStruct(xs.shape, BF16),
        grid_spec=grid_spec,
        compiler_params=pltpu.CompilerParams(
            dimension_semantics=("arbitrary",), vmem_limit_bytes=VMEM_LIMIT),
        name="group_ffn",
    )(blk_in, blk_grp, used, xs, tls, wg, wu, wd)


def _block_tables(seg_rows, cap, blk, n_steps):
    nblk = (seg_rows + blk - 1) // blk
    bend = jnp.cumsum(nblk)
    bstart = bend - nblk
    step = jnp.arange(n_steps, dtype=I32)
    used = step < bend[-1]
    grp = jnp.minimum(jnp.sum(step[:, None] >= bend[None, :], axis=1), N_GROUPS - 1).astype(I32)
    blk_in = grp * (cap // blk) + step - bstart[grp]
    last_real = jnp.maximum(bend[-1] - 1, 0)
    blk_in = jnp.where(used, blk_in, blk_in[last_real])
    grp = jnp.where(used, grp, grp[last_real])
    return blk_in.astype(I32), grp.astype(I32), used.astype(I32)


def _flat_meta(meta):
    m = meta.reshape(-1, SUBLANES, LANES)[:, :N_GROUPS, :]
    rec = jnp.concatenate([m[:, :, M_SLOT], m[:, :, M_LEN], m[:, :, M_OFF], m[:, :1, M_TOTAL],
                           jnp.zeros((m.shape[0], META_W - 3 * N_GROUPS - 1), I32)], axis=1)
    return rec.reshape(-1)


def kernel(x, mem, w_in, b_i, b_f, conv_qk, head_norm_g, pool_w, pool_scale, w_mix_out,
           ln_mix_g, ln_mix_b, w_xq, w_xkv, w_xo, ln_x_g, ln_x_b, router_w, router_bias,
           w_gate, w_up, w_down, ln_moe_g, ln_moe_b):
    bsz, seq, d = x.shape
    n_tok = bsz * seq
    n_tiles = n_tok // SEQ_TILE
    mw = N_HEADS * HEAD_DIM
    n_gate = 2 * N_HEADS
    blk = FFN_BLOCK
    cap = -(-(n_tok + RUN_ALIGN * n_tiles) // blk) * blk
    n_steps = (n_tok + N_GROUPS * RUN_ALIGN * n_tiles) // blk + N_GROUPS

    rw = jnp.pad(router_w, ((0, 0), (0, LANES - N_EXPERTS)))
    rwh, rwm, _ = _split3(rw)
    rw2 = jnp.concatenate([rwh, rwm], axis=1)
    rbias = router_bias.reshape(N_EXPERTS, 1).astype(F32)
    row = lambda v: v.reshape(1, -1).astype(F32)

    combine = None
    for l in range(DEPTH):
        wa = w_in[l][:, :4 * mw].reshape(d, 4, N_HEADS, HEAD_DIM).transpose(0, 2, 1, 3).reshape(
            d, 4 * mw).astype(BF16)
        conv_hm = conv_qk[l].reshape(CONV_WIDTH, 2, N_HEADS, HEAD_DIM).transpose(0, 2, 1, 3).reshape(
            CONV_WIDTH, 2 * mw)
        wu = w_in[l][:, 4 * mw + n_gate:].astype(BF16)
        wif = jnp.pad(w_in[l][:, 4 * mw:4 * mw + n_gate], ((0, 0), (0, LANES - n_gate))).astype(BF16)
        bif = jnp.pad(jnp.concatenate([b_i[l], b_f[l]]), (0, LANES - n_gate)).reshape(1, LANES)
        weights = (wa, wu, wif, bif, conv_hm, row(head_norm_g[l]), pool_w[l].astype(BF16),
                   row(pool_scale[l]), w_mix_out[l].astype(BF16), row(ln_mix_g[l]), row(ln_mix_b[l]))
        x = _mixer(x, weights, combine)

        x2, tail, meta, xs, tls = _xattn_router(
            x, mem, w_xq[l].astype(BF16), w_xkv[l].astype(BF16), w_xo[l].astype(BF16),
            row(ln_x_g[l]), row(ln_x_b[l]), rw2, rbias, cap, blk)

        seg_rows = meta[-1, -1, :N_GROUPS, M_END]
        blk_in, blk_grp, used = _block_tables(seg_rows, cap, blk, n_steps)
        ys = _ffn(blk_in, blk_grp, used, xs, tls, w_gate[l].astype(BF16), w_up[l].astype(BF16),
                  w_down[l].reshape(N_GROUPS, EXPERTS_PER_GROUP * w_down.shape[2], d).astype(BF16))
        x = x2
        combine = (_flat_meta(meta), tail, ys, row(ln_moe_g[l]), row(ln_moe_b[l]))

    meta, tail, ys, cg, cb = combine
    return _final_combine(meta, x.reshape(n_tok, d), tail.reshape(n_tok, TAIL_LANES), ys, cg, cb).reshape(bsz, seq, d)
```

```python
import functools

import jax
import jax.numpy as jnp
from jax import lax
from jax.experimental import pallas as pl
from jax.experimental.pallas import tpu as pltpu

F32 = jnp.float32
BF16 = jnp.bfloat16
I32 = jnp.int32

N_HEADS = 4
HEAD_DIM = 128
POOL_WINDOWS = (2, 4, 8, 16)
POOL_GROUP = 128
CONV_WIDTH = 4
XATTN_HEADS = 4
N_EXPERTS = 16
N_GROUPS = 4
EXPERTS_PER_GROUP = 4
DEPTH = 2
ALPHA = (2 * DEPTH) ** 0.25
LN_EPS = 1e-5

LANES = 128
SUBLANES = 8
BF16_TILE_ROWS = 16
VMEM_LIMIT = 56 * 1024 * 1024

SEQ_TILE = 256
MIXER_SEQS = 2
XATTN_SEQS = 4
FINAL_TILES = 2
HEADS_TOGETHER = 2
MLSTM_CHUNK = 256
FFN_BLOCK = 512
CONV_CARRY = 8
POOL_CARRY = 16

RUN_ALIGN = BF16_TILE_ROWS
STAGE_ROWS = SEQ_TILE + N_GROUPS * RUN_ALIGN
STAGE_ROWS_PADDED = 384
TAIL_LANES = 128
STAGE_LANE = 5
M_SLOT, M_LEN, M_OFF, M_TOTAL, M_END = 0, 1, 2, 3, 4
META_W = 16


def _layer_norm(y, g, b):
    mu = jnp.mean(y, axis=-1, keepdims=True)
    d = y - mu
    var = jnp.mean(d * d, axis=-1, keepdims=True)
    return d * lax.rsqrt(var + LN_EPS) * g + b


def _sigmoid(v):
    return 1.0 / (1.0 + jnp.exp(-v))


def _dot(a, b):
    return jnp.dot(a, b, preferred_element_type=F32)


def _dot_nt(a, b):
    return lax.dot_general(a, b, (((1,), (1,)), ((), ())), preferred_element_type=F32)


def _split3(v):
    hi = v.astype(BF16)
    r1 = v - hi.astype(F32)
    mid = r1.astype(BF16)
    lo = (r1 - mid.astype(F32)).astype(BF16)
    return hi, mid, lo


def _loop(n, body, unroll=1):
    lax.fori_loop(0, n, lambda j, c: (body(j), c)[1], 0, unroll=unroll)


def _aligned(v):
    return pl.multiple_of(v, RUN_ALIGN)


def _run_copy(src, src_row, dst, dst_row, length, sem):
    length = _aligned(length)

    @pl.when(length > 0)
    def _():
        pltpu.make_async_copy(src.at[pl.ds(_aligned(src_row), length)],
                              dst.at[pl.ds(_aligned(dst_row), length)], sem).start()


def _combine_fetch(meta_ref, tile, next_tile, first, has_next, slot, ys_hbm, stage, sem):
    def fetch(t, sl):
        for g in range(N_GROUPS):
            _run_copy(ys_hbm, meta_ref[t * META_W + g],
                      stage.at[sl], meta_ref[t * META_W + 2 * N_GROUPS + g],
                      meta_ref[t * META_W + N_GROUPS + g], sem.at[sl])

    @pl.when(first)
    def _():
        stage[...] = jnp.zeros(stage.shape, stage.dtype)
        fetch(tile, slot)

    @pl.when(has_next)
    def _():
        fetch(next_tile, 1 - slot)

    total = _aligned(meta_ref[tile * META_W + 3 * N_GROUPS])

    @pl.when(total > 0)
    def _():
        pltpu.make_async_copy(ys_hbm.at[pl.ds(0, total)], stage.at[slot, pl.ds(0, total)],
                              sem.at[slot]).wait()


def _combine(x2, tail, sorted_rows, lng, lnb):
    ts = x2.shape[0]
    pos = tail[:, STAGE_LANE:STAGE_LANE + 1].astype(I32)
    lane = lax.broadcasted_iota(I32, (ts, STAGE_ROWS_PADDED), 1)
    unsort = jnp.where(lane == pos, 1.0, 0.0).astype(BF16)
    return _layer_norm(ALPHA * x2 + _dot(unsort, sorted_rows), lng, lnb)


def _combine_scratch(d, lanes):
    return [pltpu.VMEM((lanes, 2, STAGE_ROWS_PADDED, d), BF16), pltpu.SemaphoreType.DMA((lanes, 2))]


def _final_kernel(meta_ref, x2_ref, tail_ref, ys_hbm, lng_ref, lnb_ref, o_ref, stage, sem):
    i = pl.program_id(0)
    ts = o_ref.shape[0] // FINAL_TILES
    for lane in range(FINAL_TILES):
        tile = i * FINAL_TILES + lane
        _combine_fetch(meta_ref, tile, tile + FINAL_TILES, i == 0, i + 1 < pl.num_programs(0), i % 2,
                       ys_hbm, stage.at[lane], sem.at[lane])
    for lane in range(FINAL_TILES):
        rows = slice(lane * ts, (lane + 1) * ts)
        o_ref[rows, :] = _combine(x2_ref[rows, :], tail_ref[rows, :], stage[lane, i % 2],
                                  lng_ref[...], lnb_ref[...])


def _final_combine(meta, x2, tail, ys, lng, lnb):
    n_tok, d = x2.shape
    ts = FINAL_TILES * (tail.shape[0] // (meta.shape[0] // META_W))
    grid_spec = pltpu.PrefetchScalarGridSpec(
        num_scalar_prefetch=1,
        grid=(n_tok // ts,),
        in_specs=[
            pl.BlockSpec((ts, d), lambda i, m: (i, 0)),
            pl.BlockSpec((ts, TAIL_LANES), lambda i, m: (i, 0)),
            pl.BlockSpec(memory_space=pl.ANY),
            pl.BlockSpec(lng.shape, lambda i, m: (0, 0)),
            pl.BlockSpec(lnb.shape, lambda i, m: (0, 0)),
        ],
        out_specs=pl.BlockSpec((ts, d), lambda i, m: (i, 0)),
        scratch_shapes=_combine_scratch(d, FINAL_TILES),
    )
    return pl.pallas_call(
        _final_kernel,
        out_shape=jax.ShapeDtypeStruct((n_tok, d), F32),
        grid_spec=grid_spec,
        compiler_params=pltpu.CompilerParams(
            dimension_semantics=("arbitrary",), vmem_limit_bytes=VMEM_LIMIT),
        name="final_combine",
    )(meta, x2, tail, ys, lng, lnb)


def _mixer_kernel(*refs, ts, lc, combine):
    n_lead = 6 if combine else 1
    zq_ext, u_carry, c_st, m_st = refs[n_lead + 12:n_lead + 16]
    b = pl.program_id(0)
    s = pl.program_id(1)
    ns = pl.num_programs(1)

    @pl.when(s == 0)
    def _():
        zq_ext[...] = jnp.zeros(zq_ext.shape, F32)
        u_carry[...] = jnp.zeros(u_carry.shape, F32)
        c_st[...] = jnp.zeros(c_st.shape, F32)
        m_st[...] = jnp.zeros(m_st.shape, F32)

    step = b * ns + s
    if combine:
        meta_ref, ys_hbm = refs[0], refs[3]
        stage, sem = refs[n_lead + 16:n_lead + 18]
        for lane in range(MIXER_SEQS):
            tile = (b * MIXER_SEQS + lane) * ns + s
            next_tile = jnp.where(s + 1 < ns, tile + 1, tile + (MIXER_SEQS - 1) * ns + 1)
            _combine_fetch(meta_ref, tile, next_tile, step == 0, step + 1 < pl.num_programs(0) * ns,
                           step % 2, ys_hbm, stage.at[lane], sem.at[lane])

    lanes = [_mixer_lane(lane, step % 2, refs, ts, lc, combine) for lane in range(MIXER_SEQS)]
    for _ in zip(*lanes):
        pass


def _mixer_lane(lane, slot, refs, ts, lc, combine):
    if combine:
        (_, x_ref, tail_ref, _, cg_ref, cb_ref), refs = refs[:6], refs[6:]
    else:
        x_ref, refs = refs[0], refs[1:]
    (wa_ref, wu_ref, wif_ref, bif_ref, conv_ref, hng_ref, poolw_ref, pscale_ref, wout_ref,
     lng_ref, lnb_ref, o_ref, zq_ext, u_carry, c_st, m_st) = refs[:16]
    zq_ext, u_carry, c_st, m_st = zq_ext.at[lane], u_carry.at[lane], c_st.at[lane], m_st.at[lane]
    s = pl.program_id(1)
    mw = N_HEADS * HEAD_DIM

    x = x_ref[lane]
    if combine:
        stage = refs[16]
        x = _combine(x, tail_ref[lane], stage[lane, slot], cg_ref[...], cb_ref[...])
    xb = x.astype(BF16)
    u = _dot(xb, wu_ref[...])
    gts = _dot(xb, wif_ref[...]) + bif_ref[...]
    yield

    cw = conv_ref[...]
    pair_w = HEADS_TOGETHER * HEAD_DIM
    projected = {}

    def conv_silu(part, cols):
        zc = _dot(xb, wa_ref[:, part * mw + cols.start:part * mw + cols.stop])
        cc = slice(part * mw + cols.start, part * mw + cols.stop)
        ze = jnp.concatenate([zq_ext[:, cc], zc], axis=0)
        zq_ext[:, cc] = zc[ts - CONV_CARRY:ts, :]
        acc = zc * cw[CONV_WIDTH - 1:CONV_WIDTH, cc]
        for j in range(1, CONV_WIDTH):
            acc = acc + pltpu.roll(ze, j, 0)[CONV_CARRY:, :] * cw[CONV_WIDTH - 1 - j:CONV_WIDTH - j, cc]
        return acc * _sigmoid(acc)

    def project(h):
        g0 = h - h % HEADS_TOGETHER
        if g0 not in projected:
            cols = slice(g0 * HEAD_DIM, g0 * HEAD_DIM + pair_w)
            projected[g0] = (conv_silu(0, cols) * (HEAD_DIM ** -0.5), conv_silu(1, cols),
                             _dot(xb, wa_ref[:, 2 * mw + cols.start:2 * mw + cols.stop]),
                             _dot(xb, wa_ref[:, 3 * mw + cols.start:3 * mw + cols.stop]))
        hs_in = slice((h - g0) * HEAD_DIM, (h - g0 + 1) * HEAD_DIM)
        return tuple(a[:, hs_in] for a in projected[g0])

    lf_all = jnp.minimum(gts, 0.0) - jnp.log1p(jnp.exp(-jnp.abs(gts)))
    yield

    row_i = lax.broadcasted_iota(I32, (lc, lc), 0)
    col_i = lax.broadcasted_iota(I32, (lc, lc), 1)
    causal = col_i <= row_i
    tri = jnp.where(causal, 1.0, 0.0).astype(BF16)
    ones_col = jnp.where(lax.broadcasted_iota(I32, (lc, HEAD_DIM), 1) == 0, 1.0, 0.0).astype(BF16)

    head_out = [[] for _ in range(N_HEADS)]
    for c in range(ts // lc):
        rows = slice(c * lc, (c + 1) * lc)
        hi, mid, lo = _split3(lf_all[rows, :])
        b_all = _dot(tri, hi) + _dot(tri, mid) + _dot(tri, lo)
        g_c = gts[rows, :]
        r_all = g_c - pltpu.roll(b_all, LANES - N_HEADS, 1)
        r_t = r_all.T
        def head(h, rows=rows, b_all=b_all, g_c=g_c, r_t=r_t):
            hs = slice(h * HEAD_DIM, (h + 1) * HEAD_DIM)
            q_h, k_h, v_h, o_h = project(h)
            yield
            qh = q_h[rows, :]
            kh = k_h[rows, :]
            vh = jnp.concatenate([v_h[rows, :].astype(BF16), ones_col], axis=1)
            qhb = qh.astype(BF16)
            bc = b_all[:, N_HEADS + h:N_HEADS + h + 1]
            igc = g_c[:, h:h + 1]
            r_row = r_t[h:h + 1, :]
            c_prev = c_st[h]
            m_prev = m_st[h][:, 0:1]
            qk = _dot_nt(qhb, kh.astype(BF16))
            qc = _dot(qhb, c_prev.astype(BF16))
            yield

            log_d = jnp.where(causal, bc + r_row, -jnp.inf)
            m_intra = jnp.max(log_d, axis=1, keepdims=True)
            log_inter = bc + m_prev
            m_t = jnp.maximum(m_intra, log_inter)
            p = jnp.exp(log_d - m_t) * qk
            inter = jnp.exp(log_inter - m_t)
            b_last = bc[lc - 1:lc, :]
            w_state = b_last - bc + igc
            m_loc = jnp.max(w_state, axis=0, keepdims=True)
            ka = kh * jnp.exp(w_state - m_loc)
            yield

            nd = _dot(p.astype(BF16), vh) + inter * qc
            c_loc = _dot(ka.T.astype(BF16), vh)
            yield

            den = nd[:, HEAD_DIM:HEAD_DIM + 1]
            hh = nd[:, :HEAD_DIM] * (1.0 / jnp.maximum(jnp.abs(den), jnp.exp(-m_t)))
            m_new = jnp.maximum(b_last + m_prev, m_loc)
            s_old = jnp.exp(b_last + m_prev - m_new)
            s_new = jnp.exp(m_loc - m_new)
            c_st[h] = s_old * c_prev + s_new * c_loc
            m_st[h] = jnp.broadcast_to(m_new, (1, LANES))

            mu = jnp.mean(hh, axis=1, keepdims=True)
            dlt = hh - mu
            var = jnp.mean(dlt * dlt, axis=1, keepdims=True)
            hn = dlt * lax.rsqrt(var + LN_EPS) * hng_ref[:, hs]
            head_out[h].append(hn * _sigmoid(o_h[rows, :]))
            yield

        for h0 in range(0, N_HEADS, HEADS_TOGETHER):
            for _ in zip(*[head(h) for h in range(h0, h0 + HEADS_TOGETHER)]):
                yield

    mixed = [jnp.concatenate(ho, axis=0) if len(ho) > 1 else ho[0] for ho in head_out]

    ue = jnp.concatenate([u_carry[...], u], axis=0)
    u_carry[...] = u[ts - POOL_CARRY:ts, :]
    pos = (lax.broadcasted_iota(I32, (ts, 1), 0) + s * ts + 1).astype(F32)
    for g, w in enumerate(POOL_WINDOWS):
        cs = slice(g * POOL_GROUP, (g + 1) * POOL_GROUP)
        win = ue[:, cs]
        shift = 1
        while shift < w:
            win = win + pltpu.roll(win, shift, 0)
            shift *= 2
        ug = u[:, cs]
        pooled = win[POOL_CARRY:, :] / jnp.minimum(pos, float(w)) - ug
        pm = _dot(pooled.astype(BF16), poolw_ref[g]) * pscale_ref[:, cs]
        mixed.append(pm)
    yield

    mixed = jnp.concatenate(mixed, axis=1).astype(BF16)
    y = _dot(mixed, wout_ref[...])
    o_ref[lane] = _layer_norm(ALPHA * x + y, lng_ref[...], lnb_ref[...])
    yield


def _mixer(x, weights, combine=None):
    bsz, seq, d = x.shape
    ts = min(SEQ_TILE, seq)
    lc = min(MLSTM_CHUNK, ts)
    ns = seq // ts
    mw = N_HEADS * HEAD_DIM
    pw = weights[1].shape[1]
    const = lambda a: pl.BlockSpec(a.shape, lambda b, s, *_: (0,) * a.ndim)
    in_specs = [pl.BlockSpec((MIXER_SEQS, ts, d), lambda b, s, *_: (b, s, 0))]
    args = [x]
    scratch = [
        pltpu.VMEM((MIXER_SEQS, CONV_CARRY, 2 * mw), F32),
        pltpu.VMEM((MIXER_SEQS, POOL_CARRY, pw), F32),
        pltpu.VMEM((MIXER_SEQS, N_HEADS, HEAD_DIM, 2 * HEAD_DIM), F32),
        pltpu.VMEM((MIXER_SEQS, N_HEADS, 1, LANES), F32),
    ]
    prefetch = []
    if combine is not None:
        meta, tail, ys, cg, cb = combine
        prefetch = [meta]
        in_specs += [pl.BlockSpec((MIXER_SEQS, ts, TAIL_LANES), lambda b, s, *_: (b, s, 0)),
                     pl.BlockSpec(memory_space=pl.ANY), const(cg), const(cb)]
        args += [tail, ys, cg, cb]
        scratch += _combine_scratch(d, MIXER_SEQS)
    in_specs += [const(w) for w in weights]
    args += list(weights)
    grid_spec = pltpu.PrefetchScalarGridSpec(
        num_scalar_prefetch=len(prefetch),
        grid=(bsz // MIXER_SEQS, ns),
        in_specs=in_specs,
        out_specs=pl.BlockSpec((MIXER_SEQS, ts, d), lambda b, s, *_: (b, s, 0)),
        scratch_shapes=scratch,
    )
    return pl.pallas_call(
        functools.partial(_mixer_kernel, ts=ts, lc=lc, combine=combine is not None),
        out_shape=jax.ShapeDtypeStruct((bsz, seq, d), F32),
        grid_spec=grid_spec,
        compiler_params=pltpu.CompilerParams(
            dimension_semantics=("arbitrary", "arbitrary"), vmem_limit_bytes=VMEM_LIMIT),
        name="mixer",
    )(*prefetch, *args)


def _top2_sum(a, b, c, d):
    hi1, lo1 = jnp.maximum(a, b), jnp.minimum(a, b)
    hi2, lo2 = jnp.maximum(c, d), jnp.minimum(c, d)
    return jnp.maximum(hi1, hi2) + jnp.maximum(jnp.minimum(hi1, hi2), jnp.maximum(lo1, lo2))


def _xattn_kernel(x_ref, mem_ref, wq_ref, wkv_ref, wo_ref, lng_ref, lnb_ref,
                  rw2_ref, rbias_ref,
                  x2_ref, tail_ref, meta_ref, xs_hbm, tls_hbm,
                  k_scr, v_scr, carry, stx, stt, mvec, msm, prev_total, zx, zt, sem, ssem, zsem,
                  *, ts, cap, blk):
    b = pl.program_id(0)
    s = pl.program_id(1)
    step = b * pl.num_programs(1) + s
    last = step == pl.num_programs(0) * pl.num_programs(1) - 1
    d = x_ref.shape[2]

    @pl.when(s == 0)
    def _():
        for lane in range(XATTN_SEQS):
            kv = _dot(mem_ref[lane].astype(BF16), wkv_ref[...])
            k_scr[lane] = kv[:, :d].astype(BF16)
            v_scr[lane] = kv[:, d:].astype(BF16)

    @pl.when(step == 0)
    def _():
        carry[...] = jnp.zeros(carry.shape, F32)

    results = [None] * XATTN_SEQS
    lanes = [_xattn_lane(lane, results, x_ref, wq_ref, wo_ref, lng_ref, lnb_ref, rw2_ref, rbias_ref,
                         x2_ref, tail_ref, k_scr, v_scr, ts) for lane in range(XATTN_SEQS)]
    for _ in zip(*lanes):
        pass

    sub1 = lax.broadcasted_iota(I32, (SUBLANES, 1), 0)
    lane8 = lax.broadcasted_iota(I32, (SUBLANES, LANES), 1)
    base = carry[:, 0:1]
    for lane in range(XATTN_SEQS):
        len8, off8, total, _, _ = results[lane]
        new_base = base + len8
        mv = jnp.where(lane8 == M_SLOT, sub1.astype(F32) * float(cap) + base, 0.0)
        mv = jnp.where(lane8 == M_LEN, len8, mv)
        mv = jnp.where(lane8 == M_OFF, off8, mv)
        mv = jnp.where(lane8 == M_TOTAL, total, mv)
        mv = jnp.where(lane8 == M_END, new_base, mv).astype(I32)
        meta_ref[lane, 0] = mv
        mvec[lane * SUBLANES:(lane + 1) * SUBLANES, :] = mv
        base = new_base
    carry[...] = jnp.broadcast_to(base, carry.shape)

    def wait_runs(lane, n_rows):
        pltpu.make_async_copy(stx.at[lane, pl.ds(0, n_rows)], xs_hbm.at[pl.ds(0, n_rows)],
                              sem.at[lane, 0]).wait()
        pltpu.make_async_copy(stt.at[lane, pl.ds(0, n_rows)], tls_hbm.at[pl.ds(0, n_rows)],
                              sem.at[lane, 1]).wait()

    @pl.when(step > 0)
    def _():
        for lane in range(XATTN_SEQS):
            n_prev = _aligned(prev_total[lane])

            @pl.when(n_prev > 0)
            def _(lane=lane, n_prev=n_prev):
                wait_runs(lane, n_prev)

    for lane in range(XATTN_SEQS):
        stx[lane] = results[lane][3]
        stt[lane] = results[lane][4]
    to_smem = pltpu.make_async_copy(mvec, msm, ssem)
    to_smem.start()
    to_smem.wait()

    for lane in range(XATTN_SEQS):
        for g in range(N_GROUPS):
            r = lane * SUBLANES + g
            _run_copy(stx.at[lane], msm[r, M_OFF], xs_hbm, msm[r, M_SLOT], msm[r, M_LEN], sem.at[lane, 0])
            _run_copy(stt.at[lane], msm[r, M_OFF], tls_hbm, msm[r, M_SLOT], msm[r, M_LEN], sem.at[lane, 1])
        prev_total[lane] = msm[lane * SUBLANES, M_TOTAL]

    @pl.when(last)
    def _():
        for lane in range(XATTN_SEQS):
            n_own = _aligned(msm[lane * SUBLANES, M_TOTAL])

            @pl.when(n_own > 0)
            def _(lane=lane, n_own=n_own):
                wait_runs(lane, n_own)

        zx[...] = jnp.zeros(zx.shape, BF16)
        zt[...] = jnp.zeros(zt.shape, F32)
        for g in range(N_GROUPS):
            end = msm[(XATTN_SEQS - 1) * SUBLANES + g, M_END]
            n_pad = _aligned((blk - end % blk) % blk)
            _run_copy(zx, 0, xs_hbm, g * cap + end, n_pad, zsem.at[0])
            _run_copy(zt, 0, tls_hbm, g * cap + end, n_pad, zsem.at[1])

            @pl.when(n_pad > 0)
            def _(n_pad=n_pad):
                pltpu.make_async_copy(zx.at[pl.ds(0, n_pad)], xs_hbm.at[pl.ds(0, n_pad)], zsem.at[0]).wait()
                pltpu.make_async_copy(zt.at[pl.ds(0, n_pad)], tls_hbm.at[pl.ds(0, n_pad)], zsem.at[1]).wait()


def _xattn_lane(lane, results, x_ref, wq_ref, wo_ref, lng_ref, lnb_ref, rw2_ref, rbias_ref,
                x2_ref, tail_ref, k_scr, v_scr, ts):
    d = x_ref.shape[2]
    dh = d // XATTN_HEADS
    x = x_ref[lane]
    q = (_dot(x.astype(BF16), wq_ref[...]) * (dh ** -0.5)).astype(BF16)
    yield
    outs = []
    for h in range(XATTN_HEADS):
        hs = slice(h * dh, (h + 1) * dh)
        sc = _dot_nt(q[:, hs], k_scr[lane, :, hs])
        e = jnp.exp(sc - jnp.max(sc, axis=1, keepdims=True))
        l = jnp.sum(e, axis=1, keepdims=True)
        outs.append(_dot(e.astype(BF16), v_scr[lane, :, hs]) * (1.0 / l))
        yield
    o = jnp.concatenate(outs, axis=1).astype(BF16)
    x2 = _layer_norm(ALPHA * x + _dot(o, wo_ref[...]), lng_ref[...], lnb_ref[...])
    x2_ref[lane] = x2
    yield

    xh, xm, _ = _split3(x2)
    both = _dot(xh, rw2_ref[...])
    logits = (both[:, :LANES] + both[:, LANES:]) + _dot(xm, rw2_ref[:, :LANES])
    yield
    lt = logits.T[0:N_EXPERTS, :]
    score = _sigmoid(lt)
    sel = score + rbias_ref[...]

    sel_r = [sel[e:e + 1, :] for e in range(N_EXPERTS)]
    score_r = [score[e:e + 1, :] for e in range(N_EXPERTS)]
    gs = [_top2_sum(*sel_r[EXPERTS_PER_GROUP * g:EXPERTS_PER_GROUP * (g + 1)]) for g in range(N_GROUPS)]
    best = jnp.zeros((1, ts), I32)
    bestv = gs[0]
    for g in range(1, N_GROUPS):
        better = gs[g] > bestv
        best = jnp.where(better, g, best)
        bestv = jnp.where(better, gs[g], bestv)
    in_g = [best == g for g in range(N_GROUPS)]

    def pick(rows, j):
        out = rows[j]
        for g in range(1, N_GROUPS):
            out = jnp.where(in_g[g], rows[EXPERTS_PER_GROUP * g + j], out)
        return out

    vsel = [pick(sel_r, j) for j in range(EXPERTS_PER_GROUP)]
    vsc = [pick(score_r, j) for j in range(EXPERTS_PER_GROUP)]
    gates = []
    for j in range(EXPERTS_PER_GROUP):
        beaten = jnp.zeros((1, ts), I32)
        for k in range(EXPERTS_PER_GROUP):
            if k == j:
                continue
            wins = (vsel[k] > vsel[j]) | ((vsel[k] == vsel[j]) & (k < j))
            beaten = beaten + wins.astype(I32)
        gates.append(jnp.where(beaten < 2, vsc[j], 0.0))
    gsum = gates[0] + gates[1] + gates[2] + gates[3]
    gates = [g / gsum for g in gates]

    sub = lax.broadcasted_iota(I32, (SUBLANES, ts), 0)
    oh8 = jnp.zeros((SUBLANES, ts), F32)
    for g in range(N_GROUPS):
        oh8 = jnp.where((sub == g) & in_g[g], 1.0, oh8)
    r_i = lax.broadcasted_iota(I32, (ts, ts), 0)
    c_i = lax.broadcasted_iota(I32, (ts, ts), 1)
    upper = jnp.where(r_i < c_i, 1.0, 0.0).astype(BF16)
    excl = _dot(oh8.astype(BF16), upper)
    n8 = jnp.sum(oh8, axis=1, keepdims=True)
    len8 = jnp.floor((n8 + (RUN_ALIGN - 1)) * (1.0 / RUN_ALIGN)) * RUN_ALIGN
    sub1 = lax.broadcasted_iota(I32, (SUBLANES, 1), 0)
    off8 = jnp.zeros((SUBLANES, 1), F32)
    run_off = jnp.zeros((1, 1), F32)
    for g in range(N_GROUPS):
        off8 = jnp.where(sub1 == g, run_off, off8)
        run_off = run_off + len8[g:g + 1, :]
    pos = jnp.sum(jnp.where(oh8 > 0.0, off8 + excl, 0.0), axis=0, keepdims=True)

    t8 = jnp.where(sub == STAGE_LANE, pos, 0.0)
    for j in range(EXPERTS_PER_GROUP):
        t8 = jnp.where(sub == j, gates[j], t8)
    tail = jnp.concatenate([t8, jnp.zeros((TAIL_LANES - SUBLANES, ts), F32)], axis=0).T
    tail_ref[lane] = tail
    yield

    srow = lax.broadcasted_iota(I32, (STAGE_ROWS, ts), 0)
    sort = jnp.where(srow == pos.astype(I32), 1.0, 0.0).astype(BF16)
    xs_sorted = _dot(sort, xh).astype(BF16)
    pieces = _dot(sort, jnp.concatenate(_split3(tail), axis=1))
    tail_sorted = (pieces[:, :LANES] + pieces[:, LANES:2 * LANES]) + pieces[:, 2 * LANES:]
    results[lane] = (len8, off8, run_off, xs_sorted, tail_sorted)
    yield


def _xattn_router(x, mem, wq, wkv, wo, lng, lnb, rw2, rbias, cap, blk):
    bsz, seq, d = x.shape
    mlen = mem.shape[1]
    ts = min(SEQ_TILE, seq)
    assert ts == SEQ_TILE and blk % RUN_ALIGN == 0
    ns = seq // ts
    nl = XATTN_SEQS
    const = lambda a: pl.BlockSpec(a.shape, lambda b, s: (0,) * a.ndim)
    zero_rows = blk
    return pl.pallas_call(
        functools.partial(_xattn_kernel, ts=ts, cap=cap, blk=blk),
        out_shape=(
            jax.ShapeDtypeStruct((bsz, seq, d), F32),
            jax.ShapeDtypeStruct((bsz, seq, TAIL_LANES), F32),
            jax.ShapeDtypeStruct((bsz, ns, SUBLANES, LANES), I32),
            jax.ShapeDtypeStruct((N_GROUPS * cap, d), BF16),
            jax.ShapeDtypeStruct((N_GROUPS * cap, TAIL_LANES), F32),
        ),
        grid=(bsz // nl, ns),
        in_specs=[
            pl.BlockSpec((nl, ts, d), lambda b, s: (b, s, 0)),
            pl.BlockSpec((nl, mlen, d), lambda b, s: (b, 0, 0)),
            const(wq), const(wkv), const(wo), const(lng), const(lnb),
            const(rw2), const(rbias),
        ],
        out_specs=(
            pl.BlockSpec((nl, ts, d), lambda b, s: (b, s, 0)),
            pl.BlockSpec((nl, ts, TAIL_LANES), lambda b, s: (b, s, 0)),
            pl.BlockSpec((nl, 1, SUBLANES, LANES), lambda b, s: (b, s, 0, 0)),
            pl.BlockSpec(memory_space=pl.ANY),
            pl.BlockSpec(memory_space=pl.ANY),
        ),
        scratch_shapes=[
            pltpu.VMEM((nl, mlen, d), BF16),
            pltpu.VMEM((nl, mlen, d), BF16),
            pltpu.VMEM((SUBLANES, LANES), F32),
            pltpu.VMEM((nl, STAGE_ROWS, d), BF16),
            pltpu.VMEM((nl, STAGE_ROWS, TAIL_LANES), F32),
            pltpu.VMEM((nl * SUBLANES, LANES), I32),
            pltpu.SMEM((nl * SUBLANES, LANES), I32),
            pltpu.SMEM((nl,), I32),
            pltpu.VMEM((zero_rows, d), BF16),
            pltpu.VMEM((zero_rows, TAIL_LANES), F32),
            pltpu.SemaphoreType.DMA((nl, 2)),
            pltpu.SemaphoreType.DMA,
            pltpu.SemaphoreType.DMA((2,)),
        ],
        compiler_params=pltpu.CompilerParams(
            dimension_semantics=("arbitrary", "arbitrary"), vmem_limit_bytes=VMEM_LIMIT,
            has_side_effects=True),
        name="xattn_router",
    )(x, mem, wq, wkv, wo, lng, lnb, rw2, rbias)


def _ffn_kernel(blk_in_ref, grp_ref, used_ref, xs_ref, tl_ref, wg_ref, wu_ref, wd_ref, o_ref):
    @pl.when(used_ref[pl.program_id(0)] == 1)
    def _():
        xb = xs_ref[...]
        parts = []
        for j in range(EXPERTS_PER_GROUP):
            hg = _dot(xb, wg_ref[j])
            hu = _dot(xb, wu_ref[j])
            gate = tl_ref[:, j:j + 1]
            parts.append(jnp.where(gate != 0.0, hg * _sigmoid(hg) * hu * gate, 0.0))
        hid = jnp.concatenate(parts, axis=1).astype(BF16)
        o_ref[...] = _dot(hid, wd_ref[0]).astype(BF16)


def _ffn(blk_in, blk_grp, used, xs, tls, wg, wu, wd):
    d = xs.shape[1]
    blk = FFN_BLOCK
    grid_spec = pltpu.PrefetchScalarGridSpec(
        num_scalar_prefetch=3,
        grid=(blk_in.shape[0],),
        in_specs=[
            pl.BlockSpec((blk, d), lambda i, bi, grp, us: (bi[i], 0)),
            pl.BlockSpec((blk, TAIL_LANES), lambda i, bi, grp, us: (bi[i], 0)),
            pl.BlockSpec((EXPERTS_PER_GROUP,) + wg.shape[1:], lambda i, bi, grp, us: (grp[i], 0, 0)),
            pl.BlockSpec((EXPERTS_PER_GROUP,) + wu.shape[1:], lambda i, bi, grp, us: (grp[i], 0, 0)),
            pl.BlockSpec((1,) + wd.shape[1:], lambda i, bi, grp, us: (grp[i], 0, 0)),
        ],
        out_specs=pl.BlockSpec((blk, d), lambda i, bi, grp, us: (bi[i], 0)),
    )
    return pl.pallas_call(
        _ffn_kernel,
        out_shape=jax.ShapeDtypeStruct(xs.shape, BF16),
        grid_spec=grid_spec,
        compiler_params=pltpu.CompilerParams(
            dimension_semantics=("arbitrary",), vmem_limit_bytes=VMEM_LIMIT),
        name="group_ffn",
    )(blk_in, blk_grp, used, xs, tls, wg, wu, wd)


def _block_tables(seg_rows, cap, blk, n_steps):
    nblk = (seg_rows + blk - 1) // blk
    bend = jnp.cumsum(nblk)
    bstart = bend - nblk
    step = jnp.arange(n_steps, dtype=I32)
    used = step < bend[-1]
    grp = jnp.minimum(jnp.sum(step[:, None] >= bend[None, :], axis=1), N_GROUPS - 1).astype(I32)
    blk_in = grp * (cap // blk) + step - bstart[grp]
    last_real = jnp.maximum(bend[-1] - 1, 0)
    blk_in = jnp.where(used, blk_in, blk_in[last_real])
    grp = jnp.where(used, grp, grp[last_real])
    return blk_in.astype(I32), grp.astype(I32), used.astype(I32)


def _flat_meta(meta):
    m = meta.reshape(-1, SUBLANES, LANES)[:, :N_GROUPS, :]
    rec = jnp.concatenate([m[:, :, M_SLOT], m[:, :, M_LEN], m[:, :, M_OFF], m[:, :1, M_TOTAL],
                           jnp.zeros((m.shape[0], META_W - 3 * N_GROUPS - 1), I32)], axis=1)
    return rec.reshape(-1)


def kernel(x, mem, w_in, b_i, b_f, conv_qk, head_norm_g, pool_w, pool_scale, w_mix_out,
           ln_mix_g, ln_mix_b, w_xq, w_xkv, w_xo, ln_x_g, ln_x_b, router_w, router_bias,
           w_gate, w_up, w_down, ln_moe_g, ln_moe_b):
    bsz, seq, d = x.shape
    n_tok = bsz * seq
    n_tiles = n_tok // SEQ_TILE
    mw = N_HEADS * HEAD_DIM
    n_gate = 2 * N_HEADS
    blk = FFN_BLOCK
    cap = -(-(n_tok + RUN_ALIGN * n_tiles) // blk) * blk
    n_steps = (n_tok + N_GROUPS * RUN_ALIGN * n_tiles) // blk + N_GROUPS

    rw = jnp.pad(router_w, ((0, 0), (0, LANES - N_EXPERTS)))
    rwh, rwm, _ = _split3(rw)
    rw2 = jnp.concatenate([rwh, rwm], axis=1)
    rbias = router_bias.reshape(N_EXPERTS, 1).astype(F32)
    row = lambda v: v.reshape(1, -1).astype(F32)

    combine = None
    for l in range(DEPTH):
        wa = w_in[l][:, :4 * mw].astype(BF16)
        wu = w_in[l][:, 4 * mw + n_gate:].astype(BF16)
        wif = jnp.pad(w_in[l][:, 4 * mw:4 * mw + n_gate], ((0, 0), (0, LANES - n_gate))).astype(BF16)
        bif = jnp.pad(jnp.concatenate([b_i[l], b_f[l]]), (0, LANES - n_gate)).reshape(1, LANES)
        weights = (wa, wu, wif, bif, conv_qk[l], row(head_norm_g[l]), pool_w[l].astype(BF16),
                   row(pool_scale[l]), w_mix_out[l].astype(BF16), row(ln_mix_g[l]), row(ln_mix_b[l]))
        x = _mixer(x, weights, combine)

        x2, tail, meta, xs, tls = _xattn_router(
            x, mem, w_xq[l].astype(BF16), w_xkv[l].astype(BF16), w_xo[l].astype(BF16),
            row(ln_x_g[l]), row(ln_x_b[l]), rw2, rbias, cap, blk)

        seg_rows = meta[-1, -1, :N_GROUPS, M_END]
        blk_in, blk_grp, used = _block_tables(seg_rows, cap, blk, n_steps)
        ys = _ffn(blk_in, blk_grp, used, xs, tls, w_gate[l].astype(BF16), w_up[l].astype(BF16),
                  w_down[l].reshape(N_GROUPS, EXPERTS_PER_GROUP * w_down.shape[2], d).astype(BF16))
        x = x2
        combine = (_flat_meta(meta), tail, ys, row(ln_moe_g[l]), row(ln_moe_b[l]))

    meta, tail, ys, cg, cb = combine
    return _final_combine(meta, x.reshape(n_tok, d), tail.reshape(n_tok, TAIL_LANES), ys, cg, cb).reshape(bsz, seq, d)
```
